```python
import jax, jax.numpy as jnp
from jax import lax
import numpy as np

D_MODEL = 1024
BATCH = 8
SEQ = 8192
DEPTH = 1

HEAD_DIM = 64
N_HEADS_DIL = 8
N_HEADS_RET = 8
W_DIL = N_HEADS_DIL * HEAD_DIM
W_RET = N_HEADS_RET * HEAD_DIM
MIX_WIDTH = W_DIL + W_RET
N_IN = 3 * W_DIL + 4 * W_RET
IN_SPLITS = (W_DIL, 2 * W_DIL, 3 * W_DIL, 3 * W_DIL + W_RET, 3 * W_DIL + 2 * W_RET, 3 * W_DIL + 3 * W_RET)
DILATED_PATTERNS = ((128, 1), (512, 4), (2048, 16))
BAND_BLOCK = 128
ROPE_THETA = 500000.0
ROT_DIM = HEAD_DIM // 4
RET_THETA = 10000.0
RET_CHUNK = 128
N_MEM = 256
N_HEADS_MEM = 4
MEM_HEAD_DIM = D_MODEL // N_HEADS_MEM
N_EXPERTS = 32
TOP_K = 4
D_FF_EXPERT = D_MODEL
SWIGLU_LIMIT = 7.0
SWIGLU_ALPHA = 1.702
MOE_BLOCK = 256
LN_EPS = 1e-5
DEEPNORM_ALPHA = (2 * DEPTH) ** 0.25
DEEPNORM_BETA = (8 * DEPTH) ** -0.25

kernel_name = "hybrid_dilated_retention_moe_block"


def layer_norm(x, g, b):
    xf = x.astype(jnp.float32)
    mu = jnp.mean(xf, axis=-1, keepdims=True)
    var = jnp.mean(jnp.square(xf - mu), axis=-1, keepdims=True)
    y = (xf - mu) * lax.rsqrt(var + LN_EPS)
    return (y * g.astype(jnp.float32) + b.astype(jnp.float32)).astype(x.dtype)


def rotary(x, theta, rot_dim):
    S = x.shape[-2]
    half = rot_dim // 2
    inv = 1.0 / (theta ** (jnp.arange(half, dtype=jnp.float32) / half))
    ang = jnp.arange(S, dtype=jnp.float32)[:, None] * inv[None, :]
    cos, sin = jnp.cos(ang), jnp.sin(ang)
    xr = x[..., :rot_dim].astype(jnp.float32)
    x1, x2 = xr[..., :half], xr[..., half:]
    rot = jnp.concatenate([x1 * cos - x2 * sin, x1 * sin + x2 * cos], axis=-1).astype(x.dtype)
    return jnp.concatenate([rot, x[..., rot_dim:]], axis=-1)


def to_heads(t, n_heads):
    B, S, _ = t.shape
    return t.reshape(B, S, n_heads, -1).transpose(0, 2, 1, 3)


def from_heads(t):
    B, H, S, hd = t.shape
    return t.transpose(0, 2, 1, 3).reshape(B, S, H * hd)


def band_attention(q, k, v, window):
    L, hd = q.shape[-2], q.shape[-1]
    nb = -(-L // BAND_BLOCK)
    Lp = nb * BAND_BLOCK
    pad = [(0, 0)] * (q.ndim - 2) + [(0, Lp - L), (0, 0)]
    q, k, v = jnp.pad(q, pad), jnp.pad(k, pad), jnp.pad(v, pad)
    lead = q.shape[:-2]
    blk = lambda t: t.reshape(*lead, nb, BAND_BLOCK, hd)
    qb, kb, vb = blk(q), blk(k), blk(v)

    def with_prev(t):
        prev = jnp.concatenate([jnp.zeros_like(t[..., :1, :, :]), t[..., :-1, :, :]], axis=-3)
        return jnp.concatenate([prev, t], axis=-2)

    kk, vv = with_prev(kb), with_prev(vb)
    s = jnp.einsum('...nqd,...nkd->...nqk', qb, kk).astype(jnp.float32)
    qi = jnp.arange(BAND_BLOCK)[:, None]
    kj = jnp.arange(2 * BAND_BLOCK)[None, :]
    dist = qi + BAND_BLOCK - kj
    bidx = jnp.arange(nb)[:, None, None]
    valid = (dist >= 0) & (dist <= window) & (bidx * BAND_BLOCK + kj - BAND_BLOCK >= 0)
    s = jnp.where(valid, s, -jnp.inf)
    m = jnp.max(s, axis=-1, keepdims=True)
    p = jnp.exp(s - m)
    l = jnp.sum(p, axis=-1, keepdims=True)
    o = jnp.einsum('...nqk,...nkd->...nqd', p, vv.astype(jnp.float32)) / l
    lse = (m + jnp.log(l))[..., 0]
    o = o.reshape(*lead, Lp, hd)[..., :L, :]
    lse = lse.reshape(*lead, Lp)[..., :L]
    return o, lse


def dilated_attention(q, k, v):
    B, H, S, hd = q.shape
    outs, lses = [], []
    for w, d in DILATED_PATTERNS:
        split = lambda t: t.reshape(B, H, S // d, d, hd).swapaxes(2, 3)
        o, lse = band_attention(split(q), split(k), split(v), w // d)
        outs.append(o.swapaxes(2, 3).reshape(B, H, S, hd))
        lses.append(lse.swapaxes(2, 3).reshape(B, H, S))
    wts = jax.nn.softmax(jnp.stack(lses), axis=0)
    return jnp.einsum('gbhs,gbhsd->bhsd', wts, jnp.stack(outs))


def retention(q, k, v):
    B, H, S, hd = q.shape
    C = RET_CHUNK
    n = S // C
    log_g = jnp.log1p(-(2.0 ** (-5.0 - jnp.arange(H, dtype=jnp.float32))))
    qc = q.reshape(B, H, n, C, hd).astype(jnp.float32)
    kc = k.reshape(B, H, n, C, hd).astype(jnp.float32)
    vc = v.reshape(B, H, n, C, hd).astype(jnp.float32)
    idx = jnp.arange(C, dtype=jnp.float32)
    diff = idx[:, None] - idx[None, :]
    inner_decay = jnp.exp(log_g[:, None, None] * jnp.maximum(diff, 0.0)) * (diff >= 0)
    s = jnp.einsum('bhnqd,bhnkd->bhnqk', qc, kc) * inner_decay[None, :, None]
    inner = jnp.einsum('bhnqk,bhnkd->bhnqd', s, vc)
    k_decay = jnp.exp(log_g[:, None] * (C - 1 - idx))
    kv = jnp.einsum('bhnkd,bhnke->nbhde', kc * k_decay[None, :, None, :, None], vc)
    chunk_decay = jnp.exp(log_g * C)[None, :, None, None]

    def step(state, kv_i):
        return state * chunk_decay + kv_i, state

    _, prev = lax.scan(step, jnp.zeros((B, H, hd, hd), jnp.float32), kv)
    q_decay = jnp.exp(log_g[:, None] * (idx + 1.0))
    cross = jnp.einsum('bhnqd,nbhde->bhnqe', qc * q_decay[None, :, None, :, None], prev)
    return (inner + cross).reshape(B, H, S, hd)


def head_norm(y, g):
    mu = jnp.mean(y, axis=-1, keepdims=True)
    var = jnp.mean(jnp.square(y - mu), axis=-1, keepdims=True)
    return (y - mu) * lax.rsqrt(var + LN_EPS) * g.astype(jnp.float32)[None, :, None, :]


def memory_cross_attention(x, mem, wq, wk, wv, wo):
    B, S, D = x.shape
    M = mem.shape[1]
    q = (x @ wq).reshape(B, S, N_HEADS_MEM, MEM_HEAD_DIM)
    k = (mem @ wk).reshape(B, M, N_HEADS_MEM, MEM_HEAD_DIM)
    v = (mem @ wv).reshape(B, M, N_HEADS_MEM, MEM_HEAD_DIM)
    s = jnp.einsum('bshd,bmhd->bhsm', q, k).astype(jnp.float32) * (MEM_HEAD_DIM ** -0.5)
    p = jax.nn.softmax(s, axis=-1)
    o = jnp.einsum('bhsm,bmhd->bshd', p, v.astype(jnp.float32)).reshape(B, S, D).astype(x.dtype)
    return o @ wo


def moe(x, router_w, router_b, w_gate, b_gate, w_up, b_up, w_down, b_down):
    B, S, D = x.shape
    N = B * S
    xf = x.reshape(N, D)
    logits = (xf @ router_w).astype(jnp.float32) + router_b.astype(jnp.float32)
    top_val, top_idx = lax.top_k(logits, TOP_K)
    top_w = jax.nn.softmax(top_val, axis=-1)
    A = N * TOP_K
    flat_e = top_idx.reshape(A)
    flat_tok = jnp.arange(A, dtype=jnp.int32) // TOP_K
    flat_w = top_w.reshape(A)
    order = jnp.argsort(flat_e)
    sorted_e = flat_e[order]
    counts = jnp.bincount(flat_e, length=N_EXPERTS)
    starts = jnp.cumsum(counts) - counts
    padded = (counts + MOE_BLOCK - 1) // MOE_BLOCK * MOE_BLOCK
    padded_end = jnp.cumsum(padded)
    padded_start = padded_end - padded
    dest = padded_start[sorted_e] + jnp.arange(A) - starts[sorted_e]
    n_blocks = -(-A // MOE_BLOCK) + N_EXPERTS
    P = n_blocks * MOE_BLOCK
    slot_tok = jnp.zeros((P,), jnp.int32).at[dest].set(flat_tok[order])
    slot_w = jnp.zeros((P,), jnp.float32).at[dest].set(flat_w[order])
    block_e = jnp.minimum(jnp.searchsorted(padded_end, jnp.arange(n_blocks) * MOE_BLOCK, side='right'),
                          N_EXPERTS - 1)

    def step(y, blk):
        e, tok, wt = blk
        xb = xf[tok]
        gate = jnp.minimum(xb @ w_gate[e] + b_gate[e], SWIGLU_LIMIT)
        up = jnp.clip(xb @ w_up[e] + b_up[e], -SWIGLU_LIMIT, SWIGLU_LIMIT)
        hmid = gate * jax.nn.sigmoid(SWIGLU_ALPHA * gate) * (up + 1.0)
        out = hmid @ w_down[e] + b_down[e]
        return y.at[tok].add((out * wt[:, None]).astype(y.dtype)), None

    y, _ = lax.scan(step, jnp.zeros_like(xf),
                    (block_e, slot_tok.reshape(n_blocks, MOE_BLOCK), slot_w.reshape(n_blocks, MOE_BLOCK)))
    return y.reshape(B, S, D)


def setup_inputs(seed: int = 0) -> dict:
    key = jax.random.key(seed)
    ks = jax.random.split(key, 24)
    L = DEPTH
    nrm = lambda k, shape, scale: jax.random.normal(k, shape, jnp.float32) * scale
    gain = lambda k, shape: 1.0 + nrm(k, shape, 0.02)
    col_scale = jnp.concatenate([
        jnp.ones((2 * W_DIL,), jnp.float32), jnp.full((W_DIL,), DEEPNORM_BETA, jnp.float32),
        jnp.ones((2 * W_RET,), jnp.float32), jnp.full((W_RET,), DEEPNORM_BETA, jnp.float32),
        jnp.ones((W_RET,), jnp.float32)])
    return {
        "x": nrm(ks[0], (BATCH, SEQ, D_MODEL), 1.0),
        "mem": nrm(ks[1], (BATCH, N_MEM, D_MODEL), 1.0),
        "w_in": nrm(ks[2], (L, D_MODEL, N_IN), D_MODEL ** -0.5) * col_scale,
        "ret_norm_g": gain(ks[3], (L, N_HEADS_RET, HEAD_DIM)),
        "w_out": nrm(ks[4], (L, MIX_WIDTH, D_MODEL), MIX_WIDTH ** -0.5 * DEEPNORM_BETA),
        "ln1_g": gain(ks[5], (L, D_MODEL)),
        "ln1_b": nrm(ks[6], (L, D_MODEL), 0.02),
        "mem_wq": nrm(ks[7], (L, D_MODEL, D_MODEL), D_MODEL ** -0.5),
        "mem_wk": nrm(ks[8], (L, D_MODEL, D_MODEL), D_MODEL ** -0.5),
        "mem_wv": nrm(ks[9], (L, D_MODEL, D_MODEL), D_MODEL ** -0.5 * DEEPNORM_BETA),
        "mem_wo": nrm(ks[10], (L, D_MODEL, D_MODEL), D_MODEL ** -0.5 * DEEPNORM_BETA),
        "ln2_g": gain(ks[11], (L, D_MODEL)),
        "ln2_b": nrm(ks[12], (L, D_MODEL), 0.02),
        "router_w": nrm(ks[13], (L, D_MODEL, N_EXPERTS), D_MODEL ** -0.5),
        "router_b": nrm(ks[14], (L, N_EXPERTS), 0.01),
        "w_gate": nrm(ks[15], (L, N_EXPERTS, D_MODEL, D_FF_EXPERT), D_MODEL ** -0.5 * DEEPNORM_BETA),
        "b_gate": nrm(ks[16], (L, N_EXPERTS, D_FF_EXPERT), 0.02),
        "w_up": nrm(ks[17], (L, N_EXPERTS, D_MODEL, D_FF_EXPERT), D_MODEL ** -0.5 * DEEPNORM_BETA),
        "b_up": nrm(ks[18], (L, N_EXPERTS, D_FF_EXPERT), 0.02),
        "w_down": nrm(ks[19], (L, N_EXPERTS, D_FF_EXPERT, D_MODEL), D_FF_EXPERT ** -0.5 * DEEPNORM_BETA),
        "b_down": nrm(ks[20], (L, N_EXPERTS, D_MODEL), 0.02),
        "ln3_g": gain(ks[21], (L, D_MODEL)),
        "ln3_b": nrm(ks[22], (L, D_MODEL), 0.02),
    }


def reference(x, mem, w_in, ret_norm_g, w_out, ln1_g, ln1_b, mem_wq, mem_wk, mem_wv, mem_wo,
              ln2_g, ln2_b, router_w, router_b, w_gate, b_gate, w_up, b_up, w_down, b_down,
              ln3_g, ln3_b):
    for l in range(DEPTH):
        h = x @ w_in[l]
        qa, ka, va, qr, kr, vr, gr = jnp.split(h, IN_SPLITS, axis=-1)
        qa = rotary(to_heads(qa, N_HEADS_DIL), ROPE_THETA, ROT_DIM) * (HEAD_DIM ** -0.5)
        ka = rotary(to_heads(ka, N_HEADS_DIL), ROPE_THETA, ROT_DIM)
        oa = dilated_attention(qa, ka, to_heads(va, N_HEADS_DIL))
        qr = rotary(to_heads(qr, N_HEADS_RET), RET_THETA, HEAD_DIM)
        kr = rotary(to_heads(kr, N_HEADS_RET), RET_THETA, HEAD_DIM) * (HEAD_DIM ** -0.5)
        orr = head_norm(retention(qr, kr, to_heads(vr, N_HEADS_RET)), ret_norm_g[l])
        orr = from_heads(orr) * jax.nn.silu(gr.astype(jnp.float32))
        mixed = jnp.concatenate([from_heads(oa), orr], axis=-1).astype(x.dtype) @ w_out[l]
        x = layer_norm(DEEPNORM_ALPHA * x + mixed, ln1_g[l], ln1_b[l])
        c = memory_cross_attention(x, mem, mem_wq[l], mem_wk[l], mem_wv[l], mem_wo[l])
        x = layer_norm(DEEPNORM_ALPHA * x + c, ln2_g[l], ln2_b[l])
        f = moe(x, router_w[l], router_b[l], w_gate[l], b_gate[l], w_up[l], b_up[l], w_down[l], b_down[l])
        x = layer_norm(DEEPNORM_ALPHA * x + f, ln3_g[l], ln3_b[l])
    return x
```

```python
import functools

import jax
import jax.numpy as jnp
from jax import lax
from jax.experimental import pallas as pl
from jax.experimental.pallas import tpu as pltpu

F32 = jnp.float32
BF16 = jnp.bfloat16

HEAD_DIM = 64
N_HEADS_DIL = 8
N_HEADS_RET = 8
W_DIL = N_HEADS_DIL * HEAD_DIM
W_RET = N_HEADS_RET * HEAD_DIM
N_IN = 3 * W_DIL + 4 * W_RET
DILATED_PATTERNS = ((128, 1), (512, 4), (2048, 16))
BAND = 128
ROPE_THETA = 500000.0
ROT_DIM = HEAD_DIM // 4
RET_THETA = 10000.0
RET_CHUNK = 128
N_HEADS_MEM = 4
N_EXPERTS = 32
TOP_K = 4
SWIGLU_LIMIT = 7.0
SWIGLU_ALPHA = 1.702
LN_EPS = 1e-5
DEPTH = 1
DEEPNORM_ALPHA = (2 * DEPTH) ** 0.25

GROUP = 512
TM = 512
TQ = 512
T_ROUTE = 1024
T_RANK = 512
T_MOVE = 256
BM = 256
VMEM_LIMIT = 56 * 1024 * 1024
NEG = -1e30


def _cparams(*sem):
    return pltpu.CompilerParams(dimension_semantics=sem, vmem_limit_bytes=VMEM_LIMIT)


def _nt_dot(a, b):
    return lax.dot_general(a, b, (((1,), (1,)), ((), ())), preferred_element_type=F32)


def _layer_norm(y, g, b):
    mu = jnp.mean(y, axis=-1, keepdims=True)
    d = y - mu
    var = jnp.mean(d * d, axis=-1, keepdims=True)
    return d * lax.rsqrt(var + LN_EPS) * g + b


def _in_proj_kernel(x_ref, w_ref, ca_ref, la_ref, ha_ref, cr_ref, lr_ref, hr_ref, o_ref):
    xb = x_ref[...].astype(BF16)
    rep = GROUP // 128

    def rot(acc, c_ref, lo_ref, hi_ref, half):
        c = jnp.tile(c_ref[...], (1, rep))
        lo = jnp.tile(lo_ref[...], (1, rep))
        hi = jnp.tile(hi_ref[...], (1, rep))
        up = pltpu.roll(acc, GROUP - half, axis=1)
        dn = pltpu.roll(acc, half, axis=1)
        return acc * c + up * lo + dn * hi

    for g in range(N_IN // GROUP):
        acc = jnp.dot(xb, w_ref[:, g * GROUP:(g + 1) * GROUP], preferred_element_type=F32)
        if g == 0:
            acc = rot(acc, ca_ref, la_ref, ha_ref, ROT_DIM // 2) * (HEAD_DIM ** -0.5)
        elif g == 1:
            acc = rot(acc, ca_ref, la_ref, ha_ref, ROT_DIM // 2)
        elif g == 3:
            acc = rot(acc, cr_ref, lr_ref, hr_ref, HEAD_DIM // 2)
        elif g == 4:
            acc = rot(acc, cr_ref, lr_ref, hr_ref, HEAD_DIM // 2) * (HEAD_DIM ** -0.5)
        o_ref[:, g * GROUP:(g + 1) * GROUP] = acc.astype(BF16)


def _rotary_tables(S, theta, rot_dim):
    half = rot_dim // 2
    inv = 1.0 / (theta ** (jnp.arange(half, dtype=F32) / half))
    ang = jnp.arange(S, dtype=F32)[:, None] * inv[None, :]
    cos, sin = jnp.cos(ang), jnp.sin(ang)
    pad = HEAD_DIM - rot_dim
    c = jnp.concatenate([cos, cos, jnp.ones((S, pad), F32)], axis=1)
    lo = jnp.concatenate([-sin, jnp.zeros((S, half + pad), F32)], axis=1)
    hi = jnp.concatenate([jnp.zeros((S, half), F32), sin, jnp.zeros((S, pad), F32)], axis=1)
    two = lambda t: jnp.concatenate([t, t], axis=1)
    return two(c), two(lo), two(hi)


def _in_proj(x2d, w_bf, S):
    N, D = x2d.shape
    tabs = _rotary_tables(S, ROPE_THETA, ROT_DIM) + _rotary_tables(S, RET_THETA, HEAD_DIM)
    nt = S // TM
    tab_spec = pl.BlockSpec((TM, 128), lambda i: (i % nt, 0))
    return pl.pallas_call(
        _in_proj_kernel,
        grid=(N // TM,),
        in_specs=[pl.BlockSpec((TM, D), lambda i: (i, 0)),
                  pl.BlockSpec((D, N_IN), lambda i: (0, 0))] + [tab_spec] * 6,
        out_specs=pl.BlockSpec((TM, N_IN), lambda i: (i, 0)),
        out_shape=jax.ShapeDtypeStruct((N, N_IN), BF16),
        compiler_params=_cparams("parallel"),
        name="in_proj",
    )(x2d, w_bf, *tabs)


def _band_attn_kernel(q_ref, kc_ref, kp_ref, vc_ref, vp_ref, o_ref, l_ref):
    j = pl.program_id(2)
    tq = q_ref.shape[0]
    qi = lax.broadcasted_iota(jnp.int32, (BAND, 2 * BAND), 0)
    kj = lax.broadcasted_iota(jnp.int32, (BAND, 2 * BAND), 1)
    dist = qi + BAND - kj
    band = (dist >= 0) & (dist <= BAND)
    for hh in range(N_HEADS_DIL):
        cs = slice(hh * HEAD_DIM, (hh + 1) * HEAD_DIM)
        for n in range(tq // BAND):
            rs = slice(n * BAND, (n + 1) * BAND)
            q = q_ref[rs, cs]
            if n == 0:
                k_prev, v_prev = kp_ref[:, cs], vp_ref[:, cs]
                valid = band & ((kj >= BAND) | (j > 0))
            else:
                ps = slice((n - 1) * BAND, n * BAND)
                k_prev, v_prev = kc_ref[ps, cs], vc_ref[ps, cs]
                valid = band
            kk = jnp.concatenate([k_prev, kc_ref[rs, cs]], axis=0)
            vv = jnp.concatenate([v_prev, vc_ref[rs, cs]], axis=0)
            s = jnp.where(valid, _nt_dot(q, kk), NEG)
            m = jnp.max(s, axis=-1, keepdims=True)
            p = jnp.exp(s - m)
            l = jnp.sum(p, axis=-1, keepdims=True)
            o = jnp.dot(p.astype(BF16), vv, preferred_element_type=F32) / l
            o_ref[rs, cs] = o
            l_ref[rs, cs] = jnp.broadcast_to(m + jnp.log(l), (BAND, HEAD_DIM))


def _band_attn(h, B, S, d):
    L = S // d
    gpr = N_IN // GROUP
    h3 = h.reshape(B, L, d * N_IN)
    tq = min(TQ, L)
    assert L % tq == 0
    sub = tq // BAND
    cur = lambda g: pl.BlockSpec((None, tq, GROUP), lambda b, r, j: (b, j, r * gpr + g))
    prev = lambda g: pl.BlockSpec((None, BAND, GROUP),
                                  lambda b, r, j: (b, jnp.maximum(j * sub - 1, 0), r * gpr + g))
    out_spec = pl.BlockSpec((None, tq, W_DIL), lambda b, r, j: (b, j, r))
    shp = jax.ShapeDtypeStruct((B, L, d * W_DIL), F32)
    o, lse = pl.pallas_call(
        _band_attn_kernel,
        grid=(B, d, L // tq),
        in_specs=[cur(0), cur(1), prev(1), cur(2), prev(2)],
        out_specs=[out_spec, out_spec],
        out_shape=[shp, shp],
        compiler_params=_cparams("parallel", "parallel", "parallel"),
        name=f"band_attn_d{d}",
    )(h3, h3, h3, h3, h3)
    return o.reshape(B * S, W_DIL), lse.reshape(B * S, W_DIL)


def _retention_kernel(q_ref, k_ref, v_ref, g_ref, gain_ref, din_ref, kd_ref, qd_ref, cd_ref,
                      o_ref, kv_ref):
    n_chunks = q_ref.shape[0] // RET_CHUNK
    C = RET_CHUNK

    def rows(i):
        return pl.ds(pl.multiple_of(i * C, C), C)

    def kv_body(i, carry):
        for hh in range(2):
            cs = slice(hh * HEAD_DIM, (hh + 1) * HEAD_DIM)
            k = k_ref[rows(i), cs].astype(F32)
            kd = (k * kd_ref[hh]).astype(BF16)
            kv_ref[i, hh] = lax.dot_general(kd, v_ref[rows(i), cs], (((0,), (0,)), ((), ())),
                                            preferred_element_type=F32)
        return carry

    lax.fori_loop(0, n_chunks, kv_body, 0)

    def out_body(i, states):
        new_states = []
        for hh in range(2):
            cs = slice(hh * HEAD_DIM, (hh + 1) * HEAD_DIM)
            state = states[hh]
            q = q_ref[rows(i), cs]
            k = k_ref[rows(i), cs]
            v = v_ref[rows(i), cs]
            s = _nt_dot(q, k) * din_ref[hh]
            inner = jnp.dot(s.astype(BF16), v, preferred_element_type=F32)
            qd = (q.astype(F32) * qd_ref[hh]).astype(BF16)
            cross = jnp.dot(qd, state.astype(BF16), preferred_element_type=F32)
            y = inner + cross
            mu = jnp.mean(y, axis=-1, keepdims=True)
            dlt = y - mu
            var = jnp.mean(dlt * dlt, axis=-1, keepdims=True)
            yn = dlt * lax.rsqrt(var + LN_EPS) * gain_ref[:, cs]
            gr = g_ref[rows(i), cs].astype(F32)
            o_ref[rows(i), cs] = (yn * (gr / (1.0 + jnp.exp(-gr)))).astype(BF16)
            new_states.append(state * cd_ref[hh] + kv_ref[i, hh])
        return tuple(new_states)

    zero = jnp.zeros((HEAD_DIM, HEAD_DIM), F32)
    lax.fori_loop(0, n_chunks, out_body, (zero, zero))


def _retention_tables():
    H, C = N_HEADS_RET, RET_CHUNK
    log_g = jnp.log1p(-(2.0 ** (-5.0 - jnp.arange(H, dtype=F32))))
    idx = jnp.arange(C, dtype=F32)
    diff = idx[:, None] - idx[None, :]
    inner_decay = jnp.exp(log_g[:, None, None] * jnp.maximum(diff, 0.0)) * (diff >= 0)
    k_decay = jnp.exp(log_g[:, None] * (C - 1 - idx))
    q_decay = jnp.exp(log_g[:, None] * (idx + 1.0))
    chunk_decay = jnp.exp(log_g * C)
    bc = lambda t: jnp.broadcast_to(t[:, :, None], (H, C, HEAD_DIM))
    cd = jnp.broadcast_to(chunk_decay[:, None, None], (H, HEAD_DIM, HEAD_DIM))
    return inner_decay.astype(F32), bc(k_decay), bc(q_decay), cd


def _retention(h, gain, B, S):
    h3 = h.reshape(B, S, N_IN)
    lanes = 2 * HEAD_DIM
    col0 = 3 * W_DIL // lanes
    per = W_RET // lanes
    col = lambda g: pl.BlockSpec((None, S, lanes), lambda b, p: (b, 0, col0 + g * per + p))
    head2 = lambda a, c: pl.BlockSpec((2, a, c), lambda b, p: (p, 0, 0))
    din, kd, qd, cd = _retention_tables()
    C = RET_CHUNK
    return pl.pallas_call(
        _retention_kernel,
        grid=(B, per),
        in_specs=[col(0), col(1), col(2), col(3),
                  pl.BlockSpec((1, lanes), lambda b, p: (0, p)),
                  head2(C, C), head2(C, HEAD_DIM), head2(C, HEAD_DIM), head2(HEAD_DIM, HEAD_DIM)],
        out_specs=pl.BlockSpec((None, S, lanes), lambda b, p: (b, 0, p)),
        out_shape=jax.ShapeDtypeStruct((B, S, W_RET), BF16),
        scratch_shapes=[pltpu.VMEM((S // C, 2, HEAD_DIM, HEAD_DIM), F32)],
        compiler_params=_cparams("parallel", "parallel"),
        name="retention",
    )(h3, h3, h3, h3, gain.reshape(1, W_RET).astype(F32), din, kd, qd, cd).reshape(B * S, W_RET)


def _out_proj_kernel(o1, o4, o16, l1, l4, l16, ret_ref, x_ref, w_ref, g_ref, b_ref, out_ref):
    a1, a4, a16 = l1[...], l4[...], l16[...]
    m = jnp.maximum(jnp.maximum(a1, a4), a16)
    e1, e4, e16 = jnp.exp(a1 - m), jnp.exp(a4 - m), jnp.exp(a16 - m)
    att = (e1 * o1[...] + e4 * o4[...] + e16 * o16[...]) / (e1 + e4 + e16)
    acc = jnp.dot(att.astype(BF16), w_ref[:W_DIL, :], preferred_element_type=F32)
    acc += jnp.dot(ret_ref[...], w_ref[W_DIL:, :], preferred_element_type=F32)
    out_ref[...] = _layer_norm(DEEPNORM_ALPHA * x_ref[...] + acc, g_ref[...], b_ref[...])


def _out_proj(os_, ls_, ret, x2d, w_bf, g, b):
    N, D = x2d.shape
    half = pl.BlockSpec((TM, W_DIL), lambda i: (i, 0))
    full = pl.BlockSpec((TM, D), lambda i: (i, 0))
    vec = pl.BlockSpec((1, D), lambda i: (0, 0))
    return pl.pallas_call(
        _out_proj_kernel,
        grid=(N // TM,),
        in_specs=[half] * 7 + [full, pl.BlockSpec((D, D), lambda i: (0, 0)), vec, vec],
        out_specs=full,
        out_shape=jax.ShapeDtypeStruct((N, D), F32),
        compiler_params=_cparams("parallel"),
        name="out_proj_ln1",
    )(*os_, *ls_, ret, x2d, w_bf, g.reshape(1, D), b.reshape(1, D))


def _mem_kv_kernel(m_ref, wk_ref, wv_ref, k_ref, v_ref):
    mb = m_ref[...].astype(BF16)
    k_ref[...] = jnp.dot(mb, wk_ref[...], preferred_element_type=F32).astype(BF16)
    v_ref[...] = jnp.dot(mb, wv_ref[...], preferred_element_type=F32).astype(BF16)


def _mem_kv(mem, wk_bf, wv_bf):
    B, M, D = mem.shape
    blk = pl.BlockSpec((None, M, D), lambda b: (b, 0, 0))
    wsp = pl.BlockSpec((D, D), lambda b: (0, 0))
    shp = jax.ShapeDtypeStruct((B, M, D), BF16)
    return pl.pallas_call(
        _mem_kv_kernel, grid=(B,), in_specs=[blk, wsp, wsp], out_specs=[blk, blk], out_shape=[shp, shp],
        compiler_params=_cparams("parallel"), name="mem_kv",
    )(mem, wk_bf, wv_bf)


def _cross_attn_kernel(x_ref, k_ref, v_ref, wq_ref, wo_ref, g_ref, b_ref, out_ref):
    x = x_ref[...]
    hd = x.shape[-1] // N_HEADS_MEM
    q = (jnp.dot(x.astype(BF16), wq_ref[...], preferred_element_type=F32) * (hd ** -0.5)).astype(BF16)
    outs = []
    for hh in range(N_HEADS_MEM):
        cs = slice(hh * hd, (hh + 1) * hd)
        s = _nt_dot(q[:, cs], k_ref[:, cs])
        m = jnp.max(s, axis=-1, keepdims=True)
        p = jnp.exp(s - m)
        l = jnp.sum(p, axis=-1, keepdims=True)
        outs.append((jnp.dot(p.astype(BF16), v_ref[:, cs], preferred_element_type=F32) / l).astype(BF16))
    o = jnp.concatenate(outs, axis=-1)
    c = jnp.dot(o, wo_ref[...], preferred_element_type=F32)
    out_ref[...] = _layer_norm(DEEPNORM_ALPHA * x + c, g_ref[...], b_ref[...])


def _cross_attn(x1, kmem, vmem, wq_bf, wo_bf, g, b, B, S):
    D = x1.shape[-1]
    M = kmem.shape[1]
    x3 = x1.reshape(B, S, D)
    xs = pl.BlockSpec((None, TM, D), lambda b_, i: (b_, i, 0))
    ms = pl.BlockSpec((None, M, D), lambda b_, i: (b_, 0, 0))
    ws = pl.BlockSpec((D, D), lambda b_, i: (0, 0))
    vs = pl.BlockSpec((1, D), lambda b_, i: (0, 0))
    return pl.pallas_call(
        _cross_attn_kernel,
        grid=(B, S // TM),
        in_specs=[xs, ms, ms, ws, ws, vs, vs],
        out_specs=xs,
        out_shape=jax.ShapeDtypeStruct((B, S, D), F32),
        compiler_params=_cparams("parallel", "parallel"),
        name="cross_attn_ln2",
    )(x3, kmem, vmem, wq_bf, wo_bf, g.reshape(1, D), b.reshape(1, D)).reshape(B * S, D)


def _router_kernel(x_ref, w_ref, b_ref, idx_ref, wt_ref):
    logits = lax.dot_general(w_ref[...], x_ref[...], (((1,), (1,)), ((), ())),
                             precision=lax.Precision.HIGHEST, preferred_element_type=F32) + b_ref[:, :1]
    e_iota = lax.broadcasted_iota(jnp.int32, logits.shape, 0)
    cur = logits
    vals, idxs = [], []
    for _ in range(TOP_K):
        m = jnp.max(cur, axis=0, keepdims=True)
        idx = jnp.min(jnp.where(cur == m, e_iota, N_EXPERTS), axis=0, keepdims=True)
        cur = jnp.where(e_iota == idx, -jnp.inf, cur)
        vals.append(m)
        idxs.append(idx)
    es = [jnp.exp(v - vals[0]) for v in vals]
    tot = es[0] + es[1] + es[2] + es[3]
    idx_ref[...] = jnp.concatenate(idxs, axis=0)
    wt_ref[...] = jnp.concatenate([e / tot for e in es], axis=0)


def _router(x2, router_w, router_b):
    N, D = x2.shape
    out = pl.BlockSpec((TOP_K, T_ROUTE), lambda i: (0, i))
    return pl.pallas_call(
        _router_kernel,
        grid=(N // T_ROUTE,),
        in_specs=[pl.BlockSpec((T_ROUTE, D), lambda i: (i, 0)),
                  pl.BlockSpec((N_EXPERTS, D), lambda i: (0, 0)),
                  pl.BlockSpec((N_EXPERTS, 128), lambda i: (0, 0))],
        out_specs=[out, out],
        out_shape=[jax.ShapeDtypeStruct((TOP_K, N), jnp.int32), jax.ShapeDtypeStruct((TOP_K, N), F32)],
        compiler_params=_cparams("parallel"),
        name="router_top4",
    )(x2, router_w.T.astype(F32), jnp.broadcast_to(router_b.astype(F32)[:, None], (N_EXPERTS, 128)))


def _rank_kernel(idx_ref, tri_ref, rank_ref, cnt_ref, carry_ref):
    @pl.when(pl.program_id(0) == 0)
    def _():
        carry_ref[...] = jnp.zeros_like(carry_ref)

    T = idx_ref.shape[1]
    e_iota = lax.broadcasted_iota(jnp.int32, (N_EXPERTS, T), 0)
    hot = [e_iota == idx_ref[k:k + 1, :] for k in range(TOP_K)]
    c = sum(h.astype(F32) for h in hot)
    before = jnp.dot(c.astype(BF16), tri_ref[...], preferred_element_type=F32) + carry_ref[:, :1]
    rank_ref[...] = jnp.concatenate(
        [jnp.sum(jnp.where(h, before, 0.0), axis=0, keepdims=True) for h in hot], axis=0).astype(jnp.int32)
    carry_ref[...] = carry_ref[...] + jnp.sum(c, axis=1, keepdims=True)
    cnt_ref[...] = carry_ref[...].astype(jnp.int32)


def _ranks(idx):
    N = idx.shape[1]
    T = T_RANK
    tri = (jnp.arange(T)[:, None] < jnp.arange(T)[None, :]).astype(BF16)
    blk = pl.BlockSpec((TOP_K, T), lambda i: (0, i))
    return pl.pallas_call(
        _rank_kernel,
        grid=(N // T,),
        in_specs=[blk, pl.BlockSpec((T, T), lambda i: (0, 0))],
        out_specs=[blk, pl.BlockSpec((N_EXPERTS, 128), lambda i: (0, 0))],
        out_shape=[jax.ShapeDtypeStruct((TOP_K, N), jnp.int32),
                   jax.ShapeDtypeStruct((N_EXPERTS, 128), jnp.int32)],
        scratch_shapes=[pltpu.VMEM((N_EXPERTS, 128), F32)],
        compiler_params=_cparams("arbitrary"),
        name="expert_ranks",
    )(idx, tri)


def _dest_kernel(idx_ref, rank_ref, start_ref, dest_ref):
    T = idx_ref.shape[1]
    e_iota = lax.broadcasted_iota(jnp.int32, (N_EXPERTS, T), 0)
    start = jnp.tile(start_ref[...], (1, T // 128))
    rows = [jnp.sum(jnp.where(e_iota == idx_ref[k:k + 1, :], start, 0), axis=0, keepdims=True)
            for k in range(TOP_K)]
    dest_ref[...] = jnp.concatenate(rows, axis=0) + rank_ref[...]


def _dests(idx, rank, group_start):
    N = idx.shape[1]
    T = T_ROUTE
    blk = pl.BlockSpec((TOP_K, T), lambda i: (0, i))
    return pl.pallas_call(
        _dest_kernel,
        grid=(N // T,),
        in_specs=[blk, blk, pl.BlockSpec((N_EXPERTS, 128), lambda i: (0, 0))],
        out_specs=blk,
        out_shape=jax.ShapeDtypeStruct((TOP_K, N), jnp.int32),
        compiler_params=_cparams("parallel"),
        name="expert_dests",
    )(idx, rank, jnp.broadcast_to(group_start[:, None], (N_EXPERTS, 128)))


def _pack_bf16_pairs(x):
    c = x.shape[1] // 2
    hi = pltpu.bitcast(x[:, :c].astype(BF16).astype(F32), jnp.uint32)
    lo = pltpu.bitcast(x[:, c:].astype(BF16).astype(F32), jnp.uint32)
    return (hi & jnp.uint32(0xFFFF0000)) | (lo >> 16)


def _unpack_bf16_pairs(u):
    hi = pltpu.bitcast(u & jnp.uint32(0xFFFF0000), F32).astype(BF16)
    lo = pltpu.bitcast(u << 16, F32).astype(BF16)
    return hi, lo


def _dispatch_kernel(dest_ref, x_ref, xs_ref, pk_ref, sem):
    T = x_ref.shape[0]
    pk_ref[...] = _pack_bf16_pairs(x_ref[...])

    def row_copy(t, k):
        return pltpu.make_async_copy(pk_ref.at[pl.ds(t, 1), :], xs_ref.at[pl.ds(dest_ref[k, t], 1), :], sem)

    def issue(t, carry):
        for k in range(TOP_K):
            row_copy(t, k).start()
        return carry

    lax.fori_loop(0, T, issue, 0)

    def drain(t, carry):
        for k in range(TOP_K):
            row_copy(t, k).wait()
        return carry

    lax.fori_loop(0, T, drain, 0)


def _dispatch(x2, dest, P):
    N, D = x2.shape
    T = T_MOVE
    return pl.pallas_call(
        _dispatch_kernel,
        grid=(N // T,),
        in_specs=[pl.BlockSpec((TOP_K, T), lambda i: (0, i), memory_space=pltpu.SMEM),
                  pl.BlockSpec((T, D), lambda i: (i, 0))],
        out_specs=pl.BlockSpec(memory_space=pl.ANY),
        out_shape=jax.ShapeDtypeStruct((P, D // 2), jnp.uint32),
        scratch_shapes=[pltpu.VMEM((T, D // 2), jnp.uint32), pltpu.SemaphoreType.DMA],
        compiler_params=_cparams("arbitrary"),
        name="moe_dispatch",
    )(dest, x2)


def _expert_kernel(be_ref, bv_ref, xs_ref, wg_ref, wu_ref, wd_ref, bg_ref, bu_ref, bd_ref, ys_ref):
    j = pl.program_id(0)
    valid = bv_ref[j]

    @pl.when(valid > 0)
    def _():
        u = xs_ref[...]
        row = lax.broadcasted_iota(jnp.int32, u.shape, 0)
        u = jnp.where(row < valid, u, jnp.uint32(0))
        hi, lo = _unpack_bf16_pairs(u)
        c = u.shape[1]

        def proj(w_ref, b_ref):
            return (jnp.dot(hi, w_ref[:c, :], preferred_element_type=F32)
                    + jnp.dot(lo, w_ref[c:, :], preferred_element_type=F32) + b_ref[...])

        gate = jnp.minimum(proj(wg_ref, bg_ref), SWIGLU_LIMIT)
        up = jnp.clip(proj(wu_ref, bu_ref), -SWIGLU_LIMIT, SWIGLU_LIMIT)
        hmid = gate * (1.0 / (1.0 + jnp.exp(-SWIGLU_ALPHA * gate))) * (up + 1.0)
        ys_ref[...] = jnp.dot(hmid.astype(BF16), wd_ref[...], preferred_element_type=F32) + bd_ref[...]


def _experts(xs, block_e, block_valid, wg, wu, wd, bg, bu, bd):
    P, C = xs.shape
    D = 2 * C
    F = wg.shape[-1]
    wspec = lambda a, c: pl.BlockSpec((None, a, c), lambda j, be, bv: (be[j], 0, 0))
    grid_spec = pltpu.PrefetchScalarGridSpec(
        num_scalar_prefetch=2,
        grid=(P // BM,),
        in_specs=[pl.BlockSpec((BM, C), lambda j, be, bv: (j, 0)),
                  wspec(D, F), wspec(D, F), wspec(F, D), wspec(1, F), wspec(1, F), wspec(1, D)],
        out_specs=pl.BlockSpec((BM, D), lambda j, be, bv: (j, 0)),
    )
    return pl.pallas_call(
        _expert_kernel,
        grid_spec=grid_spec,
        out_shape=jax.ShapeDtypeStruct((P, D), F32),
        compiler_params=_cparams("arbitrary"),
        name="moe_experts",
    )(block_e, block_valid, xs, wg, wu, wd, bg, bu, bd)


def _combine_kernel(dest_ref, x_ref, wt_ref, ys_ref, g_ref, b_ref, out_ref, buf_ref, sem):
    T = x_ref.shape[0]

    def row_copy(t, k):
        return pltpu.make_async_copy(ys_ref.at[pl.ds(dest_ref[k, t], 1), :],
                                     buf_ref.at[pl.ds(k * T + t, 1), :], sem)

    def issue(t, carry):
        for k in range(TOP_K):
            row_copy(t, k).start()
        return carry

    lax.fori_loop(0, T, issue, 0)

    def drain(t, carry):
        for k in range(TOP_K):
            row_copy(t, k).wait()
        return carry

    lax.fori_loop(0, T, drain, 0)

    wt = wt_ref[...]
    y = wt[:, 0:1] * buf_ref[0:T, :]
    for k in range(1, TOP_K):
        y = y + wt[:, k:k + 1] * buf_ref[k * T:(k + 1) * T, :]
    out_ref[...] = _layer_norm(DEEPNORM_ALPHA * x_ref[...] + y, g_ref[...], b_ref[...])


def _combine(x2, dest, wt_tok, ys, g, b):
    N, D = x2.shape
    T = T_MOVE
    vec = pl.BlockSpec((1, D), lambda i: (0, 0))
    return pl.pallas_call(
        _combine_kernel,
        grid=(N // T,),
        in_specs=[pl.BlockSpec((TOP_K, T), lambda i: (0, i), memory_space=pltpu.SMEM),
                  pl.BlockSpec((T, D), lambda i: (i, 0)),
                  pl.BlockSpec((T, TOP_K), lambda i: (i, 0)),
                  pl.BlockSpec(memory_space=pl.ANY), vec, vec],
        out_specs=pl.BlockSpec((T, D), lambda i: (i, 0)),
        out_shape=jax.ShapeDtypeStruct((N, D), F32),
        scratch_shapes=[pltpu.VMEM((TOP_K * T, D), F32), pltpu.SemaphoreType.DMA],
        compiler_params=_cparams("arbitrary"),
        name="moe_combine_ln3",
    )(dest, x2, wt_tok, ys, g.reshape(1, D), b.reshape(1, D))


def _moe(x2, router_w, router_b, w_gate, b_gate, w_up, b_up, w_down, b_down, ln_g, ln_b):
    N, D = x2.shape
    idx, wt = _router(x2, router_w, router_b)
    rank, cnt = _ranks(idx)
    counts = cnt[:, 0]
    blocks = (counts + BM - 1) // BM
    blk_end = jnp.cumsum(blocks)
    blk_start = blk_end - blocks
    dest = _dests(idx, rank, (blk_start * BM).astype(jnp.int32))
    n_blocks = -(-N * TOP_K // BM) + N_EXPERTS
    bi = jnp.arange(n_blocks, dtype=jnp.int32)
    be_raw = jnp.searchsorted(blk_end, bi, side='right').astype(jnp.int32)
    last_e = jnp.max(jnp.where(counts > 0, jnp.arange(N_EXPERTS), 0)).astype(jnp.int32)
    block_e = jnp.minimum(be_raw, last_e)
    in_use = bi < blk_end[-1]
    block_valid = jnp.where(in_use, jnp.clip(counts[block_e] - (bi - blk_start[block_e]) * BM, 0, BM), 0)
    xs = _dispatch(x2, dest, n_blocks * BM)
    f3 = lambda t: t.astype(F32)[:, None, :]
    ys = _experts(xs, block_e, block_valid.astype(jnp.int32),
                  w_gate.astype(BF16), w_up.astype(BF16), w_down.astype(BF16),
                  f3(b_gate), f3(b_up), f3(b_down))
    return _combine(x2, dest, wt.T, ys, ln_g, ln_b)


def kernel(x, mem, w_in, ret_norm_g, w_out, ln1_g, ln1_b, mem_wq, mem_wk, mem_wv, mem_wo, ln2_g, ln2_b,
           router_w, router_b, w_gate, b_gate, w_up, b_up, w_down, b_down, ln3_g, ln3_b):
    B, S, D = x.shape
    xf = x.reshape(B * S, D)
    for l in range(w_in.shape[0]):
        h = _in_proj(xf, w_in[l].astype(BF16), S)
        pats = [_band_attn(h, B, S, d) for _, d in DILATED_PATTERNS]
        ret = _retention(h, ret_norm_g[l], B, S)
        x1 = _out_proj([p[0] for p in pats], [p[1] for p in pats], ret, xf,
                       w_out[l].astype(BF16), ln1_g[l], ln1_b[l])
        kmem, vmem = _mem_kv(mem, mem_wk[l].astype(BF16), mem_wv[l].astype(BF16))
        x2 = _cross_attn(x1, kmem, vmem, mem_wq[l].astype(BF16), mem_wo[l].astype(BF16),
                         ln2_g[l], ln2_b[l], B, S)
        xf = _moe(x2, router_w[l], router_b[l], w_gate[l], b_gate[l], w_up[l], b_up[l],
                  w_down[l], b_down[l], ln3_g[l], ln3_b[l])
    return xf.reshape(B, S, D)
```

```python
import functools

import jax
import jax.numpy as jnp
from jax import lax
from jax.experimental import pallas as pl
from jax.experimental.pallas import tpu as pltpu

F32 = jnp.float32
BF16 = jnp.bfloat16

HEAD_DIM = 64
N_HEADS_DIL = 8
N_HEADS_RET = 8
W_DIL = N_HEADS_DIL * HEAD_DIM
W_RET = N_HEADS_RET * HEAD_DIM
N_IN = 3 * W_DIL + 4 * W_RET
DILATED_PATTERNS = ((128, 1), (512, 4), (2048, 16))
BAND = 128
ROPE_THETA = 500000.0
ROT_DIM = HEAD_DIM // 4
RET_THETA = 10000.0
RET_CHUNK = 128
N_HEADS_MEM = 4
N_EXPERTS = 32
TOP_K = 4
SWIGLU_LIMIT = 7.0
SWIGLU_ALPHA = 1.702
LN_EPS = 1e-5
DEPTH = 1
DEEPNORM_ALPHA = (2 * DEPTH) ** 0.25

GROUP = 512
TM = 512
TQ = 512
T_ROUTE = 1024
T_RANK = 512
T_MOVE = 256
BM = 256
VMEM_LIMIT = 56 * 1024 * 1024
NEG = -1e30


def _cparams(*sem):
    return pltpu.CompilerParams(dimension_semantics=sem, vmem_limit_bytes=VMEM_LIMIT)


def _nt_dot(a, b):
    return lax.dot_general(a, b, (((1,), (1,)), ((), ())), preferred_element_type=F32)


def _layer_norm(y, g, b):
    mu = jnp.mean(y, axis=-1, keepdims=True)
    d = y - mu
    var = jnp.mean(d * d, axis=-1, keepdims=True)
    return d * lax.rsqrt(var + LN_EPS) * g + b


def _in_proj_kernel(x_ref, w_ref, ca_ref, la_ref, ha_ref, cr_ref, lr_ref, hr_ref, o_ref, o4_ref, o16_ref,
                    acc_ref):
    xb = x_ref[...].astype(BF16)
    rep = GROUP // 128
    tm = x_ref.shape[0]
    qkv = 3 * W_DIL

    def rot(acc, c_ref, lo_ref, hi_ref, half):
        c = jnp.tile(c_ref[...], (1, rep))
        lo = jnp.tile(lo_ref[...], (1, rep))
        hi = jnp.tile(hi_ref[...], (1, rep))
        up = pltpu.roll(acc, GROUP - half, axis=1)
        dn = pltpu.roll(acc, half, axis=1)
        return acc * c + up * lo + dn * hi

    for g in range(N_IN // GROUP):
        acc = jnp.dot(xb, w_ref[:, g * GROUP:(g + 1) * GROUP], preferred_element_type=F32)
        if g == 0:
            acc = rot(acc, ca_ref, la_ref, ha_ref, ROT_DIM // 2) * (HEAD_DIM ** -0.5)
        elif g == 1:
            acc = rot(acc, ca_ref, la_ref, ha_ref, ROT_DIM // 2)
        elif g == 3:
            acc = rot(acc, cr_ref, lr_ref, hr_ref, HEAD_DIM // 2)
        elif g == 4:
            acc = rot(acc, cr_ref, lr_ref, hr_ref, HEAD_DIM // 2) * (HEAD_DIM ** -0.5)
        o_ref[:, g * GROUP:(g + 1) * GROUP] = acc.astype(BF16)
        if g < 3:
            for c in range(rep):
                acc_ref[c] = acc[:, c * 128:(c + 1) * 128]
            for d, od_ref in ((4, o4_ref), (16, o16_ref)):
                for r in range(d):
                    for c in range(rep):
                        c0 = r * qkv + g * GROUP + c * 128
                        od_ref[:, c0:c0 + 128] = acc_ref[c, pl.ds(r, tm // d, stride=d), :].astype(BF16)


def _rotary_tables(S, theta, rot_dim):
    half = rot_dim // 2
    inv = 1.0 / (theta ** (jnp.arange(half, dtype=F32) / half))
    ang = jnp.arange(S, dtype=F32)[:, None] * inv[None, :]
    cos, sin = jnp.cos(ang), jnp.sin(ang)
    pad = HEAD_DIM - rot_dim
    c = jnp.concatenate([cos, cos, jnp.ones((S, pad), F32)], axis=1)
    lo = jnp.concatenate([-sin, jnp.zeros((S, half + pad), F32)], axis=1)
    hi = jnp.concatenate([jnp.zeros((S, half), F32), sin, jnp.zeros((S, pad), F32)], axis=1)
    two = lambda t: jnp.concatenate([t, t], axis=1)
    return two(c), two(lo), two(hi)


def _in_proj(x3, w_bf):
    B, S, D = x3.shape
    tabs = _rotary_tables(S, ROPE_THETA, ROT_DIM) + _rotary_tables(S, RET_THETA, HEAD_DIM)
    tab_spec = pl.BlockSpec((TM, 128), lambda b, i: (i, 0))
    qkv = 3 * W_DIL
    cls_spec = lambda d: pl.BlockSpec((None, TM // d, d * qkv), lambda b, i: (b, i, 0))
    cls_shape = lambda d: jax.ShapeDtypeStruct((B, S // d, d * qkv), BF16)
    return pl.pallas_call(
        _in_proj_kernel,
        grid=(B, S // TM),
        in_specs=[pl.BlockSpec((None, TM, D), lambda b, i: (b, i, 0)),
                  pl.BlockSpec((D, N_IN), lambda b, i: (0, 0))] + [tab_spec] * 6,
        out_specs=[pl.BlockSpec((None, TM, N_IN), lambda b, i: (b, i, 0)), cls_spec(4), cls_spec(16)],
        out_shape=[jax.ShapeDtypeStruct((B, S, N_IN), BF16), cls_shape(4), cls_shape(16)],
        scratch_shapes=[pltpu.VMEM((GROUP // 128, TM, 128), F32)],
        compiler_params=_cparams("parallel", "parallel"),
        name="in_proj",
    )(x3, w_bf, *tabs)


def _band_attn_kernel(q_ref, kc_ref, kp_ref, vc_ref, vp_ref, o_ref, l_ref,
                      k_all, v_all, s_scr, p_scr, r_scr):
    j = pl.program_id(2)
    tq = q_ref.shape[0]
    nsub = tq // BAND
    pair = 2 * HEAD_DIM
    n_pairs = N_HEADS_DIL // 2
    k_all[0:BAND, :] = kp_ref[...]
    k_all[BAND:, :] = kc_ref[...]
    v_all[0:BAND, :] = vp_ref[...]
    v_all[BAND:, :] = vc_ref[...]
    qi = lax.broadcasted_iota(jnp.int32, (BAND, 2 * BAND), 0)
    kj = lax.broadcasted_iota(jnp.int32, (BAND, 2 * BAND), 1)
    dist = qi + BAND - kj
    band = (dist >= 0) & (dist <= BAND)
    bias = jnp.where(band, 0.0, NEG)
    bias_first = jnp.where(band & ((kj >= BAND) | (j > 0)), 0.0, NEG)
    low = lax.broadcasted_iota(jnp.int32, (BAND, pair), 1) < HEAD_DIM
    zero = jnp.zeros((BAND, pair), BF16)

    for pr in range(n_pairs):
        cs = slice(pr * pair, (pr + 1) * pair)
        for n in range(nsub):
            q = q_ref[n * BAND:(n + 1) * BAND, cs]
            kk = k_all[n * BAND:(n + 2) * BAND, cs]
            b = bias_first if n == 0 else bias
            i = pr * nsub + n
            s_scr[2 * i] = _nt_dot(jnp.where(low, q, zero), kk) + b
            s_scr[2 * i + 1] = _nt_dot(jnp.where(low, zero, q), kk) + b

    for pr in range(n_pairs):
        cs = slice(pr * pair, (pr + 1) * pair)
        for n in range(nsub):
            i = pr * nsub + n
            stats = []
            for a in range(2):
                s = s_scr[2 * i + a]
                m = jnp.max(s, axis=-1, keepdims=True)
                p = jnp.exp(s - m)
                l = jnp.sum(p, axis=-1, keepdims=True)
                p_scr[2 * i + a] = p.astype(BF16)
                stats.append((m, l))
            (m0, l0), (m1, l1) = stats
            r_scr[i] = jnp.where(low, 1.0 / l0, 1.0 / l1)
            l_ref[n * BAND:(n + 1) * BAND, cs] = jnp.where(low, m0 + jnp.log(l0), m1 + jnp.log(l1))

    for pr in range(n_pairs):
        cs = slice(pr * pair, (pr + 1) * pair)
        for n in range(nsub):
            i = pr * nsub + n
            vv = v_all[n * BAND:(n + 2) * BAND, cs]
            o0 = jnp.dot(p_scr[2 * i], vv, preferred_element_type=F32)
            o1 = jnp.dot(p_scr[2 * i + 1], vv, preferred_element_type=F32)
            o_ref[n * BAND:(n + 1) * BAND, cs] = jnp.where(low, o0, o1) * r_scr[i]


def _band_attn(src, B, L, d, gpr):
    tq = min(TQ, L)
    assert L % tq == 0
    sub = tq // BAND
    cur = lambda g: pl.BlockSpec((None, tq, GROUP), lambda b, r, j: (b, j, r * gpr + g))
    prev = lambda g: pl.BlockSpec((None, BAND, GROUP),
                                  lambda b, r, j: (b, jnp.maximum(j * sub - 1, 0), r * gpr + g))
    out_spec = pl.BlockSpec((None, tq, W_DIL), lambda b, r, j: (b, j, r))
    shp = jax.ShapeDtypeStruct((B, L, d * W_DIL), F32)
    n_blk = (N_HEADS_DIL // 2) * sub
    return pl.pallas_call(
        _band_attn_kernel,
        grid=(B, d, L // tq),
        in_specs=[cur(0), cur(1), prev(1), cur(2), prev(2)],
        out_specs=[out_spec, out_spec],
        out_shape=[shp, shp],
        scratch_shapes=[pltpu.VMEM((BAND + tq, GROUP), BF16), pltpu.VMEM((BAND + tq, GROUP), BF16),
                        pltpu.VMEM((2 * n_blk, BAND, 2 * BAND), F32),
                        pltpu.VMEM((2 * n_blk, BAND, 2 * BAND), BF16),
                        pltpu.VMEM((n_blk, BAND, 2 * HEAD_DIM), F32)],
        compiler_params=_cparams("parallel", "parallel", "parallel"),
        name=f"band_attn_d{d}",
    )(src, src, src, src, src)


RET_UNROLL = 4


def _retention_kernel(q_ref, k_ref, v_ref, g_ref, gain_ref, din_ref, kd_ref, qd_ref, cd_ref,
                      o_ref, kv_ref, st_ref):
    C = RET_CHUNK
    n_iter = q_ref.shape[0] // (C * RET_UNROLL)
    pair = 2 * HEAD_DIM
    low = lax.broadcasted_iota(jnp.int32, (C, pair), 1) < HEAD_DIM

    def rows(i, u):
        return pl.ds(pl.multiple_of(i * (C * RET_UNROLL), C * RET_UNROLL) + u * C, C)

    def kv_body(i, carry):
        for u in range(RET_UNROLL):
            kd = (k_ref[rows(i, u), :].astype(F32) * kd_ref[...]).astype(BF16)
            kv_ref[i * RET_UNROLL + u] = lax.dot_general(kd, v_ref[rows(i, u), :], (((0,), (0,)), ((), ())),
                                                         preferred_element_type=F32)
        return carry

    lax.fori_loop(0, n_iter, kv_body, 0)

    def scan_body(c, state):
        st_ref[c] = state.astype(BF16)
        return state * cd_ref[...] + kv_ref[c]

    lax.fori_loop(0, n_iter * RET_UNROLL, scan_body, jnp.zeros((pair, pair), F32))

    def out_body(i, carry):
        for u in range(RET_UNROLL):
            q = q_ref[rows(i, u), :]
            k = k_ref[rows(i, u), :]
            v = v_ref[rows(i, u), :]
            prev = st_ref[i * RET_UNROLL + u]
            ys = []
            for a in range(2):
                qa = jnp.where(low, q, jnp.zeros_like(q)) if a == 0 else jnp.where(low, jnp.zeros_like(q), q)
                s = _nt_dot(qa, k) * din_ref[a]
                inner = jnp.dot(s.astype(BF16), v, preferred_element_type=F32)
                qd = (qa.astype(F32) * qd_ref[...]).astype(BF16)
                ys.append(inner + jnp.dot(qd, prev, preferred_element_type=F32))
            y = jnp.where(low, ys[0], ys[1])
            inv = 1.0 / HEAD_DIM
            half_sum = lambda t: jnp.where(low, jnp.sum(jnp.where(low, t, 0.0), axis=-1, keepdims=True),
                                           jnp.sum(jnp.where(low, 0.0, t), axis=-1, keepdims=True))
            dlt = y - half_sum(y) * inv
            var = half_sum(dlt * dlt) * inv
            yn = dlt * lax.rsqrt(var + LN_EPS) * gain_ref[...]
            gr = g_ref[rows(i, u), :].astype(F32)
            o_ref[rows(i, u), :] = (yn * (gr / (1.0 + jnp.exp(-gr)))).astype(BF16)
        return carry

    lax.fori_loop(0, n_iter, out_body, 0)


def _retention_tables():
    H, C = N_HEADS_RET, RET_CHUNK
    log_g = jnp.log1p(-(2.0 ** (-5.0 - jnp.arange(H, dtype=F32))))
    idx = jnp.arange(C, dtype=F32)
    diff = idx[:, None] - idx[None, :]
    inner_decay = jnp.exp(log_g[:, None, None] * jnp.maximum(diff, 0.0)) * (diff >= 0)
    k_decay = jnp.exp(log_g[:, None] * (C - 1 - idx))
    q_decay = jnp.exp(log_g[:, None] * (idx + 1.0))
    chunk_decay = jnp.exp(log_g * C)
    lanes = lambda t: jnp.repeat(t.reshape(H // 2, 2, C).transpose(0, 2, 1), HEAD_DIM, axis=2)
    cd = jnp.broadcast_to(jnp.repeat(chunk_decay.reshape(H // 2, 2), HEAD_DIM, axis=1)[:, :, None],
                          (H // 2, 2 * HEAD_DIM, 2 * HEAD_DIM))
    return inner_decay.astype(F32), lanes(k_decay), lanes(q_decay), cd


def _retention(h, gain):
    B, S, _ = h.shape
    lanes = 2 * HEAD_DIM
    col0 = 3 * W_DIL // lanes
    per = W_RET // lanes
    col = lambda g: pl.BlockSpec((None, S, lanes), lambda b, p: (b, 0, col0 + g * per + p))
    din, kd, qd, cd = _retention_tables()
    C = RET_CHUNK
    assert S % (C * RET_UNROLL) == 0
    tab = lambda a: pl.BlockSpec((None, a, lanes), lambda b, p: (p, 0, 0))
    return pl.pallas_call(
        _retention_kernel,
        grid=(B, per),
        in_specs=[col(0), col(1), col(2), col(3),
                  pl.BlockSpec((1, lanes), lambda b, p: (0, p)),
                  pl.BlockSpec((2, C, C), lambda b, p: (p, 0, 0)), tab(C), tab(C), tab(lanes)],
        out_specs=pl.BlockSpec((None, S, lanes), lambda b, p: (b, 0, p)),
        out_shape=jax.ShapeDtypeStruct((B, S, W_RET), BF16),
        scratch_shapes=[pltpu.VMEM((S // C, lanes, lanes), F32), pltpu.VMEM((S // C, lanes, lanes), BF16)],
        compiler_params=_cparams("parallel", "parallel"),
        name="retention",
    )(h, h, h, h, gain.reshape(1, W_RET).astype(F32), din, kd, qd, cd)


def _out_proj_kernel(o1, o4, o16, l1, l4, l16, ret_ref, x_ref, w_ref, g_ref, b_ref, out_ref,
                     so4, so16, sl4, sl16):
    tm = x_ref.shape[0]
    n_slab = W_DIL // 128
    for d, pairs in ((4, ((o4, so4), (l4, sl4))), (16, ((o16, so16), (l16, sl16)))):
        for src, dst in pairs:
            for r in range(d):
                for c in range(n_slab):
                    c0 = r * W_DIL + c * 128
                    dst[c, pl.ds(r, tm // d, stride=d), :] = src[:, c0:c0 + 128]
    atts = []
    for c in range(n_slab):
        cs = slice(c * 128, (c + 1) * 128)
        a1, a4, a16 = l1[:, cs], sl4[c], sl16[c]
        m = jnp.maximum(jnp.maximum(a1, a4), a16)
        e1, e4, e16 = jnp.exp(a1 - m), jnp.exp(a4 - m), jnp.exp(a16 - m)
        att = (e1 * o1[:, cs] + e4 * so4[c] + e16 * so16[c]) / (e1 + e4 + e16)
        atts.append(att.astype(BF16))
    acc = jnp.dot(jnp.concatenate(atts, axis=1), w_ref[:W_DIL, :], preferred_element_type=F32)
    acc += jnp.dot(ret_ref[...], w_ref[W_DIL:, :], preferred_element_type=F32)
    out_ref[...] = _layer_norm(DEEPNORM_ALPHA * x_ref[...] + acc, g_ref[...], b_ref[...])


def _out_proj(os_, ls_, ret, x3, w_bf, g, b):
    B, S, D = x3.shape
    cls = lambda d: pl.BlockSpec((None, TM // d, d * W_DIL), lambda b_, i: (b_, i, 0))
    half = cls(1)
    full = pl.BlockSpec((None, TM, D), lambda b_, i: (b_, i, 0))
    vec = pl.BlockSpec((1, D), lambda b_, i: (0, 0))
    return pl.pallas_call(
        _out_proj_kernel,
        grid=(B, S // TM),
        in_specs=[half, cls(4), cls(16), half, cls(4), cls(16), half, full,
                  pl.BlockSpec((D, D), lambda b_, i: (0, 0)), vec, vec],
        out_specs=full,
        out_shape=jax.ShapeDtypeStruct((B, S, D), F32),
        scratch_shapes=[pltpu.VMEM((W_DIL // 128, TM, 128), F32)] * 4,
        compiler_params=_cparams("parallel", "parallel"),
        name="out_proj_ln1",
    )(*os_, *ls_, ret, x3, w_bf, g.reshape(1, D), b.reshape(1, D))


def _mem_kv_kernel(m_ref, wk_ref, wv_ref, k_ref, v_ref):
    mb = m_ref[...].astype(BF16)
    k_ref[...] = jnp.dot(mb, wk_ref[...], preferred_element_type=F32).astype(BF16)
    v_ref[...] = jnp.dot(mb, wv_ref[...], preferred_element_type=F32).astype(BF16)


def _mem_kv(mem, wk_bf, wv_bf):
    B, M, D = mem.shape
    blk = pl.BlockSpec((None, M, D), lambda b: (b, 0, 0))
    wsp = pl.BlockSpec((D, D), lambda b: (0, 0))
    shp = jax.ShapeDtypeStruct((B, M, D), BF16)
    return pl.pallas_call(
        _mem_kv_kernel, grid=(B,), in_specs=[blk, wsp, wsp], out_specs=[blk, blk], out_shape=[shp, shp],
        compiler_params=_cparams("parallel"), name="mem_kv",
    )(mem, wk_bf, wv_bf)


def _cross_attn_kernel(x_ref, k_ref, v_ref, wq_ref, wo_ref, g_ref, b_ref, out_ref):
    x = x_ref[...]
    hd = x.shape[-1] // N_HEADS_MEM
    q = (jnp.dot(x.astype(BF16), wq_ref[...], preferred_element_type=F32) * (hd ** -0.5)).astype(BF16)
    outs = []
    for hh in range(N_HEADS_MEM):
        cs = slice(hh * hd, (hh + 1) * hd)
        s = _nt_dot(q[:, cs], k_ref[:, cs])
        m = jnp.max(s, axis=-1, keepdims=True)
        p = jnp.exp(s - m)
        l = jnp.sum(p, axis=-1, keepdims=True)
        outs.append((jnp.dot(p.astype(BF16), v_ref[:, cs], preferred_element_type=F32) / l).astype(BF16))
    o = jnp.concatenate(outs, axis=-1)
    c = jnp.dot(o, wo_ref[...], preferred_element_type=F32)
    out_ref[...] = _layer_norm(DEEPNORM_ALPHA * x + c, g_ref[...], b_ref[...])


def _cross_attn(x3, kmem, vmem, wq_bf, wo_bf, g, b):
    B, S, D = x3.shape
    M = kmem.shape[1]
    xs = pl.BlockSpec((None, TM, D), lambda b_, i: (b_, i, 0))
    ms = pl.BlockSpec((None, M, D), lambda b_, i: (b_, 0, 0))
    ws = pl.BlockSpec((D, D), lambda b_, i: (0, 0))
    vs = pl.BlockSpec((1, D), lambda b_, i: (0, 0))
    return pl.pallas_call(
        _cross_attn_kernel,
        grid=(B, S // TM),
        in_specs=[xs, ms, ms, ws, ws, vs, vs],
        out_specs=xs,
        out_shape=jax.ShapeDtypeStruct((B, S, D), F32),
        compiler_params=_cparams("parallel", "parallel"),
        name="cross_attn_ln2",
    )(x3, kmem, vmem, wq_bf, wo_bf, g.reshape(1, D), b.reshape(1, D)).reshape(B * S, D)


def _router_kernel(x_ref, w_ref, b_ref, idx_ref, wt_ref):
    logits = lax.dot_general(w_ref[...], x_ref[...], (((1,), (1,)), ((), ())),
                             precision=lax.Precision.HIGHEST, preferred_element_type=F32) + b_ref[:, :1]
    e_iota = lax.broadcasted_iota(jnp.int32, logits.shape, 0)
    cur = logits
    vals, idxs = [], []
    for _ in range(TOP_K):
        m = jnp.max(cur, axis=0, keepdims=True)
        idx = jnp.min(jnp.where(cur == m, e_iota, N_EXPERTS), axis=0, keepdims=True)
        cur = jnp.where(e_iota == idx, -jnp.inf, cur)
        vals.append(m)
        idxs.append(idx)
    es = [jnp.exp(v - vals[0]) for v in vals]
    tot = es[0] + es[1] + es[2] + es[3]
    idx_ref[...] = jnp.concatenate(idxs, axis=0)
    wt_ref[...] = jnp.concatenate([e / tot for e in es], axis=0)


def _router(x2, router_w, router_b):
    N, D = x2.shape
    out = pl.BlockSpec((TOP_K, T_ROUTE), lambda i: (0, i))
    return pl.pallas_call(
        _router_kernel,
        grid=(N // T_ROUTE,),
        in_specs=[pl.BlockSpec((T_ROUTE, D), lambda i: (i, 0)),
                  pl.BlockSpec((N_EXPERTS, D), lambda i: (0, 0)),
                  pl.BlockSpec((N_EXPERTS, 128), lambda i: (0, 0))],
        out_specs=[out, out],
        out_shape=[jax.ShapeDtypeStruct((TOP_K, N), jnp.int32), jax.ShapeDtypeStruct((TOP_K, N), F32)],
        compiler_params=_cparams("parallel"),
        name="router_top4",
    )(x2, router_w.T.astype(F32), jnp.broadcast_to(router_b.astype(F32)[:, None], (N_EXPERTS, 128)))


def _rank_kernel(idx_ref, tri_ref, rank_ref, cnt_ref, carry_ref):
    @pl.when(pl.program_id(0) == 0)
    def _():
        carry_ref[...] = jnp.zeros_like(carry_ref)

    T = idx_ref.shape[1]
    e_iota = lax.broadcasted_iota(jnp.int32, (N_EXPERTS, T), 0)
    hot = [e_iota == idx_ref[k:k + 1, :] for k in range(TOP_K)]
    c = sum(h.astype(F32) for h in hot)
    before = jnp.dot(c.astype(BF16), tri_ref[...], preferred_element_type=F32) + carry_ref[:, :1]
    rank_ref[...] = jnp.concatenate(
        [jnp.sum(jnp.where(h, before, 0.0), axis=0, keepdims=True) for h in hot], axis=0).astype(jnp.int32)
    carry_ref[...] = carry_ref[...] + jnp.sum(c, axis=1, keepdims=True)
    cnt_ref[...] = carry_ref[...].astype(jnp.int32)


def _ranks(idx):
    N = idx.shape[1]
    T = T_RANK
    tri = (jnp.arange(T)[:, None] < jnp.arange(T)[None, :]).astype(BF16)
    blk = pl.BlockSpec((TOP_K, T), lambda i: (0, i))
    return pl.pallas_call(
        _rank_kernel,
        grid=(N // T,),
        in_specs=[blk, pl.BlockSpec((T, T), lambda i: (0, 0))],
        out_specs=[blk, pl.BlockSpec((N_EXPERTS, 128), lambda i: (0, 0))],
        out_shape=[jax.ShapeDtypeStruct((TOP_K, N), jnp.int32),
                   jax.ShapeDtypeStruct((N_EXPERTS, 128), jnp.int32)],
        scratch_shapes=[pltpu.VMEM((N_EXPERTS, 128), F32)],
        compiler_params=_cparams("arbitrary"),
        name="expert_ranks",
    )(idx, tri)


def _dest_kernel(idx_ref, rank_ref, start_ref, dest_ref):
    T = idx_ref.shape[1]
    e_iota = lax.broadcasted_iota(jnp.int32, (N_EXPERTS, T), 0)
    start = jnp.tile(start_ref[...], (1, T // 128))
    rows = [jnp.sum(jnp.where(e_iota == idx_ref[k:k + 1, :], start, 0), axis=0, keepdims=True)
            for k in range(TOP_K)]
    dest_ref[...] = jnp.concatenate(rows, axis=0) + rank_ref[...]


def _dests(idx, rank, group_start):
    N = idx.shape[1]
    T = T_ROUTE
    blk = pl.BlockSpec((TOP_K, T), lambda i: (0, i))
    return pl.pallas_call(
        _dest_kernel,
        grid=(N // T,),
        in_specs=[blk, blk, pl.BlockSpec((N_EXPERTS, 128), lambda i: (0, 0))],
        out_specs=blk,
        out_shape=jax.ShapeDtypeStruct((TOP_K, N), jnp.int32),
        compiler_params=_cparams("parallel"),
        name="expert_dests",
    )(idx, rank, jnp.broadcast_to(group_start[:, None], (N_EXPERTS, 128)))


def _pack_bf16_pairs(x):
    c = x.shape[1] // 2
    hi = pltpu.bitcast(x[:, :c].astype(BF16).astype(F32), jnp.uint32)
    lo = pltpu.bitcast(x[:, c:].astype(BF16).astype(F32), jnp.uint32)
    return (hi & jnp.uint32(0xFFFF0000)) | (lo >> 16)


def _unpack_bf16_pairs(u):
    hi = pltpu.bitcast(u & jnp.uint32(0xFFFF0000), F32).astype(BF16)
    lo = pltpu.bitcast(u << 16, F32).astype(BF16)
    return hi, lo


def _dispatch_kernel(dest_ref, x_ref, xs_ref, pk_ref, sem):
    T = x_ref.shape[0]
    pk_ref[...] = _pack_bf16_pairs(x_ref[...])

    def row_copy(t, k):
        return pltpu.make_async_copy(pk_ref.at[pl.ds(t, 1), :], xs_ref.at[pl.ds(dest_ref[k, t], 1), :], sem)

    def issue(t, carry):
        for k in range(TOP_K):
            row_copy(t, k).start()
        return carry

    lax.fori_loop(0, T, issue, 0)

    def drain(t, carry):
        for k in range(TOP_K):
            row_copy(t, k).wait()
        return carry

    lax.fori_loop(0, T, drain, 0)


def _dispatch(x2, dest, P):
    N, D = x2.shape
    T = T_MOVE
    return pl.pallas_call(
        _dispatch_kernel,
        grid=(N // T,),
        in_specs=[pl.BlockSpec((TOP_K, T), lambda i: (0, i), memory_space=pltpu.SMEM),
                  pl.BlockSpec((T, D), lambda i: (i, 0))],
        out_specs=pl.BlockSpec(memory_space=pl.ANY),
        out_shape=jax.ShapeDtypeStruct((P, D // 2), jnp.uint32),
        scratch_shapes=[pltpu.VMEM((T, D // 2), jnp.uint32), pltpu.SemaphoreType.DMA],
        compiler_params=_cparams("arbitrary"),
        name="moe_dispatch",
    )(dest, x2)


def _expert_kernel(be_ref, bv_ref, xs_ref, wg_ref, wu_ref, wd_ref, bg_ref, bu_ref, bd_ref, ys_ref):
    j = pl.program_id(0)
    valid = bv_ref[j]

    @pl.when(valid > 0)
    def _():
        u = xs_ref[...]
        row = lax.broadcasted_iota(jnp.int32, u.shape, 0)
        u = jnp.where(row < valid, u, jnp.uint32(0))
        hi, lo = _unpack_bf16_pairs(u)
        c = u.shape[1]

        def proj(w_ref, b_ref):
            return (jnp.dot(hi, w_ref[:c, :], preferred_element_type=F32)
                    + jnp.dot(lo, w_ref[c:, :], preferred_element_type=F32) + b_ref[...])

        gate = jnp.minimum(proj(wg_ref, bg_ref), SWIGLU_LIMIT)
        up = jnp.clip(proj(wu_ref, bu_ref), -SWIGLU_LIMIT, SWIGLU_LIMIT)
        hmid = gate * (1.0 / (1.0 + jnp.exp(-SWIGLU_ALPHA * gate))) * (up + 1.0)
        ys_ref[...] = jnp.dot(hmid.astype(BF16), wd_ref[...], preferred_element_type=F32) + bd_ref[...]


def _experts(xs, block_e, block_valid, wg, wu, wd, bg, bu, bd):
    P, C = xs.shape
    D = 2 * C
    F = wg.shape[-1]
    wspec = lambda a, c: pl.BlockSpec((None, a, c), lambda j, be, bv: (be[j], 0, 0))
    grid_spec = pltpu.PrefetchScalarGridSpec(
        num_scalar_prefetch=2,
        grid=(P // BM,),
        in_specs=[pl.BlockSpec((BM, C), lambda j, be, bv: (j, 0)),
                  wspec(D, F), wspec(D, F), wspec(F, D), wspec(1, F), wspec(1, F), wspec(1, D)],
        out_specs=pl.BlockSpec((BM, D), lambda j, be, bv: (j, 0)),
    )
    return pl.pallas_call(
        _expert_kernel,
        grid_spec=grid_spec,
        out_shape=jax.ShapeDtypeStruct((P, D), F32),
        compiler_params=_cparams("arbitrary"),
        name="moe_experts",
    )(block_e, block_valid, xs, wg, wu, wd, bg, bu, bd)


def _combine_kernel(dest_ref, x_ref, wt_ref, ys_ref, g_ref, b_ref, out_ref, buf_ref, sem):
    T = x_ref.shape[0]

    def row_copy(t, k):
        return pltpu.make_async_copy(ys_ref.at[pl.ds(dest_ref[k, t], 1), :],
                                     buf_ref.at[pl.ds(k * T + t, 1), :], sem)

    def issue(t, carry):
        for k in range(TOP_K):
            row_copy(t, k).start()
        return carry

    lax.fori_loop(0, T, issue, 0)

    def drain(t, carry):
        for k in range(TOP_K):
            row_copy(t, k).wait()
        return carry

    lax.fori_loop(0, T, drain, 0)

    wt = wt_ref[...]
    y = wt[:, 0:1] * buf_ref[0:T, :]
    for k in range(1, TOP_K):
        y = y + wt[:, k:k + 1] * buf_ref[k * T:(k + 1) * T, :]
    out_ref[...] = _layer_norm(DEEPNORM_ALPHA * x_ref[...] + y, g_ref[...], b_ref[...])


def _combine(x2, dest, wt_tok, ys, g, b):
    N, D = x2.shape
    T = T_MOVE
    vec = pl.BlockSpec((1, D), lambda i: (0, 0))
    return pl.pallas_call(
        _combine_kernel,
        grid=(N // T,),
        in_specs=[pl.BlockSpec((TOP_K, T), lambda i: (0, i), memory_space=pltpu.SMEM),
                  pl.BlockSpec((T, D), lambda i: (i, 0)),
                  pl.BlockSpec((T, TOP_K), lambda i: (i, 0)),
                  pl.BlockSpec(memory_space=pl.ANY), vec, vec],
        out_specs=pl.BlockSpec((T, D), lambda i: (i, 0)),
        out_shape=jax.ShapeDtypeStruct((N, D), F32),
        scratch_shapes=[pltpu.VMEM((TOP_K * T, D), F32), pltpu.SemaphoreType.DMA],
        compiler_params=_cparams("arbitrary"),
        name="moe_combine_ln3",
    )(dest, x2, wt_tok, ys, g.reshape(1, D), b.reshape(1, D))


def _moe(x2, router_w, router_b, w_gate, b_gate, w_up, b_up, w_down, b_down, ln_g, ln_b):
    N, D = x2.shape
    idx, wt = _router(x2, router_w, router_b)
    rank, cnt = _ranks(idx)
    counts = cnt[:, 0]
    blocks = (counts + BM - 1) // BM
    blk_end = jnp.cumsum(blocks)
    blk_start = blk_end - blocks
    dest = _dests(idx, rank, (blk_start * BM).astype(jnp.int32))
    n_blocks = -(-N * TOP_K // BM) + N_EXPERTS
    bi = jnp.arange(n_blocks, dtype=jnp.int32)
    be_raw = jnp.sum(bi[:, None] >= blk_end[None, :], axis=1).astype(jnp.int32)
    last_e = jnp.max(jnp.where(counts > 0, jnp.arange(N_EXPERTS), 0)).astype(jnp.int32)
    block_e = jnp.minimum(be_raw, last_e)
    in_use = bi < blk_end[-1]
    block_valid = jnp.where(in_use, jnp.clip(counts[block_e] - (bi - blk_start[block_e]) * BM, 0, BM), 0)
    xs = _dispatch(x2, dest, n_blocks * BM)
    f3 = lambda t: t.astype(F32)[:, None, :]
    ys = _experts(xs, block_e, block_valid.astype(jnp.int32),
                  w_gate.astype(BF16), w_up.astype(BF16), w_down.astype(BF16),
                  f3(b_gate), f3(b_up), f3(b_down))
    return _combine(x2, dest, wt.T, ys, ln_g, ln_b)


def kernel(x, mem, w_in, ret_norm_g, w_out, ln1_g, ln1_b, mem_wq, mem_wk, mem_wv, mem_wo, ln2_g, ln2_b,
           router_w, router_b, w_gate, b_gate, w_up, b_up, w_down, b_down, ln3_g, ln3_b):
    B, S, D = x.shape
    assert [d for _, d in DILATED_PATTERNS] == [1, 4, 16] and all(w // d == BAND for w, d in DILATED_PATTERNS)
    for l in range(w_in.shape[0]):
        h, qkv4, qkv16 = _in_proj(x, w_in[l].astype(BF16))
        pats = [_band_attn(h, B, S, 1, N_IN // GROUP),
                _band_attn(qkv4, B, S // 4, 4, 3), _band_attn(qkv16, B, S // 16, 16, 3)]
        ret = _retention(h, ret_norm_g[l])
        x1 = _out_proj([p[0] for p in pats], [p[1] for p in pats], ret, x,
                       w_out[l].astype(BF16), ln1_g[l], ln1_b[l])
        kmem, vmem = _mem_kv(mem, mem_wk[l].astype(BF16), mem_wv[l].astype(BF16))
        x2 = _cross_attn(x1, kmem, vmem, mem_wq[l].astype(BF16), mem_wo[l].astype(BF16),
                         ln2_g[l], ln2_b[l])
        x = _moe(x2, router_w[l], router_b[l], w_gate[l], b_gate[l], w_up[l], b_up[l],
                 w_down[l], b_down[l], ln3_g[l], ln3_b[l]).reshape(B, S, D)
    return x
```

```python
import functools

import jax
import jax.numpy as jnp
from jax import lax
from jax.experimental import pallas as pl
from jax.experimental.pallas import tpu as pltpu

F32 = jnp.float32
BF16 = jnp.bfloat16

HEAD_DIM = 64
N_HEADS_DIL = 8
N_HEADS_RET = 8
W_DIL = N_HEADS_DIL * HEAD_DIM
W_RET = N_HEADS_RET * HEAD_DIM
N_IN = 3 * W_DIL + 4 * W_RET
DILATED_PATTERNS = ((128, 1), (512, 4), (2048, 16))
BAND = 128
ROPE_THETA = 500000.0
ROT_DIM = HEAD_DIM // 4
RET_THETA = 10000.0
RET_CHUNK = 128
N_HEADS_MEM = 4
N_EXPERTS = 32
TOP_K = 4
SWIGLU_LIMIT = 7.0
SWIGLU_ALPHA = 1.702
LN_EPS = 1e-5
DEPTH = 1
DEEPNORM_ALPHA = (2 * DEPTH) ** 0.25

GROUP = 512
TM = 512
TQ = 512
T_ROUTE = 1024
T_RANK = 512
T_MOVE = 256
BM = 512
VMEM_LIMIT = 56 * 1024 * 1024
NEG = -1e30


def _cparams(*sem):
    return pltpu.CompilerParams(dimension_semantics=sem, vmem_limit_bytes=VMEM_LIMIT)


def _nt_dot(a, b):
    return lax.dot_general(a, b, (((1,), (1,)), ((), ())), preferred_element_type=F32)


def _layer_norm(y, g, b):
    mu = jnp.mean(y, axis=-1, keepdims=True)
    d = y - mu
    var = jnp.mean(d * d, axis=-1, keepdims=True)
    return d * lax.rsqrt(var + LN_EPS) * g + b


def _in_proj_kernel(x_ref, w_ref, ca_ref, la_ref, ha_ref, cr_ref, lr_ref, hr_ref, o_ref, o4_ref, o16_ref,
                    acc_ref):
    xb = x_ref[...].astype(BF16)
    rep = GROUP // 128
    tm = x_ref.shape[0]
    qkv = 3 * W_DIL

    def rot(acc, c_ref, lo_ref, hi_ref, half):
        c = jnp.tile(c_ref[...], (1, rep))
        lo = jnp.tile(lo_ref[...], (1, rep))
        hi = jnp.tile(hi_ref[...], (1, rep))
        up = pltpu.roll(acc, GROUP - half, axis=1)
        dn = pltpu.roll(acc, half, axis=1)
        return acc * c + up * lo + dn * hi

    for g in range(N_IN // GROUP):
        acc = jnp.dot(xb, w_ref[:, g * GROUP:(g + 1) * GROUP], preferred_element_type=F32)
        if g == 0:
            acc = rot(acc, ca_ref, la_ref, ha_ref, ROT_DIM // 2) * (HEAD_DIM ** -0.5)
        elif g == 1:
            acc = rot(acc, ca_ref, la_ref, ha_ref, ROT_DIM // 2)
        elif g == 3:
            acc = rot(acc, cr_ref, lr_ref, hr_ref, HEAD_DIM // 2)
        elif g == 4:
            acc = rot(acc, cr_ref, lr_ref, hr_ref, HEAD_DIM // 2) * (HEAD_DIM ** -0.5)
        o_ref[:, g * GROUP:(g + 1) * GROUP] = acc.astype(BF16)
        if g < 3:
            for c in range(rep):
                acc_ref[c] = acc[:, c * 128:(c + 1) * 128]
            for d, od_ref in ((4, o4_ref), (16, o16_ref)):
                for r in range(d):
                    for c in range(rep):
                        c0 = r * qkv + g * GROUP + c * 128
                        od_ref[:, c0:c0 + 128] = acc_ref[c, pl.ds(r, tm // d, stride=d), :].astype(BF16)


def _rotary_tables(S, theta, rot_dim):
    half = rot_dim // 2
    inv = 1.0 / (theta ** (jnp.arange(half, dtype=F32) / half))
    ang = jnp.arange(S, dtype=F32)[:, None] * inv[None, :]
    cos, sin = jnp.cos(ang), jnp.sin(ang)
    pad = HEAD_DIM - rot_dim
    c = jnp.concatenate([cos, cos, jnp.ones((S, pad), F32)], axis=1)
    lo = jnp.concatenate([-sin, jnp.zeros((S, half + pad), F32)], axis=1)
    hi = jnp.concatenate([jnp.zeros((S, half), F32), sin, jnp.zeros((S, pad), F32)], axis=1)
    two = lambda t: jnp.concatenate([t, t], axis=1)
    return two(c), two(lo), two(hi)


def _in_proj(x3, w_bf):
    B, S, D = x3.shape
    tabs = _rotary_tables(S, ROPE_THETA, ROT_DIM) + _rotary_tables(S, RET_THETA, HEAD_DIM)
    tab_spec = pl.BlockSpec((TM, 128), lambda b, i: (i, 0))
    qkv = 3 * W_DIL
    cls_spec = lambda d: pl.BlockSpec((None, TM // d, d * qkv), lambda b, i: (b, i, 0))
    cls_shape = lambda d: jax.ShapeDtypeStruct((B, S // d, d * qkv), BF16)
    return pl.pallas_call(
        _in_proj_kernel,
        grid=(B, S // TM),
        in_specs=[pl.BlockSpec((None, TM, D), lambda b, i: (b, i, 0)),
                  pl.BlockSpec((D, N_IN), lambda b, i: (0, 0))] + [tab_spec] * 6,
        out_specs=[pl.BlockSpec((None, TM, N_IN), lambda b, i: (b, i, 0)), cls_spec(4), cls_spec(16)],
        out_shape=[jax.ShapeDtypeStruct((B, S, N_IN), BF16), cls_shape(4), cls_shape(16)],
        scratch_shapes=[pltpu.VMEM((GROUP // 128, TM, 128), F32)],
        compiler_params=_cparams("parallel", "parallel"),
        name="in_proj",
    )(x3, w_bf, *tabs)


def _band_attn_kernel(q_ref, kc_ref, kp_ref, vc_ref, vp_ref, o_ref, l_ref,
                      k_all, v_all, s_scr, p_scr, r_scr):
    j = pl.program_id(2)
    tq = q_ref.shape[0]
    nsub = tq // BAND
    pair = 2 * HEAD_DIM
    n_pairs = N_HEADS_DIL // 2
    k_all[0:BAND, :] = kp_ref[...]
    k_all[BAND:, :] = kc_ref[...]
    v_all[0:BAND, :] = vp_ref[...]
    v_all[BAND:, :] = vc_ref[...]
    qi = lax.broadcasted_iota(jnp.int32, (BAND, 2 * BAND), 0)
    kj = lax.broadcasted_iota(jnp.int32, (BAND, 2 * BAND), 1)
    dist = qi + BAND - kj
    band = (dist >= 0) & (dist <= BAND)
    bias = jnp.where(band, 0.0, NEG)
    bias_first = jnp.where(band & ((kj >= BAND) | (j > 0)), 0.0, NEG)
    low = lax.broadcasted_iota(jnp.int32, (BAND, pair), 1) < HEAD_DIM
    zero = jnp.zeros((BAND, pair), BF16)

    for pr in range(n_pairs):
        cs = slice(pr * pair, (pr + 1) * pair)
        for n in range(nsub):
            q = q_ref[n * BAND:(n + 1) * BAND, cs]
            kk = k_all[n * BAND:(n + 2) * BAND, cs]
            b = bias_first if n == 0 else bias
            i = pr * nsub + n
            s_scr[2 * i] = _nt_dot(jnp.where(low, q, zero), kk) + b
            s_scr[2 * i + 1] = _nt_dot(jnp.where(low, zero, q), kk) + b

    for pr in range(n_pairs):
        cs = slice(pr * pair, (pr + 1) * pair)
        for n in range(nsub):
            i = pr * nsub + n
            stats = []
            for a in range(2):
                s = s_scr[2 * i + a]
                m = jnp.max(s, axis=-1, keepdims=True)
                p = jnp.exp(s - m)
                l = jnp.sum(p, axis=-1, keepdims=True)
                p_scr[2 * i + a] = p.astype(BF16)
                stats.append((m, l))
            (m0, l0), (m1, l1) = stats
            r_scr[i] = jnp.where(low, 1.0 / l0, 1.0 / l1)
            l_ref[n * BAND:(n + 1) * BAND, cs] = jnp.where(low, m0 + jnp.log(l0), m1 + jnp.log(l1))

    for pr in range(n_pairs):
        cs = slice(pr * pair, (pr + 1) * pair)
        for n in range(nsub):
            i = pr * nsub + n
            vv = v_all[n * BAND:(n + 2) * BAND, cs]
            o0 = jnp.dot(p_scr[2 * i], vv, preferred_element_type=F32)
            o1 = jnp.dot(p_scr[2 * i + 1], vv, preferred_element_type=F32)
            o_ref[n * BAND:(n + 1) * BAND, cs] = jnp.where(low, o0, o1) * r_scr[i]


def _band_attn(src, B, L, d, gpr):
    tq = min(TQ, L)
    assert L % tq == 0
    sub = tq // BAND
    cur = lambda g: pl.BlockSpec((None, tq, GROUP), lambda b, r, j: (b, j, r * gpr + g))
    prev = lambda g: pl.BlockSpec((None, BAND, GROUP),
                                  lambda b, r, j: (b, jnp.maximum(j * sub - 1, 0), r * gpr + g))
    out_spec = pl.BlockSpec((None, tq, W_DIL), lambda b, r, j: (b, j, r))
    shp = jax.ShapeDtypeStruct((B, L, d * W_DIL), F32)
    n_blk = (N_HEADS_DIL // 2) * sub
    return pl.pallas_call(
        _band_attn_kernel,
        grid=(B, d, L // tq),
        in_specs=[cur(0), cur(1), prev(1), cur(2), prev(2)],
        out_specs=[out_spec, out_spec],
        out_shape=[shp, shp],
        scratch_shapes=[pltpu.VMEM((BAND + tq, GROUP), BF16), pltpu.VMEM((BAND + tq, GROUP), BF16),
                        pltpu.VMEM((2 * n_blk, BAND, 2 * BAND), F32),
                        pltpu.VMEM((2 * n_blk, BAND, 2 * BAND), BF16),
                        pltpu.VMEM((n_blk, BAND, 2 * HEAD_DIM), F32)],
        compiler_params=_cparams("parallel", "parallel", "parallel"),
        name=f"band_attn_d{d}",
    )(src, src, src, src, src)


RET_UNROLL = 4


def _retention_kernel(q_ref, k_ref, v_ref, g_ref, gain_ref, din_ref, kd_ref, qd_ref, cd_ref,
                      o_ref, kv_ref, st_ref):
    C = RET_CHUNK
    n_iter = q_ref.shape[0] // (C * RET_UNROLL)
    pair = 2 * HEAD_DIM
    low = lax.broadcasted_iota(jnp.int32, (C, pair), 1) < HEAD_DIM

    def rows(i, u):
        return pl.ds(pl.multiple_of(i * (C * RET_UNROLL), C * RET_UNROLL) + u * C, C)

    def kv_body(i, carry):
        for u in range(RET_UNROLL):
            kd = (k_ref[rows(i, u), :].astype(F32) * kd_ref[...]).astype(BF16)
            kv_ref[i * RET_UNROLL + u] = lax.dot_general(kd, v_ref[rows(i, u), :], (((0,), (0,)), ((), ())),
                                                         preferred_element_type=F32)
        return carry

    lax.fori_loop(0, n_iter, kv_body, 0)

    def scan_body(c, state):
        st_ref[c] = state.astype(BF16)
        return state * cd_ref[...] + kv_ref[c]

    lax.fori_loop(0, n_iter * RET_UNROLL, scan_body, jnp.zeros((pair, pair), F32))

    def out_body(i, carry):
        for u in range(RET_UNROLL):
            q = q_ref[rows(i, u), :]
            k = k_ref[rows(i, u), :]
            v = v_ref[rows(i, u), :]
            prev = st_ref[i * RET_UNROLL + u]
            ys = []
            for a in range(2):
                qa = jnp.where(low, q, jnp.zeros_like(q)) if a == 0 else jnp.where(low, jnp.zeros_like(q), q)
                s = _nt_dot(qa, k) * din_ref[a]
                inner = jnp.dot(s.astype(BF16), v, preferred_element_type=F32)
                qd = (qa.astype(F32) * qd_ref[...]).astype(BF16)
                ys.append(inner + jnp.dot(qd, prev, preferred_element_type=F32))
            y = jnp.where(low, ys[0], ys[1])
            inv = 1.0 / HEAD_DIM
            half_sum = lambda t: jnp.where(low, jnp.sum(jnp.where(low, t, 0.0), axis=-1, keepdims=True),
                                           jnp.sum(jnp.where(low, 0.0, t), axis=-1, keepdims=True))
            dlt = y - half_sum(y) * inv
            var = half_sum(dlt * dlt) * inv
            yn = dlt * lax.rsqrt(var + LN_EPS) * gain_ref[...]
            gr = g_ref[rows(i, u), :].astype(F32)
            o_ref[rows(i, u), :] = (yn * (gr / (1.0 + jnp.exp(-gr)))).astype(BF16)
        return carry

    lax.fori_loop(0, n_iter, out_body, 0)


def _retention_tables():
    H, C = N_HEADS_RET, RET_CHUNK
    log_g = jnp.log1p(-(2.0 ** (-5.0 - jnp.arange(H, dtype=F32))))
    idx = jnp.arange(C, dtype=F32)
    diff = idx[:, None] - idx[None, :]
    inner_decay = jnp.exp(log_g[:, None, None] * jnp.maximum(diff, 0.0)) * (diff >= 0)
    k_decay = jnp.exp(log_g[:, None] * (C - 1 - idx))
    q_decay = jnp.exp(log_g[:, None] * (idx + 1.0))
    chunk_decay = jnp.exp(log_g * C)
    lanes = lambda t: jnp.repeat(t.reshape(H // 2, 2, C).transpose(0, 2, 1), HEAD_DIM, axis=2)
    cd = jnp.broadcast_to(jnp.repeat(chunk_decay.reshape(H // 2, 2), HEAD_DIM, axis=1)[:, :, None],
                          (H // 2, 2 * HEAD_DIM, 2 * HEAD_DIM))
    return inner_decay.astype(F32), lanes(k_decay), lanes(q_decay), cd


def _retention(h, gain):
    B, S, _ = h.shape
    lanes = 2 * HEAD_DIM
    col0 = 3 * W_DIL // lanes
    per = W_RET // lanes
    col = lambda g: pl.BlockSpec((None, S, lanes), lambda b, p: (b, 0, col0 + g * per + p))
    din, kd, qd, cd = _retention_tables()
    C = RET_CHUNK
    assert S % (C * RET_UNROLL) == 0
    tab = lambda a: pl.BlockSpec((None, a, lanes), lambda b, p: (p, 0, 0))
    return pl.pallas_call(
        _retention_kernel,
        grid=(B, per),
        in_specs=[col(0), col(1), col(2), col(3),
                  pl.BlockSpec((1, lanes), lambda b, p: (0, p)),
                  pl.BlockSpec((2, C, C), lambda b, p: (p, 0, 0)), tab(C), tab(C), tab(lanes)],
        out_specs=pl.BlockSpec((None, S, lanes), lambda b, p: (b, 0, p)),
        out_shape=jax.ShapeDtypeStruct((B, S, W_RET), BF16),
        scratch_shapes=[pltpu.VMEM((S // C, lanes, lanes), F32), pltpu.VMEM((S // C, lanes, lanes), BF16)],
        compiler_params=_cparams("parallel", "parallel"),
        name="retention",
    )(h, h, h, h, gain.reshape(1, W_RET).astype(F32), din, kd, qd, cd)


def _out_proj_kernel(o1, o4, o16, l1, l4, l16, ret_ref, x_ref, w_ref, g_ref, b_ref, out_ref,
                     so4, so16, sl4, sl16):
    tm = x_ref.shape[0]
    n_slab = W_DIL // 128
    for d, pairs in ((4, ((o4, so4), (l4, sl4))), (16, ((o16, so16), (l16, sl16)))):
        for src, dst in pairs:
            for r in range(d):
                for c in range(n_slab):
                    c0 = r * W_DIL + c * 128
                    dst[c, pl.ds(r, tm // d, stride=d), :] = src[:, c0:c0 + 128]
    atts = []
    for c in range(n_slab):
        cs = slice(c * 128, (c + 1) * 128)
        a1, a4, a16 = l1[:, cs], sl4[c], sl16[c]
        m = jnp.maximum(jnp.maximum(a1, a4), a16)
        e1, e4, e16 = jnp.exp(a1 - m), jnp.exp(a4 - m), jnp.exp(a16 - m)
        att = (e1 * o1[:, cs] + e4 * so4[c] + e16 * so16[c]) / (e1 + e4 + e16)
        atts.append(att.astype(BF16))
    acc = jnp.dot(jnp.concatenate(atts, axis=1), w_ref[:W_DIL, :], preferred_element_type=F32)
    acc += jnp.dot(ret_ref[...], w_ref[W_DIL:, :], preferred_element_type=F32)
    out_ref[...] = _layer_norm(DEEPNORM_ALPHA * x_ref[...] + acc, g_ref[...], b_ref[...])


def _out_proj(os_, ls_, ret, x3, w_bf, g, b):
    B, S, D = x3.shape
    cls = lambda d: pl.BlockSpec((None, TM // d, d * W_DIL), lambda b_, i: (b_, i, 0))
    half = cls(1)
    full = pl.BlockSpec((None, TM, D), lambda b_, i: (b_, i, 0))
    vec = pl.BlockSpec((1, D), lambda b_, i: (0, 0))
    return pl.pallas_call(
        _out_proj_kernel,
        grid=(B, S // TM),
        in_specs=[half, cls(4), cls(16), half, cls(4), cls(16), half, full,
                  pl.BlockSpec((D, D), lambda b_, i: (0, 0)), vec, vec],
        out_specs=full,
        out_shape=jax.ShapeDtypeStruct((B, S, D), F32),
        scratch_shapes=[pltpu.VMEM((W_DIL // 128, TM, 128), F32)] * 4,
        compiler_params=_cparams("parallel", "parallel"),
        name="out_proj_ln1",
    )(*os_, *ls_, ret, x3, w_bf, g.reshape(1, D), b.reshape(1, D))


def _mem_kv_kernel(m_ref, wk_ref, wv_ref, k_ref, v_ref):
    mb = m_ref[...].astype(BF16)
    k_ref[...] = jnp.dot(mb, wk_ref[...], preferred_element_type=F32).astype(BF16)
    v_ref[...] = jnp.dot(mb, wv_ref[...], preferred_element_type=F32).astype(BF16)


def _mem_kv(mem, wk_bf, wv_bf):
    B, M, D = mem.shape
    blk = pl.BlockSpec((None, M, D), lambda b: (b, 0, 0))
    wsp = pl.BlockSpec((D, D), lambda b: (0, 0))
    shp = jax.ShapeDtypeStruct((B, M, D), BF16)
    return pl.pallas_call(
        _mem_kv_kernel, grid=(B,), in_specs=[blk, wsp, wsp], out_specs=[blk, blk], out_shape=[shp, shp],
        compiler_params=_cparams("parallel"), name="mem_kv",
    )(mem, wk_bf, wv_bf)


def _cross_attn_kernel(x_ref, k_ref, v_ref, wq_ref, wo_ref, g_ref, b_ref, out_ref):
    x = x_ref[...]
    hd = x.shape[-1] // N_HEADS_MEM
    q = (jnp.dot(x.astype(BF16), wq_ref[...], preferred_element_type=F32) * (hd ** -0.5)).astype(BF16)
    outs = []
    for hh in range(N_HEADS_MEM):
        cs = slice(hh * hd, (hh + 1) * hd)
        s = _nt_dot(q[:, cs], k_ref[:, cs])
        m = jnp.max(s, axis=-1, keepdims=True)
        p = jnp.exp(s - m)
        l = jnp.sum(p, axis=-1, keepdims=True)
        outs.append((jnp.dot(p.astype(BF16), v_ref[:, cs], preferred_element_type=F32) / l).astype(BF16))
    o = jnp.concatenate(outs, axis=-1)
    c = jnp.dot(o, wo_ref[...], preferred_element_type=F32)
    out_ref[...] = _layer_norm(DEEPNORM_ALPHA * x + c, g_ref[...], b_ref[...])


def _cross_attn(x3, kmem, vmem, wq_bf, wo_bf, g, b):
    B, S, D = x3.shape
    M = kmem.shape[1]
    xs = pl.BlockSpec((None, TM, D), lambda b_, i: (b_, i, 0))
    ms = pl.BlockSpec((None, M, D), lambda b_, i: (b_, 0, 0))
    ws = pl.BlockSpec((D, D), lambda b_, i: (0, 0))
    vs = pl.BlockSpec((1, D), lambda b_, i: (0, 0))
    return pl.pallas_call(
        _cross_attn_kernel,
        grid=(B, S // TM),
        in_specs=[xs, ms, ms, ws, ws, vs, vs],
        out_specs=xs,
        out_shape=jax.ShapeDtypeStruct((B, S, D), F32),
        compiler_params=_cparams("parallel", "parallel"),
        name="cross_attn_ln2",
    )(x3, kmem, vmem, wq_bf, wo_bf, g.reshape(1, D), b.reshape(1, D)).reshape(B * S, D)


def _router_kernel(x_ref, w_ref, b_ref, idx_ref, wt_ref):
    logits = lax.dot_general(w_ref[...], x_ref[...], (((1,), (1,)), ((), ())),
                             precision=lax.Precision.HIGHEST, preferred_element_type=F32) + b_ref[:, :1]
    e_iota = lax.broadcasted_iota(jnp.int32, logits.shape, 0)
    cur = logits
    vals, idxs = [], []
    for _ in range(TOP_K):
        m = jnp.max(cur, axis=0, keepdims=True)
        idx = jnp.min(jnp.where(cur == m, e_iota, N_EXPERTS), axis=0, keepdims=True)
        cur = jnp.where(e_iota == idx, -jnp.inf, cur)
        vals.append(m)
        idxs.append(idx)
    es = [jnp.exp(v - vals[0]) for v in vals]
    tot = es[0] + es[1] + es[2] + es[3]
    idx_ref[...] = jnp.concatenate(idxs, axis=0)
    wt_ref[...] = jnp.concatenate([e / tot for e in es], axis=0)


def _router(x2, router_w, router_b):
    N, D = x2.shape
    out = pl.BlockSpec((TOP_K, T_ROUTE), lambda i: (0, i))
    return pl.pallas_call(
        _router_kernel,
        grid=(N // T_ROUTE,),
        in_specs=[pl.BlockSpec((T_ROUTE, D), lambda i: (i, 0)),
                  pl.BlockSpec((N_EXPERTS, D), lambda i: (0, 0)),
                  pl.BlockSpec((N_EXPERTS, 128), lambda i: (0, 0))],
        out_specs=[out, out],
        out_shape=[jax.ShapeDtypeStruct((TOP_K, N), jnp.int32), jax.ShapeDtypeStruct((TOP_K, N), F32)],
        compiler_params=_cparams("parallel"),
        name="router_top4",
    )(x2, router_w.T.astype(F32), jnp.broadcast_to(router_b.astype(F32)[:, None], (N_EXPERTS, 128)))


def _rank_kernel(idx_ref, tri_ref, rank_ref, cnt_ref, carry_ref):
    @pl.when(pl.program_id(0) == 0)
    def _():
        carry_ref[...] = jnp.zeros_like(carry_ref)

    T = idx_ref.shape[1]
    e_iota = lax.broadcasted_iota(jnp.int32, (N_EXPERTS, T), 0)
    hot = [e_iota == idx_ref[k:k + 1, :] for k in range(TOP_K)]
    c = sum(h.astype(F32) for h in hot)
    before = jnp.dot(c.astype(BF16), tri_ref[...], preferred_element_type=F32) + carry_ref[:, :1]
    rank_ref[...] = jnp.concatenate(
        [jnp.sum(jnp.where(h, before, 0.0), axis=0, keepdims=True) for h in hot], axis=0).astype(jnp.int32)
    carry_ref[...] = carry_ref[...] + jnp.sum(c, axis=1, keepdims=True)
    cnt_ref[...] = carry_ref[...].astype(jnp.int32)


def _ranks(idx):
    N = idx.shape[1]
    T = T_RANK
    tri = (jnp.arange(T)[:, None] < jnp.arange(T)[None, :]).astype(BF16)
    blk = pl.BlockSpec((TOP_K, T), lambda i: (0, i))
    return pl.pallas_call(
        _rank_kernel,
        grid=(N // T,),
        in_specs=[blk, pl.BlockSpec((T, T), lambda i: (0, 0))],
        out_specs=[blk, pl.BlockSpec((N_EXPERTS, 128), lambda i: (0, 0))],
        out_shape=[jax.ShapeDtypeStruct((TOP_K, N), jnp.int32),
                   jax.ShapeDtypeStruct((N_EXPERTS, 128), jnp.int32)],
        scratch_shapes=[pltpu.VMEM((N_EXPERTS, 128), F32)],
        compiler_params=_cparams("arbitrary"),
        name="expert_ranks",
    )(idx, tri)


def _dest_kernel(idx_ref, rank_ref, start_ref, dest_ref):
    T = idx_ref.shape[1]
    e_iota = lax.broadcasted_iota(jnp.int32, (N_EXPERTS, T), 0)
    start = jnp.tile(start_ref[...], (1, T // 128))
    rows = [jnp.sum(jnp.where(e_iota == idx_ref[k:k + 1, :], start, 0), axis=0, keepdims=True)
            for k in range(TOP_K)]
    dest_ref[...] = jnp.concatenate(rows, axis=0) + rank_ref[...]


def _dests(idx, rank, group_start):
    N = idx.shape[1]
    T = T_ROUTE
    blk = pl.BlockSpec((TOP_K, T), lambda i: (0, i))
    return pl.pallas_call(
        _dest_kernel,
        grid=(N // T,),
        in_specs=[blk, blk, pl.BlockSpec((N_EXPERTS, 128), lambda i: (0, 0))],
        out_specs=blk,
        out_shape=jax.ShapeDtypeStruct((TOP_K, N), jnp.int32),
        compiler_params=_cparams("parallel"),
        name="expert_dests",
    )(idx, rank, jnp.broadcast_to(group_start[:, None], (N_EXPERTS, 128)))


MOVE_UNROLL = 8
SLAB_ROWS = 8


def _for_rows(T, fn):
    def body(g, carry):
        for u in range(MOVE_UNROLL):
            for k in range(TOP_K):
                t = g * MOVE_UNROLL + u
                fn(t, k, g * (MOVE_UNROLL * TOP_K) + (u * TOP_K + k))
        return carry

    lax.fori_loop(0, T // MOVE_UNROLL, body, 0)


def _slab(ref_at, row8):
    return ref_at.at[pl.ds(pl.multiple_of(row8, SLAB_ROWS), SLAB_ROWS), :]


def _dispatch_kernel(dcur_ref, dprev_ref, x_ref, xs_ref, pk_ref, sem):
    i = pl.program_id(0)
    n = pl.num_programs(0)
    T = x_ref.shape[0]
    slot = i % 2
    for c in range(SLAB_ROWS):
        pk_ref[slot, pl.ds(c, T, stride=SLAB_ROWS), :] = x_ref[:, c * 128:(c + 1) * 128]

    def row_copy(sl, dref, t, j):
        return pltpu.make_async_copy(_slab(pk_ref.at[sl], t * SLAB_ROWS), _slab(xs_ref, dref[j]), sem.at[sl])

    _for_rows(T, lambda t, k, j: row_copy(slot, dcur_ref, t, j).start(priority=k % 2))

    @pl.when(i > 0)
    def _():
        _for_rows(T, lambda t, k, j: row_copy(1 - slot, dprev_ref, t, j).wait())

    @pl.when(i == n - 1)
    def _():
        _for_rows(T, lambda t, k, j: row_copy(slot, dcur_ref, t, j).wait())


def _dispatch(x2, dest8, P):
    N, D = x2.shape
    assert D == SLAB_ROWS * 128
    T = T_MOVE
    smem = lambda f: pl.BlockSpec((TOP_K * T,), f, memory_space=pltpu.SMEM)
    return pl.pallas_call(
        _dispatch_kernel,
        grid=(N // T,),
        in_specs=[smem(lambda i: (i,)), smem(lambda i: (jnp.maximum(i - 1, 0),)),
                  pl.BlockSpec((T, D), lambda i: (i, 0))],
        out_specs=pl.BlockSpec(memory_space=pl.ANY),
        out_shape=jax.ShapeDtypeStruct((P * SLAB_ROWS, 128), F32),
        scratch_shapes=[pltpu.VMEM((2, T * SLAB_ROWS, 128), F32), pltpu.SemaphoreType.DMA((2,))],
        compiler_params=_cparams("arbitrary"),
        name="moe_dispatch",
    )(dest8, dest8, x2)


def _expert_kernel(be_ref, bv_ref, xs_ref, wg_ref, wu_ref, wd_ref, bg_ref, bu_ref, bd_ref, ys_ref):
    j = pl.program_id(0)
    valid = bv_ref[j]

    @pl.when(valid > 0)
    def _():
        bm = xs_ref.shape[0] // SLAB_ROWS
        row = lax.broadcasted_iota(jnp.int32, (bm, 128), 0)
        xb = jnp.concatenate(
            [jnp.where(row < valid, xs_ref[pl.ds(c, bm, stride=SLAB_ROWS), :], 0.0).astype(BF16)
             for c in range(SLAB_ROWS)], axis=1)

        def proj(w_ref, b_ref):
            return jnp.dot(xb, w_ref[...], preferred_element_type=F32) + b_ref[...]

        gate = jnp.minimum(proj(wg_ref, bg_ref), SWIGLU_LIMIT)
        up = jnp.clip(proj(wu_ref, bu_ref), -SWIGLU_LIMIT, SWIGLU_LIMIT)
        hmid = gate * (1.0 / (1.0 + jnp.exp(-SWIGLU_ALPHA * gate))) * (up + 1.0)
        out = jnp.dot(hmid.astype(BF16), wd_ref[...], preferred_element_type=F32) + bd_ref[...]
        for c in range(SLAB_ROWS):
            ys_ref[pl.ds(c, bm, stride=SLAB_ROWS), :] = out[:, c * 128:(c + 1) * 128]


def _experts(xs, block_e, block_valid, wg, wu, wd, bg, bu, bd):
    D = SLAB_ROWS * 128
    F = wg.shape[-1]
    wspec = lambda a, c: pl.BlockSpec((None, a, c), lambda j, be, bv: (be[j], 0, 0))
    slabs = pl.BlockSpec((BM * SLAB_ROWS, 128), lambda j, be, bv: (j, 0))
    grid_spec = pltpu.PrefetchScalarGridSpec(
        num_scalar_prefetch=2,
        grid=(xs.shape[0] // (BM * SLAB_ROWS),),
        in_specs=[slabs, wspec(D, F), wspec(D, F), wspec(F, D), wspec(1, F), wspec(1, F), wspec(1, D)],
        out_specs=slabs,
    )
    return pl.pallas_call(
        _expert_kernel,
        grid_spec=grid_spec,
        out_shape=jax.ShapeDtypeStruct(xs.shape, F32),
        compiler_params=_cparams("arbitrary"),
        name="moe_experts",
    )(block_e, block_valid, xs, wg, wu, wd, bg, bu, bd)


def _combine_kernel(dcur_ref, dnext_ref, x_ref, wt_ref, ys_ref, g_ref, b_ref, out_ref, buf_ref, sem):
    i = pl.program_id(0)
    n = pl.num_programs(0)
    T = x_ref.shape[0]
    slot = i % 2

    def row_copy(sl, dref, t, k, j):
        return pltpu.make_async_copy(_slab(ys_ref, dref[j]), _slab(buf_ref.at[sl], (k * T + t) * SLAB_ROWS),
                                     sem.at[sl])

    @pl.when(i == 0)
    def _():
        _for_rows(T, lambda t, k, j: row_copy(slot, dcur_ref, t, k, j).start(priority=k % 2))

    @pl.when(i + 1 < n)
    def _():
        _for_rows(T, lambda t, k, j: row_copy(1 - slot, dnext_ref, t, k, j).start(priority=k % 2))

    _for_rows(T, lambda t, k, j: row_copy(slot, dcur_ref, t, k, j).wait())

    wt = wt_ref[...]
    wb = [jnp.broadcast_to(wt[:, k:k + 1], (T, 128)) for k in range(TOP_K)]
    ys = []
    for c in range(SLAB_ROWS):
        chunk = lambda k: buf_ref[slot, pl.ds(k * T * SLAB_ROWS + c, T, stride=SLAB_ROWS), :]
        yc = wb[0] * chunk(0)
        for k in range(1, TOP_K):
            yc = yc + wb[k] * chunk(k)
        ys.append(yc)
    y = jnp.concatenate(ys, axis=1)
    out_ref[...] = _layer_norm(DEEPNORM_ALPHA * x_ref[...] + y, g_ref[...], b_ref[...])


def _combine(x2, dest_flat, wt_tok, ys, g, b):
    N, D = x2.shape
    T = T_MOVE
    n = N // T
    vec = pl.BlockSpec((1, D), lambda i: (0, 0))
    smem = lambda f: pl.BlockSpec((TOP_K * T,), f, memory_space=pltpu.SMEM)
    return pl.pallas_call(
        _combine_kernel,
        grid=(n,),
        in_specs=[smem(lambda i: (i,)), smem(lambda i: (jnp.minimum(i + 1, n - 1),)),
                  pl.BlockSpec((T, D), lambda i: (i, 0)),
                  pl.BlockSpec((T, TOP_K), lambda i: (i, 0)),
                  pl.BlockSpec(memory_space=pl.ANY), vec, vec],
        out_specs=pl.BlockSpec((T, D), lambda i: (i, 0)),
        out_shape=jax.ShapeDtypeStruct((N, D), F32),
        scratch_shapes=[pltpu.VMEM((2, TOP_K * T * SLAB_ROWS, 128), F32), pltpu.SemaphoreType.DMA((2,))],
        compiler_params=_cparams("arbitrary"),
        name="moe_combine_ln3",
    )(dest_flat, dest_flat, x2, wt_tok, ys, g.reshape(1, D), b.reshape(1, D))


def _moe(x2, router_w, router_b, w_gate, b_gate, w_up, b_up, w_down, b_down, ln_g, ln_b):
    N, D = x2.shape
    idx, wt = _router(x2, router_w, router_b)
    rank, cnt = _ranks(idx)
    counts = cnt[:, 0]
    blocks = (counts + BM - 1) // BM
    blk_end = jnp.cumsum(blocks)
    blk_start = blk_end - blocks
    dest = _dests(idx, rank, (blk_start * BM).astype(jnp.int32))
    n_blocks = -(-N * TOP_K // BM) + N_EXPERTS
    bi = jnp.arange(n_blocks, dtype=jnp.int32)
    be_raw = jnp.sum(bi[:, None] >= blk_end[None, :], axis=1).astype(jnp.int32)
    last_e = jnp.max(jnp.where(counts > 0, jnp.arange(N_EXPERTS), 0)).astype(jnp.int32)
    block_e = jnp.minimum(be_raw, last_e)
    in_use = bi < blk_end[-1]
    block_valid = jnp.where(in_use, jnp.clip(counts[block_e] - (bi - blk_start[block_e]) * BM, 0, BM), 0)
    dest_flat = (dest * SLAB_ROWS).T.reshape(N * TOP_K)
    xs = _dispatch(x2, dest_flat, n_blocks * BM)
    f3 = lambda t: t.astype(F32)[:, None, :]
    ys = _experts(xs, block_e, block_valid.astype(jnp.int32),
                  w_gate.astype(BF16), w_up.astype(BF16), w_down.astype(BF16),
                  f3(b_gate), f3(b_up), f3(b_down))
    return _combine(x2, dest_flat, wt.T, ys, ln_g, ln_b)


def kernel(x, mem, w_in, ret_norm_g, w_out, ln1_g, ln1_b, mem_wq, mem_wk, mem_wv, mem_wo, ln2_g, ln2_b,
           router_w, router_b, w_gate, b_gate, w_up, b_up, w_down, b_down, ln3_g, ln3_b):
    B, S, D = x.shape
    assert [d for _, d in DILATED_PATTERNS] == [1, 4, 16] and all(w // d == BAND for w, d in DILATED_PATTERNS)
    for l in range(w_in.shape[0]):
        h, qkv4, qkv16 = _in_proj(x, w_in[l].astype(BF16))
        pats = [_band_attn(h, B, S, 1, N_IN // GROUP),
                _band_attn(qkv4, B, S // 4, 4, 3), _band_attn(qkv16, B, S // 16, 16, 3)]
        ret = _retention(h, ret_norm_g[l])
        x1 = _out_proj([p[0] for p in pats], [p[1] for p in pats], ret, x,
                       w_out[l].astype(BF16), ln1_g[l], ln1_b[l])
        kmem, vmem = _mem_kv(mem, mem_wk[l].astype(BF16), mem_wv[l].astype(BF16))
        x2 = _cross_attn(x1, kmem, vmem, mem_wq[l].astype(BF16), mem_wo[l].astype(BF16),
                         ln2_g[l], ln2_b[l])
        x = _moe(x2, router_w[l], router_b[l], w_gate[l], b_gate[l], w_up[l], b_up[l],
                 w_down[l], b_down[l], ln3_g[l], ln3_b[l]).reshape(B, S, D)
    return x
```

```python
import functools

import jax
import jax.numpy as jnp
from jax import lax
from jax.experimental import pallas as pl
from jax.experimental.pallas import tpu as pltpu

F32 = jnp.float32
BF16 = jnp.bfloat16

HEAD_DIM = 64
N_HEADS_DIL = 8
N_HEADS_RET = 8
W_DIL = N_HEADS_DIL * HEAD_DIM
W_RET = N_HEADS_RET * HEAD_DIM
N_IN = 3 * W_DIL + 4 * W_RET
DILATED_PATTERNS = ((128, 1), (512, 4), (2048, 16))
BAND = 128
ROPE_THETA = 500000.0
ROT_DIM = HEAD_DIM // 4
RET_THETA = 10000.0
RET_CHUNK = 128
N_HEADS_MEM = 4
N_EXPERTS = 32
TOP_K = 4
SWIGLU_LIMIT = 7.0
SWIGLU_ALPHA = 1.702
LN_EPS = 1e-5
DEPTH = 1
DEEPNORM_ALPHA = (2 * DEPTH) ** 0.25

GROUP = 512
TM = 512
TQ = 512
T_ROUTE = 1024
T_RANK = 512
T_MOVE = 512
BM = 512
VMEM_LIMIT = 56 * 1024 * 1024
NEG = -1e30


def _cparams(*sem):
    return pltpu.CompilerParams(dimension_semantics=sem, vmem_limit_bytes=VMEM_LIMIT)


def _nt_dot(a, b):
    return lax.dot_general(a, b, (((1,), (1,)), ((), ())), preferred_element_type=F32)


def _layer_norm(y, g, b):
    mu = jnp.mean(y, axis=-1, keepdims=True)
    d = y - mu
    var = jnp.mean(d * d, axis=-1, keepdims=True)
    return d * lax.rsqrt(var + LN_EPS) * g + b


def _in_proj_kernel(x_ref, w_ref, ca_ref, la_ref, ha_ref, cr_ref, lr_ref, hr_ref, o_ref, o4_ref, o16_ref,
                    acc_ref):
    xb = x_ref[...].astype(BF16)
    rep = GROUP // 128
    tm = x_ref.shape[0]
    qkv = 3 * W_DIL

    def rot(acc, c_ref, lo_ref, hi_ref, half):
        c = jnp.tile(c_ref[...], (1, rep))
        lo = jnp.tile(lo_ref[...], (1, rep))
        hi = jnp.tile(hi_ref[...], (1, rep))
        up = pltpu.roll(acc, GROUP - half, axis=1)
        dn = pltpu.roll(acc, half, axis=1)
        return acc * c + up * lo + dn * hi

    for g in range(N_IN // GROUP):
        acc = jnp.dot(xb, w_ref[:, g * GROUP:(g + 1) * GROUP], preferred_element_type=F32)
        if g == 0:
            acc = rot(acc, ca_ref, la_ref, ha_ref, ROT_DIM // 2) * (HEAD_DIM ** -0.5)
        elif g == 1:
            acc = rot(acc, ca_ref, la_ref, ha_ref, ROT_DIM // 2)
        elif g == 3:
            acc = rot(acc, cr_ref, lr_ref, hr_ref, HEAD_DIM // 2)
        elif g == 4:
            acc = rot(acc, cr_ref, lr_ref, hr_ref, HEAD_DIM // 2) * (HEAD_DIM ** -0.5)
        o_ref[:, g * GROUP:(g + 1) * GROUP] = acc.astype(BF16)
        if g < 3:
            for c in range(rep):
                acc_ref[c] = acc[:, c * 128:(c + 1) * 128]
            for d, od_ref in ((4, o4_ref), (16, o16_ref)):
                for r in range(d):
                    for c in range(rep):
                        c0 = r * qkv + g * GROUP + c * 128
                        od_ref[:, c0:c0 + 128] = acc_ref[c, pl.ds(r, tm // d, stride=d), :].astype(BF16)


def _rotary_tables(S, theta, rot_dim):
    half = rot_dim // 2
    inv = 1.0 / (theta ** (jnp.arange(half, dtype=F32) / half))
    ang = jnp.arange(S, dtype=F32)[:, None] * inv[None, :]
    cos, sin = jnp.cos(ang), jnp.sin(ang)
    pad = HEAD_DIM - rot_dim
    c = jnp.concatenate([cos, cos, jnp.ones((S, pad), F32)], axis=1)
    lo = jnp.concatenate([-sin, jnp.zeros((S, half + pad), F32)], axis=1)
    hi = jnp.concatenate([jnp.zeros((S, half), F32), sin, jnp.zeros((S, pad), F32)], axis=1)
    two = lambda t: jnp.concatenate([t, t], axis=1)
    return two(c), two(lo), two(hi)


def _in_proj(x3, w_bf):
    B, S, D = x3.shape
    tabs = _rotary_tables(S, ROPE_THETA, ROT_DIM) + _rotary_tables(S, RET_THETA, HEAD_DIM)
    tab_spec = pl.BlockSpec((TM, 128), lambda b, i: (i, 0))
    qkv = 3 * W_DIL
    cls_spec = lambda d: pl.BlockSpec((None, TM // d, d * qkv), lambda b, i: (b, i, 0))
    cls_shape = lambda d: jax.ShapeDtypeStruct((B, S // d, d * qkv), BF16)
    return pl.pallas_call(
        _in_proj_kernel,
        grid=(B, S // TM),
        in_specs=[pl.BlockSpec((None, TM, D), lambda b, i: (b, i, 0)),
                  pl.BlockSpec((D, N_IN), lambda b, i: (0, 0))] + [tab_spec] * 6,
        out_specs=[pl.BlockSpec((None, TM, N_IN), lambda b, i: (b, i, 0)), cls_spec(4), cls_spec(16)],
        out_shape=[jax.ShapeDtypeStruct((B, S, N_IN), BF16), cls_shape(4), cls_shape(16)],
        scratch_shapes=[pltpu.VMEM((GROUP // 128, TM, 128), F32)],
        compiler_params=_cparams("parallel", "parallel"),
        name="in_proj",
    )(x3, w_bf, *tabs)


def _band_attn_kernel(q_ref, kc_ref, kp_ref, vc_ref, vp_ref, o_ref, l_ref,
                      k_all, v_all, s_scr, p_scr, r_scr):
    j = pl.program_id(2)
    tq = q_ref.shape[0]
    nsub = tq // BAND
    pair = 2 * HEAD_DIM
    n_pairs = N_HEADS_DIL // 2
    k_all[0:BAND, :] = kp_ref[...]
    k_all[BAND:, :] = kc_ref[...]
    v_all[0:BAND, :] = vp_ref[...]
    v_all[BAND:, :] = vc_ref[...]
    qi = lax.broadcasted_iota(jnp.int32, (BAND, 2 * BAND), 0)
    kj = lax.broadcasted_iota(jnp.int32, (BAND, 2 * BAND), 1)
    dist = qi + BAND - kj
    band = (dist >= 0) & (dist <= BAND)
    bias = jnp.where(band, 0.0, NEG)
    bias_first = jnp.where(band & ((kj >= BAND) | (j > 0)), 0.0, NEG)
    low = lax.broadcasted_iota(jnp.int32, (BAND, pair), 1) < HEAD_DIM
    zero = jnp.zeros((BAND, pair), BF16)

    for pr in range(n_pairs):
        cs = slice(pr * pair, (pr + 1) * pair)
        for n in range(nsub):
            q = q_ref[n * BAND:(n + 1) * BAND, cs]
            kk = k_all[n * BAND:(n + 2) * BAND, cs]
            b = bias_first if n == 0 else bias
            i = pr * nsub + n
            s_scr[2 * i] = _nt_dot(jnp.where(low, q, zero), kk) + b
            s_scr[2 * i + 1] = _nt_dot(jnp.where(low, zero, q), kk) + b

    for pr in range(n_pairs):
        cs = slice(pr * pair, (pr + 1) * pair)
        for n in range(nsub):
            i = pr * nsub + n
            stats = []
            for a in range(2):
                s = s_scr[2 * i + a]
                m = jnp.max(s, axis=-1, keepdims=True)
                p = jnp.exp(s - m)
                l = jnp.sum(p, axis=-1, keepdims=True)
                p_scr[2 * i + a] = p.astype(BF16)
                stats.append((m, l))
            (m0, l0), (m1, l1) = stats
            r_scr[i] = jnp.where(low, 1.0 / l0, 1.0 / l1)
            l_ref[n * BAND:(n + 1) * BAND, cs] = jnp.where(low, m0 + jnp.log(l0), m1 + jnp.log(l1))

    for pr in range(n_pairs):
        cs = slice(pr * pair, (pr + 1) * pair)
        for n in range(nsub):
            i = pr * nsub + n
            vv = v_all[n * BAND:(n + 2) * BAND, cs]
            o0 = jnp.dot(p_scr[2 * i], vv, preferred_element_type=F32)
            o1 = jnp.dot(p_scr[2 * i + 1], vv, preferred_element_type=F32)
            o_ref[n * BAND:(n + 1) * BAND, cs] = jnp.where(low, o0, o1) * r_scr[i]


def _band_attn(src, B, L, d, gpr):
    tq = min(TQ, L)
    assert L % tq == 0
    sub = tq // BAND
    cur = lambda g: pl.BlockSpec((None, tq, GROUP), lambda b, r, j: (b, j, r * gpr + g))
    prev = lambda g: pl.BlockSpec((None, BAND, GROUP),
                                  lambda b, r, j: (b, jnp.maximum(j * sub - 1, 0), r * gpr + g))
    out_spec = pl.BlockSpec((None, tq, W_DIL), lambda b, r, j: (b, j, r))
    shp = jax.ShapeDtypeStruct((B, L, d * W_DIL), F32)
    n_blk = (N_HEADS_DIL // 2) * sub
    return pl.pallas_call(
        _band_attn_kernel,
        grid=(B, d, L // tq),
        in_specs=[cur(0), cur(1), prev(1), cur(2), prev(2)],
        out_specs=[out_spec, out_spec],
        out_shape=[shp, shp],
        scratch_shapes=[pltpu.VMEM((BAND + tq, GROUP), BF16), pltpu.VMEM((BAND + tq, GROUP), BF16),
                        pltpu.VMEM((2 * n_blk, BAND, 2 * BAND), F32),
                        pltpu.VMEM((2 * n_blk, BAND, 2 * BAND), BF16),
                        pltpu.VMEM((n_blk, BAND, 2 * HEAD_DIM), F32)],
        compiler_params=_cparams("parallel", "parallel", "parallel"),
        name=f"band_attn_d{d}",
    )(src, src, src, src, src)


RET_UNROLL = 4


def _retention_kernel(q_ref, k_ref, v_ref, g_ref, gain_ref, din_ref, kd_ref, qd_ref, cd_ref,
                      o_ref, kv_ref, st_ref):
    C = RET_CHUNK
    n_iter = q_ref.shape[0] // (C * RET_UNROLL)
    pair = 2 * HEAD_DIM
    low = lax.broadcasted_iota(jnp.int32, (C, pair), 1) < HEAD_DIM

    def rows(i, u):
        return pl.ds(pl.multiple_of(i * (C * RET_UNROLL), C * RET_UNROLL) + u * C, C)

    def kv_body(i, carry):
        for u in range(RET_UNROLL):
            kd = (k_ref[rows(i, u), :].astype(F32) * kd_ref[...]).astype(BF16)
            kv_ref[i * RET_UNROLL + u] = lax.dot_general(kd, v_ref[rows(i, u), :], (((0,), (0,)), ((), ())),
                                                         preferred_element_type=F32)
        return carry

    lax.fori_loop(0, n_iter, kv_body, 0)

    def scan_body(c, state):
        st_ref[c] = state.astype(BF16)
        return state * cd_ref[...] + kv_ref[c]

    lax.fori_loop(0, n_iter * RET_UNROLL, scan_body, jnp.zeros((pair, pair), F32))

    def out_body(i, carry):
        for u in range(RET_UNROLL):
            q = q_ref[rows(i, u), :]
            k = k_ref[rows(i, u), :]
            v = v_ref[rows(i, u), :]
            prev = st_ref[i * RET_UNROLL + u]
            ys = []
            for a in range(2):
                qa = jnp.where(low, q, jnp.zeros_like(q)) if a == 0 else jnp.where(low, jnp.zeros_like(q), q)
                s = _nt_dot(qa, k) * din_ref[a]
                inner = jnp.dot(s.astype(BF16), v, preferred_element_type=F32)
                qd = (qa.astype(F32) * qd_ref[...]).astype(BF16)
                ys.append(inner + jnp.dot(qd, prev, preferred_element_type=F32))
            y = jnp.where(low, ys[0], ys[1])
            inv = 1.0 / HEAD_DIM
            half_sum = lambda t: jnp.where(low, jnp.sum(jnp.where(low, t, 0.0), axis=-1, keepdims=True),
                                           jnp.sum(jnp.where(low, 0.0, t), axis=-1, keepdims=True))
            dlt = y - half_sum(y) * inv
            var = half_sum(dlt * dlt) * inv
            yn = dlt * lax.rsqrt(var + LN_EPS) * gain_ref[...]
            gr = g_ref[rows(i, u), :].astype(F32)
            o_ref[rows(i, u), :] = (yn * (gr / (1.0 + jnp.exp(-gr)))).astype(BF16)
        return carry

    lax.fori_loop(0, n_iter, out_body, 0)


def _retention_tables():
    H, C = N_HEADS_RET, RET_CHUNK
    log_g = jnp.log1p(-(2.0 ** (-5.0 - jnp.arange(H, dtype=F32))))
    idx = jnp.arange(C, dtype=F32)
    diff = idx[:, None] - idx[None, :]
    inner_decay = jnp.exp(log_g[:, None, None] * jnp.maximum(diff, 0.0)) * (diff >= 0)
    k_decay = jnp.exp(log_g[:, None] * (C - 1 - idx))
    q_decay = jnp.exp(log_g[:, None] * (idx + 1.0))
    chunk_decay = jnp.exp(log_g * C)
    lanes = lambda t: jnp.repeat(t.reshape(H // 2, 2, C).transpose(0, 2, 1), HEAD_DIM, axis=2)
    cd = jnp.broadcast_to(jnp.repeat(chunk_decay.reshape(H // 2, 2), HEAD_DIM, axis=1)[:, :, None],
                          (H // 2, 2 * HEAD_DIM, 2 * HEAD_DIM))
    return inner_decay.astype(F32), lanes(k_decay), lanes(q_decay), cd


def _retention(h, gain):
    B, S, _ = h.shape
    lanes = 2 * HEAD_DIM
    col0 = 3 * W_DIL // lanes
    per = W_RET // lanes
    col = lambda g: pl.BlockSpec((None, S, lanes), lambda b, p: (b, 0, col0 + g * per + p))
    din, kd, qd, cd = _retention_tables()
    C = RET_CHUNK
    assert S % (C * RET_UNROLL) == 0
    tab = lambda a: pl.BlockSpec((None, a, lanes), lambda b, p: (p, 0, 0))
    return pl.pallas_call(
        _retention_kernel,
        grid=(B, per),
        in_specs=[col(0), col(1), col(2), col(3),
                  pl.BlockSpec((1, lanes), lambda b, p: (0, p)),
                  pl.BlockSpec((2, C, C), lambda b, p: (p, 0, 0)), tab(C), tab(C), tab(lanes)],
        out_specs=pl.BlockSpec((None, S, lanes), lambda b, p: (b, 0, p)),
        out_shape=jax.ShapeDtypeStruct((B, S, W_RET), BF16),
        scratch_shapes=[pltpu.VMEM((S // C, lanes, lanes), F32), pltpu.VMEM((S // C, lanes, lanes), BF16)],
        compiler_params=_cparams("parallel", "parallel"),
        name="retention",
    )(h, h, h, h, gain.reshape(1, W_RET).astype(F32), din, kd, qd, cd)


def _out_proj_kernel(o1, o4, o16, l1, l4, l16, ret_ref, x_ref, w_ref, g_ref, b_ref, out_ref,
                     so4, so16, sl4, sl16):
    tm = x_ref.shape[0]
    n_slab = W_DIL // 128
    for d, pairs in ((4, ((o4, so4), (l4, sl4))), (16, ((o16, so16), (l16, sl16)))):
        for src, dst in pairs:
            for r in range(d):
                for c in range(n_slab):
                    c0 = r * W_DIL + c * 128
                    dst[c, pl.ds(r, tm // d, stride=d), :] = src[:, c0:c0 + 128]
    atts = []
    for c in range(n_slab):
        cs = slice(c * 128, (c + 1) * 128)
        a1, a4, a16 = l1[:, cs], sl4[c], sl16[c]
        m = jnp.maximum(jnp.maximum(a1, a4), a16)
        e1, e4, e16 = jnp.exp(a1 - m), jnp.exp(a4 - m), jnp.exp(a16 - m)
        att = (e1 * o1[:, cs] + e4 * so4[c] + e16 * so16[c]) / (e1 + e4 + e16)
        atts.append(att.astype(BF16))
    acc = jnp.dot(jnp.concatenate(atts, axis=1), w_ref[:W_DIL, :], preferred_element_type=F32)
    acc += jnp.dot(ret_ref[...], w_ref[W_DIL:, :], preferred_element_type=F32)
    out_ref[...] = _layer_norm(DEEPNORM_ALPHA * x_ref[...] + acc, g_ref[...], b_ref[...])


def _out_proj(os_, ls_, ret, x3, w_bf, g, b):
    B, S, D = x3.shape
    cls = lambda d: pl.BlockSpec((None, TM // d, d * W_DIL), lambda b_, i: (b_, i, 0))
    half = cls(1)
    full = pl.BlockSpec((None, TM, D), lambda b_, i: (b_, i, 0))
    vec = pl.BlockSpec((1, D), lambda b_, i: (0, 0))
    return pl.pallas_call(
        _out_proj_kernel,
        grid=(B, S // TM),
        in_specs=[half, cls(4), cls(16), half, cls(4), cls(16), half, full,
                  pl.BlockSpec((D, D), lambda b_, i: (0, 0)), vec, vec],
        out_specs=full,
        out_shape=jax.ShapeDtypeStruct((B, S, D), F32),
        scratch_shapes=[pltpu.VMEM((W_DIL // 128, TM, 128), F32)] * 4,
        compiler_params=_cparams("parallel", "parallel"),
        name="out_proj_ln1",
    )(*os_, *ls_, ret, x3, w_bf, g.reshape(1, D), b.reshape(1, D))


def _mem_kv_kernel(m_ref, wk_ref, wv_ref, k_ref, v_ref):
    mb = m_ref[...].astype(BF16)
    k_ref[...] = jnp.dot(mb, wk_ref[...], preferred_element_type=F32).astype(BF16)
    v_ref[...] = jnp.dot(mb, wv_ref[...], preferred_element_type=F32).astype(BF16)


def _mem_kv(mem, wk_bf, wv_bf):
    B, M, D = mem.shape
    blk = pl.BlockSpec((None, M, D), lambda b: (b, 0, 0))
    wsp = pl.BlockSpec((D, D), lambda b: (0, 0))
    shp = jax.ShapeDtypeStruct((B, M, D), BF16)
    return pl.pallas_call(
        _mem_kv_kernel, grid=(B,), in_specs=[blk, wsp, wsp], out_specs=[blk, blk], out_shape=[shp, shp],
        compiler_params=_cparams("parallel"), name="mem_kv",
    )(mem, wk_bf, wv_bf)


def _cross_attn_kernel(x_ref, k_ref, v_ref, wq_ref, wo_ref, g_ref, b_ref, out_ref):
    x = x_ref[...]
    hd = x.shape[-1] // N_HEADS_MEM
    q = (jnp.dot(x.astype(BF16), wq_ref[...], preferred_element_type=F32) * (hd ** -0.5)).astype(BF16)
    outs = []
    for hh in range(N_HEADS_MEM):
        cs = slice(hh * hd, (hh + 1) * hd)
        s = _nt_dot(q[:, cs], k_ref[:, cs])
        m = jnp.max(s, axis=-1, keepdims=True)
        p = jnp.exp(s - m)
        l = jnp.sum(p, axis=-1, keepdims=True)
        outs.append((jnp.dot(p.astype(BF16), v_ref[:, cs], preferred_element_type=F32) / l).astype(BF16))
    o = jnp.concatenate(outs, axis=-1)
    c = jnp.dot(o, wo_ref[...], preferred_element_type=F32)
    out_ref[...] = _layer_norm(DEEPNORM_ALPHA * x + c, g_ref[...], b_ref[...])


def _cross_attn(x3, kmem, vmem, wq_bf, wo_bf, g, b):
    B, S, D = x3.shape
    M = kmem.shape[1]
    xs = pl.BlockSpec((None, TM, D), lambda b_, i: (b_, i, 0))
    ms = pl.BlockSpec((None, M, D), lambda b_, i: (b_, 0, 0))
    ws = pl.BlockSpec((D, D), lambda b_, i: (0, 0))
    vs = pl.BlockSpec((1, D), lambda b_, i: (0, 0))
    return pl.pallas_call(
        _cross_attn_kernel,
        grid=(B, S // TM),
        in_specs=[xs, ms, ms, ws, ws, vs, vs],
        out_specs=xs,
        out_shape=jax.ShapeDtypeStruct((B, S, D), F32),
        compiler_params=_cparams("parallel", "parallel"),
        name="cross_attn_ln2",
    )(x3, kmem, vmem, wq_bf, wo_bf, g.reshape(1, D), b.reshape(1, D)).reshape(B * S, D)


def _split_bf16(t):
    hi = t.astype(BF16)
    return hi, (t - hi.astype(F32)).astype(BF16)


def _router_kernel(x_ref, w_ref, b_ref, idx_ref, wt_ref):
    x_hi, x_lo = _split_bf16(x_ref[...])
    by_hi = _nt_dot(w_ref[...], x_hi)
    logits = (by_hi[:N_EXPERTS] + (by_hi[N_EXPERTS:] + _nt_dot(w_ref[:N_EXPERTS, :], x_lo))) + b_ref[:, :1]
    e_iota = lax.broadcasted_iota(jnp.int32, logits.shape, 0)
    cur = logits
    vals, idxs = [], []
    for _ in range(TOP_K):
        m = jnp.max(cur, axis=0, keepdims=True)
        idx = jnp.min(jnp.where(cur == m, e_iota, N_EXPERTS), axis=0, keepdims=True)
        cur = jnp.where(e_iota == idx, -jnp.inf, cur)
        vals.append(m)
        idxs.append(idx)
    es = [jnp.exp(v - vals[0]) for v in vals]
    tot = es[0] + es[1] + es[2] + es[3]
    idx_ref[...] = jnp.concatenate(idxs, axis=0)
    wt_ref[...] = jnp.concatenate([e / tot for e in es], axis=0)


def _router(x2, router_w, router_b):
    N, D = x2.shape
    out = pl.BlockSpec((TOP_K, T_ROUTE), lambda i: (0, i))
    return pl.pallas_call(
        _router_kernel,
        grid=(N // T_ROUTE,),
        in_specs=[pl.BlockSpec((T_ROUTE, D), lambda i: (i, 0)),
                  pl.BlockSpec((2 * N_EXPERTS, D), lambda i: (0, 0)),
                  pl.BlockSpec((N_EXPERTS, 128), lambda i: (0, 0))],
        out_specs=[out, out],
        out_shape=[jax.ShapeDtypeStruct((TOP_K, N), jnp.int32), jax.ShapeDtypeStruct((TOP_K, N), F32)],
        compiler_params=_cparams("parallel"),
        name="router_top4",
    )(x2, jnp.concatenate(_split_bf16(router_w.T.astype(F32)), axis=0),
      jnp.broadcast_to(router_b.astype(F32)[:, None], (N_EXPERTS, 128)))


def _rank_kernel(idx_ref, tri_ref, rank_ref, cnt_ref, carry_ref):
    @pl.when(pl.program_id(0) == 0)
    def _():
        carry_ref[...] = jnp.zeros_like(carry_ref)

    T = idx_ref.shape[1]
    e_iota = lax.broadcasted_iota(jnp.int32, (N_EXPERTS, T), 0)
    hot = [e_iota == idx_ref[k:k + 1, :] for k in range(TOP_K)]
    c = sum(h.astype(F32) for h in hot)
    before = jnp.dot(c.astype(BF16), tri_ref[...], preferred_element_type=F32) + carry_ref[:, :1]
    rank_ref[...] = jnp.concatenate(
        [jnp.sum(jnp.where(h, before, 0.0), axis=0, keepdims=True) for h in hot], axis=0).astype(jnp.int32)
    carry_ref[...] = carry_ref[...] + jnp.sum(c, axis=1, keepdims=True)
    cnt_ref[...] = carry_ref[...].astype(jnp.int32)


def _ranks(idx):
    N = idx.shape[1]
    T = T_RANK
    tri = (jnp.arange(T)[:, None] < jnp.arange(T)[None, :]).astype(BF16)
    blk = pl.BlockSpec((TOP_K, T), lambda i: (0, i))
    return pl.pallas_call(
        _rank_kernel,
        grid=(N // T,),
        in_specs=[blk, pl.BlockSpec((T, T), lambda i: (0, 0))],
        out_specs=[blk, pl.BlockSpec((N_EXPERTS, 128), lambda i: (0, 0))],
        out_shape=[jax.ShapeDtypeStruct((TOP_K, N), jnp.int32),
                   jax.ShapeDtypeStruct((N_EXPERTS, 128), jnp.int32)],
        scratch_shapes=[pltpu.VMEM((N_EXPERTS, 128), F32)],
        compiler_params=_cparams("arbitrary"),
        name="expert_ranks",
    )(idx, tri)


def _dest_kernel(idx_ref, rank_ref, start_ref, dest_ref):
    T = idx_ref.shape[1]
    e_iota = lax.broadcasted_iota(jnp.int32, (N_EXPERTS, T), 0)
    start = jnp.tile(start_ref[...], (1, T // 128))
    rows = [jnp.sum(jnp.where(e_iota == idx_ref[k:k + 1, :], start, 0), axis=0, keepdims=True)
            for k in range(TOP_K)]
    dest_ref[...] = jnp.concatenate(rows, axis=0) + rank_ref[...]


def _dests(idx, rank, group_start):
    N = idx.shape[1]
    T = T_ROUTE
    blk = pl.BlockSpec((TOP_K, T), lambda i: (0, i))
    return pl.pallas_call(
        _dest_kernel,
        grid=(N // T,),
        in_specs=[blk, blk, pl.BlockSpec((N_EXPERTS, 128), lambda i: (0, 0))],
        out_specs=blk,
        out_shape=jax.ShapeDtypeStruct((TOP_K, N), jnp.int32),
        compiler_params=_cparams("parallel"),
        name="expert_dests",
    )(idx, rank, jnp.broadcast_to(group_start[:, None], (N_EXPERTS, 128)))


MOVE_UNROLL = 8
SLAB_ROWS = 8


def _for_rows(T, fn):
    def body(g, carry):
        for u in range(MOVE_UNROLL):
            for k in range(TOP_K):
                t = g * MOVE_UNROLL + u
                fn(t, k, g * (MOVE_UNROLL * TOP_K) + (u * TOP_K + k))
        return carry

    lax.fori_loop(0, T // MOVE_UNROLL, body, 0)


def _slab(ref_at, row8):
    return ref_at.at[pl.ds(pl.multiple_of(row8, SLAB_ROWS), SLAB_ROWS), :]


def _dispatch_kernel(dcur_ref, dprev_ref, x_ref, *refs, n_cast):
    w_refs, xs_ref, wb_refs, (pk_ref, sem) = refs[:n_cast], refs[n_cast], refs[n_cast + 1:-2], refs[-2:]
    for w_ref, wb_ref in zip(w_refs, wb_refs):
        wb_ref[...] = w_ref[...].astype(BF16)
    i = pl.program_id(0)
    n = pl.num_programs(0)
    T = x_ref.shape[0]
    slot = i % 2
    for c in range(SLAB_ROWS):
        pk_ref[slot, pl.ds(c, T, stride=SLAB_ROWS), :] = x_ref[:, c * 128:(c + 1) * 128]

    def row_copy(sl, dref, t, j):
        return pltpu.make_async_copy(_slab(pk_ref.at[sl], t * SLAB_ROWS), _slab(xs_ref, dref[j]), sem.at[sl])

    _for_rows(T, lambda t, k, j: row_copy(slot, dcur_ref, t, j).start(priority=k % 2))

    @pl.when(i > 0)
    def _():
        _for_rows(T, lambda t, k, j: row_copy(1 - slot, dprev_ref, t, j).wait())

    @pl.when(i == n - 1)
    def _():
        _for_rows(T, lambda t, k, j: row_copy(slot, dcur_ref, t, j).wait())


CAST_BLOCK_BYTES = 2 * 1024 * 1024


def _dispatch(x2, dest8, P, weights):
    N, D = x2.shape
    assert D == SLAB_ROWS * 128
    T = T_MOVE
    n = N // T
    flat = [w.reshape(-1, w.shape[-1]) for w in weights]
    fused = all(f.shape[0] % n == 0 and (f.shape[0] // n) % 16 == 0
                and (f.shape[0] // n) * f.shape[1] * 4 <= CAST_BLOCK_BYTES for f in flat)
    if not fused:
        flat = []
    w_specs = [pl.BlockSpec((f.shape[0] // n, f.shape[1]), lambda i: (i, 0)) for f in flat]
    smem = lambda f: pl.BlockSpec((TOP_K * T,), f, memory_space=pltpu.SMEM)
    outs = pl.pallas_call(
        functools.partial(_dispatch_kernel, n_cast=len(flat)),
        grid=(n,),
        in_specs=[smem(lambda i: (i,)), smem(lambda i: (jnp.maximum(i - 1, 0),)),
                  pl.BlockSpec((T, D), lambda i: (i, 0))] + w_specs,
        out_specs=[pl.BlockSpec(memory_space=pl.ANY)] + w_specs,
        out_shape=[jax.ShapeDtypeStruct((P * SLAB_ROWS, 128), F32)]
                  + [jax.ShapeDtypeStruct(f.shape, BF16) for f in flat],
        scratch_shapes=[pltpu.VMEM((2, T * SLAB_ROWS, 128), F32), pltpu.SemaphoreType.DMA((2,))],
        compiler_params=_cparams("arbitrary"),
        name="moe_dispatch",
    )(dest8, dest8, x2, *flat)
    if fused:
        return outs[0], [o.reshape(w.shape) for o, w in zip(outs[1:], weights)]
    return outs[0], [w.astype(BF16) for w in weights]


def _expert_kernel(be_ref, bv_ref, xs_ref, wg_ref, wu_ref, wd_ref, bg_ref, bu_ref, bd_ref, ys_ref):
    j = pl.program_id(0)
    valid = bv_ref[j]

    @pl.when(valid > 0)
    def _():
        bm = xs_ref.shape[0] // SLAB_ROWS
        row = lax.broadcasted_iota(jnp.int32, (bm, 128), 0)
        xb = jnp.concatenate(
            [jnp.where(row < valid, xs_ref[pl.ds(c, bm, stride=SLAB_ROWS), :], 0.0).astype(BF16)
             for c in range(SLAB_ROWS)], axis=1)

        def proj(w_ref, b_ref):
            return jnp.dot(xb, w_ref[...], preferred_element_type=F32) + b_ref[...]

        gate = jnp.minimum(proj(wg_ref, bg_ref), SWIGLU_LIMIT)
        up = jnp.clip(proj(wu_ref, bu_ref), -SWIGLU_LIMIT, SWIGLU_LIMIT)
        hmid = gate * (1.0 / (1.0 + jnp.exp(-SWIGLU_ALPHA * gate))) * (up + 1.0)
        out = jnp.dot(hmid.astype(BF16), wd_ref[...], preferred_element_type=F32) + bd_ref[...]
        for c in range(SLAB_ROWS):
            ys_ref[pl.ds(c, bm, stride=SLAB_ROWS), :] = out[:, c * 128:(c + 1) * 128]


def _experts(xs, block_e, block_valid, wg, wu, wd, bg, bu, bd):
    D = SLAB_ROWS * 128
    F = wg.shape[-1]
    wspec = lambda a, c: pl.BlockSpec((None, a, c), lambda j, be, bv: (be[j], 0, 0))
    slabs = pl.BlockSpec((BM * SLAB_ROWS, 128), lambda j, be, bv: (j, 0))
    grid_spec = pltpu.PrefetchScalarGridSpec(
        num_scalar_prefetch=2,
        grid=(xs.shape[0] // (BM * SLAB_ROWS),),
        in_specs=[slabs, wspec(D, F), wspec(D, F), wspec(F, D), wspec(1, F), wspec(1, F), wspec(1, D)],
        out_specs=slabs,
    )
    return pl.pallas_call(
        _expert_kernel,
        grid_spec=grid_spec,
        out_shape=jax.ShapeDtypeStruct(xs.shape, F32),
        compiler_params=_cparams("arbitrary"),
        name="moe_experts",
    )(block_e, block_valid, xs, wg, wu, wd, bg, bu, bd)


def _combine_kernel(dcur_ref, dnext_ref, x_ref, wt_ref, ys_ref, g_ref, b_ref, out_ref, buf_ref, sem):
    i = pl.program_id(0)
    n = pl.num_programs(0)
    T = x_ref.shape[0]
    slot = i % 2

    def row_copy(sl, dref, t, k, j):
        return pltpu.make_async_copy(_slab(ys_ref, dref[j]), _slab(buf_ref.at[sl], (k * T + t) * SLAB_ROWS),
                                     sem.at[sl])

    @pl.when(i == 0)
    def _():
        _for_rows(T, lambda t, k, j: row_copy(slot, dcur_ref, t, k, j).start(priority=k % 2))

    @pl.when(i + 1 < n)
    def _():
        _for_rows(T, lambda t, k, j: row_copy(1 - slot, dnext_ref, t, k, j).start(priority=k % 2))

    _for_rows(T, lambda t, k, j: row_copy(slot, dcur_ref, t, k, j).wait())

    wt = wt_ref[...]
    wb = [jnp.broadcast_to(wt[:, k:k + 1], (T, 128)) for k in range(TOP_K)]
    ys = []
    for c in range(SLAB_ROWS):
        chunk = lambda k: buf_ref[slot, pl.ds(k * T * SLAB_ROWS + c, T, stride=SLAB_ROWS), :]
        yc = wb[0] * chunk(0)
        for k in range(1, TOP_K):
            yc = yc + wb[k] * chunk(k)
        ys.append(yc)
    y = jnp.concatenate(ys, axis=1)
    out_ref[...] = _layer_norm(DEEPNORM_ALPHA * x_ref[...] + y, g_ref[...], b_ref[...])


def _combine(x2, dest_flat, wt_tok, ys, g, b):
    N, D = x2.shape
    T = T_MOVE
    n = N // T
    vec = pl.BlockSpec((1, D), lambda i: (0, 0))
    smem = lambda f: pl.BlockSpec((TOP_K * T,), f, memory_space=pltpu.SMEM)
    return pl.pallas_call(
        _combine_kernel,
        grid=(n,),
        in_specs=[smem(lambda i: (i,)), smem(lambda i: (jnp.minimum(i + 1, n - 1),)),
                  pl.BlockSpec((T, D), lambda i: (i, 0)),
                  pl.BlockSpec((T, TOP_K), lambda i: (i, 0)),
                  pl.BlockSpec(memory_space=pl.ANY), vec, vec],
        out_specs=pl.BlockSpec((T, D), lambda i: (i, 0)),
        out_shape=jax.ShapeDtypeStruct((N, D), F32),
        scratch_shapes=[pltpu.VMEM((2, TOP_K * T * SLAB_ROWS, 128), F32), pltpu.SemaphoreType.DMA((2,))],
        compiler_params=_cparams("arbitrary"),
        name="moe_combine_ln3",
    )(dest_flat, dest_flat, x2, wt_tok, ys, g.reshape(1, D), b.reshape(1, D))


def _moe(x2, router_w, router_b, w_gate, b_gate, w_up, b_up, w_down, b_down, ln_g, ln_b):
    N, D = x2.shape
    idx, wt = _router(x2, router_w, router_b)
    rank, cnt = _ranks(idx)
    counts = cnt[:, 0]
    blocks = (counts + BM - 1) // BM
    blk_end = jnp.cumsum(blocks)
    blk_start = blk_end - blocks
    dest = _dests(idx, rank, (blk_start * BM).astype(jnp.int32))
    n_blocks = -(-N * TOP_K // BM) + N_EXPERTS
    bi = jnp.arange(n_blocks, dtype=jnp.int32)
    be_raw = jnp.sum(bi[:, None] >= blk_end[None, :], axis=1).astype(jnp.int32)
    last_e = jnp.max(jnp.where(counts > 0, jnp.arange(N_EXPERTS), 0)).astype(jnp.int32)
    block_e = jnp.minimum(be_raw, last_e)
    in_use = bi < blk_end[-1]
    block_valid = jnp.where(in_use, jnp.clip(counts[block_e] - (bi - blk_start[block_e]) * BM, 0, BM), 0)
    dest_flat = (dest * SLAB_ROWS).T.reshape(N * TOP_K)
    xs, (wg_bf, wu_bf, wd_bf) = _dispatch(x2, dest_flat, n_blocks * BM, [w_gate, w_up, w_down])
    f3 = lambda t: t.astype(F32)[:, None, :]
    ys = _experts(xs, block_e, block_valid.astype(jnp.int32), wg_bf, wu_bf, wd_bf,
                  f3(b_gate), f3(b_up), f3(b_down))
    return _combine(x2, dest_flat, wt.T, ys, ln_g, ln_b)


def kernel(x, mem, w_in, ret_norm_g, w_out, ln1_g, ln1_b, mem_wq, mem_wk, mem_wv, mem_wo, ln2_g, ln2_b,
           router_w, router_b, w_gate, b_gate, w_up, b_up, w_down, b_down, ln3_g, ln3_b):
    B, S, D = x.shape
    assert [d for _, d in DILATED_PATTERNS] == [1, 4, 16] and all(w // d == BAND for w, d in DILATED_PATTERNS)
    for l in range(w_in.shape[0]):
        h, qkv4, qkv16 = _in_proj(x, w_in[l].astype(BF16))
        pats = [_band_attn(h, B, S, 1, N_IN // GROUP),
                _band_attn(qkv4, B, S // 4, 4, 3), _band_attn(qkv16, B, S // 16, 16, 3)]
        ret = _retention(h, ret_norm_g[l])
        x1 = _out_proj([p[0] for p in pats], [p[1] for p in pats], ret, x,
                       w_out[l].astype(BF16), ln1_g[l], ln1_b[l])
        kmem, vmem = _mem_kv(mem, mem_wk[l].astype(BF16), mem_wv[l].astype(BF16))
        x2 = _cross_attn(x1, kmem, vmem, mem_wq[l].astype(BF16), mem_wo[l].astype(BF16),
                         ln2_g[l], ln2_b[l])
        x = _moe(x2, router_w[l], router_b[l], w_gate[l], b_gate[l], w_up[l], b_up[l],
                 w_down[l], b_down[l], ln3_g[l], ln3_b[l]).reshape(B, S, D)
    return x
```

```python
import functools

import jax
import jax.numpy as jnp
import numpy as np
from jax import lax
from jax.experimental import pallas as pl
from jax.experimental.pallas import tpu as pltpu

F32 = jnp.float32
BF16 = jnp.bfloat16

HEAD_DIM = 64
N_HEADS_DIL = 8
N_HEADS_RET = 8
W_DIL = N_HEADS_DIL * HEAD_DIM
W_RET = N_HEADS_RET * HEAD_DIM
N_IN = 3 * W_DIL + 4 * W_RET
DILATED_PATTERNS = ((128, 1), (512, 4), (2048, 16))
BAND = 128
ROPE_THETA = 500000.0
ROT_DIM = HEAD_DIM // 4
RET_THETA = 10000.0
RET_CHUNK = 128
N_HEADS_MEM = 4
N_EXPERTS = 32
TOP_K = 4
SWIGLU_LIMIT = 7.0
SWIGLU_ALPHA = 1.702
LN_EPS = 1e-5
DEPTH = 1
DEEPNORM_ALPHA = (2 * DEPTH) ** 0.25

GROUP = 512
TM = 512
TQ = 512
T_ROUTE = 1024
T_RANK = 512
T_MOVE = 512
T_COMBINE = 256
BM = 512
VMEM_LIMIT = 56 * 1024 * 1024
NEG = -1e30


def _cparams(*sem):
    return pltpu.CompilerParams(dimension_semantics=sem, vmem_limit_bytes=VMEM_LIMIT)


def _nt_dot(a, b):
    return lax.dot_general(a, b, (((1,), (1,)), ((), ())), preferred_element_type=F32)


def _layer_norm(y, g, b):
    mu = jnp.mean(y, axis=-1, keepdims=True)
    d = y - mu
    var = jnp.mean(d * d, axis=-1, keepdims=True)
    return d * lax.rsqrt(var + LN_EPS) * g + b


def _in_proj_kernel(x_ref, w_ref, ca_ref, la_ref, ha_ref, cr_ref, lr_ref, hr_ref, o_ref, o4_ref, o16_ref,
                    acc_ref):
    xb = x_ref[...].astype(BF16)
    rep = GROUP // 128
    tm = x_ref.shape[0]
    qkv = 3 * W_DIL

    def rot(acc, c_ref, lo_ref, hi_ref, half):
        c = jnp.tile(c_ref[...], (1, rep))
        lo = jnp.tile(lo_ref[...], (1, rep))
        hi = jnp.tile(hi_ref[...], (1, rep))
        up = pltpu.roll(acc, GROUP - half, axis=1)
        dn = pltpu.roll(acc, half, axis=1)
        return acc * c + up * lo + dn * hi

    for g in range(N_IN // GROUP):
        acc = jnp.dot(xb, w_ref[:, g * GROUP:(g + 1) * GROUP], preferred_element_type=F32)
        if g == 0:
            acc = rot(acc, ca_ref, la_ref, ha_ref, ROT_DIM // 2) * (HEAD_DIM ** -0.5)
        elif g == 1:
            acc = rot(acc, ca_ref, la_ref, ha_ref, ROT_DIM // 2)
        elif g == 3:
            acc = rot(acc, cr_ref, lr_ref, hr_ref, HEAD_DIM // 2)
        elif g == 4:
            acc = rot(acc, cr_ref, lr_ref, hr_ref, HEAD_DIM // 2) * (HEAD_DIM ** -0.5)
        o_ref[:, g * GROUP:(g + 1) * GROUP] = acc.astype(BF16)
        if g < 3:
            for c in range(rep):
                acc_ref[c] = acc[:, c * 128:(c + 1) * 128]
            for d, od_ref in ((4, o4_ref), (16, o16_ref)):
                for r in range(d):
                    for c in range(rep):
                        c0 = r * qkv + g * GROUP + c * 128
                        od_ref[:, c0:c0 + 128] = acc_ref[c, pl.ds(r, tm // d, stride=d), :].astype(BF16)


def _rotary_tables(S, theta, rot_dim):
    half = rot_dim // 2
    f32 = np.float32
    inv = f32(1.0) / (f32(theta) ** (np.arange(half, dtype=f32) / f32(half)))
    ang = np.arange(S, dtype=f32)[:, None] * inv[None, :]
    cos, sin = np.cos(ang).astype(f32), np.sin(ang).astype(f32)
    pad = HEAD_DIM - rot_dim
    c = np.concatenate([cos, cos, np.ones((S, pad), f32)], axis=1)
    lo = np.concatenate([-sin, np.zeros((S, half + pad), f32)], axis=1)
    hi = np.concatenate([np.zeros((S, half), f32), sin, np.zeros((S, pad), f32)], axis=1)
    two = lambda t: np.concatenate([t, t], axis=1)
    return two(c), two(lo), two(hi)


def _in_proj(x3, w_bf):
    B, S, D = x3.shape
    tabs = _rotary_tables(S, ROPE_THETA, ROT_DIM) + _rotary_tables(S, RET_THETA, HEAD_DIM)
    tab_spec = pl.BlockSpec((TM, 128), lambda b, i: (i, 0))
    qkv = 3 * W_DIL
    cls_spec = lambda d: pl.BlockSpec((None, TM // d, d * qkv), lambda b, i: (b, i, 0))
    cls_shape = lambda d: jax.ShapeDtypeStruct((B, S // d, d * qkv), BF16)
    return pl.pallas_call(
        _in_proj_kernel,
        grid=(B, S // TM),
        in_specs=[pl.BlockSpec((None, TM, D), lambda b, i: (b, i, 0)),
                  pl.BlockSpec((D, N_IN), lambda b, i: (0, 0))] + [tab_spec] * 6,
        out_specs=[pl.BlockSpec((None, TM, N_IN), lambda b, i: (b, i, 0)), cls_spec(4), cls_spec(16)],
        out_shape=[jax.ShapeDtypeStruct((B, S, N_IN), BF16), cls_shape(4), cls_shape(16)],
        scratch_shapes=[pltpu.VMEM((GROUP // 128, TM, 128), F32)],
        compiler_params=_cparams("parallel", "parallel"),
        name="in_proj",
    )(x3, w_bf, *tabs)


def _band_attn_kernel(q_ref, kc_ref, kp_ref, vc_ref, vp_ref, o_ref, l_ref,
                      k_all, v_all, s_scr, p_scr, r_scr):
    j = pl.program_id(2)
    tq = q_ref.shape[0]
    nsub = tq // BAND
    pair = 2 * HEAD_DIM
    n_pairs = N_HEADS_DIL // 2
    k_all[0:BAND, :] = kp_ref[...]
    k_all[BAND:, :] = kc_ref[...]
    v_all[0:BAND, :] = vp_ref[...]
    v_all[BAND:, :] = vc_ref[...]
    qi = lax.broadcasted_iota(jnp.int32, (BAND, 2 * BAND), 0)
    kj = lax.broadcasted_iota(jnp.int32, (BAND, 2 * BAND), 1)
    dist = qi + BAND - kj
    band = (dist >= 0) & (dist <= BAND)
    bias = jnp.where(band, 0.0, NEG)
    bias_first = jnp.where(band & ((kj >= BAND) | (j > 0)), 0.0, NEG)
    low = lax.broadcasted_iota(jnp.int32, (BAND, pair), 1) < HEAD_DIM
    zero = jnp.zeros((BAND, pair), BF16)

    for pr in range(n_pairs):
        cs = slice(pr * pair, (pr + 1) * pair)
        for n in range(nsub):
            q = q_ref[n * BAND:(n + 1) * BAND, cs]
            kk = k_all[n * BAND:(n + 2) * BAND, cs]
            b = bias_first if n == 0 else bias
            i = pr * nsub + n
            s_scr[2 * i] = _nt_dot(jnp.where(low, q, zero), kk) + b
            s_scr[2 * i + 1] = _nt_dot(jnp.where(low, zero, q), kk) + b

    for pr in range(n_pairs):
        cs = slice(pr * pair, (pr + 1) * pair)
        for n in range(nsub):
            i = pr * nsub + n
            stats = []
            for a in range(2):
                s = s_scr[2 * i + a]
                m = jnp.max(s, axis=-1, keepdims=True)
                p = jnp.exp(s - m)
                l = jnp.sum(p, axis=-1, keepdims=True)
                p_scr[2 * i + a] = p.astype(BF16)
                stats.append((m, l))
            (m0, l0), (m1, l1) = stats
            r_scr[i] = jnp.where(low, 1.0 / l0, 1.0 / l1)
            l_ref[n * BAND:(n + 1) * BAND, cs] = jnp.where(low, m0 + jnp.log(l0), m1 + jnp.log(l1))

    for pr in range(n_pairs):
        cs = slice(pr * pair, (pr + 1) * pair)
        for n in range(nsub):
            i = pr * nsub + n
            vv = v_all[n * BAND:(n + 2) * BAND, cs]
            o0 = jnp.dot(p_scr[2 * i], vv, preferred_element_type=F32)
            o1 = jnp.dot(p_scr[2 * i + 1], vv, preferred_element_type=F32)
            o_ref[n * BAND:(n + 1) * BAND, cs] = jnp.where(low, o0, o1) * r_scr[i]


def _band_attn(src, B, L, d, gpr):
    tq = min(TQ, L)
    assert L % tq == 0
    sub = tq // BAND
    cur = lambda g: pl.BlockSpec((None, tq, GROUP), lambda b, r, j: (b, j, r * gpr + g))
    prev = lambda g: pl.BlockSpec((None, BAND, GROUP),
                                  lambda b, r, j: (b, jnp.maximum(j * sub - 1, 0), r * gpr + g))
    out_spec = pl.BlockSpec((None, tq, W_DIL), lambda b, r, j: (b, j, r))
    shp = jax.ShapeDtypeStruct((B, L, d * W_DIL), F32)
    n_blk = (N_HEADS_DIL // 2) * sub
    return pl.pallas_call(
        _band_attn_kernel,
        grid=(B, d, L // tq),
        in_specs=[cur(0), cur(1), prev(1), cur(2), prev(2)],
        out_specs=[out_spec, out_spec],
        out_shape=[shp, shp],
        scratch_shapes=[pltpu.VMEM((BAND + tq, GROUP), BF16), pltpu.VMEM((BAND + tq, GROUP), BF16),
                        pltpu.VMEM((2 * n_blk, BAND, 2 * BAND), F32),
                        pltpu.VMEM((2 * n_blk, BAND, 2 * BAND), BF16),
                        pltpu.VMEM((n_blk, BAND, 2 * HEAD_DIM), F32)],
        compiler_params=_cparams("parallel", "parallel", "parallel"),
        name=f"band_attn_d{d}",
    )(src, src, src, src, src)


RET_UNROLL = 4


def _retention_kernel(q_ref, k_ref, v_ref, g_ref, gain_ref, din_ref, kd_ref, qd_ref, cd_ref,
                      o_ref, kv_ref, st_ref):
    C = RET_CHUNK
    n_iter = q_ref.shape[0] // (C * RET_UNROLL)
    pair = 2 * HEAD_DIM
    low = lax.broadcasted_iota(jnp.int32, (C, pair), 1) < HEAD_DIM

    def rows(i, u):
        return pl.ds(pl.multiple_of(i * (C * RET_UNROLL), C * RET_UNROLL) + u * C, C)

    def kv_body(i, carry):
        for u in range(RET_UNROLL):
            kd = (k_ref[rows(i, u), :].astype(F32) * kd_ref[...]).astype(BF16)
            kv_ref[i * RET_UNROLL + u] = lax.dot_general(kd, v_ref[rows(i, u), :], (((0,), (0,)), ((), ())),
                                                         preferred_element_type=F32)
        return carry

    lax.fori_loop(0, n_iter, kv_body, 0)

    def scan_body(c, state):
        st_ref[c] = state.astype(BF16)
        return state * cd_ref[...] + kv_ref[c]

    lax.fori_loop(0, n_iter * RET_UNROLL, scan_body, jnp.zeros((pair, pair), F32))

    def out_body(i, carry):
        for u in range(RET_UNROLL):
            q = q_ref[rows(i, u), :]
            k = k_ref[rows(i, u), :]
            v = v_ref[rows(i, u), :]
            prev = st_ref[i * RET_UNROLL + u]
            ys = []
            for a in range(2):
                qa = jnp.where(low, q, jnp.zeros_like(q)) if a == 0 else jnp.where(low, jnp.zeros_like(q), q)
                s = _nt_dot(qa, k) * din_ref[a]
                inner = jnp.dot(s.astype(BF16), v, preferred_element_type=F32)
                qd = (qa.astype(F32) * qd_ref[...]).astype(BF16)
                ys.append(inner + jnp.dot(qd, prev, preferred_element_type=F32))
            y = jnp.where(low, ys[0], ys[1])
            inv = 1.0 / HEAD_DIM
            half_sum = lambda t: jnp.where(low, jnp.sum(jnp.where(low, t, 0.0), axis=-1, keepdims=True),
                                           jnp.sum(jnp.where(low, 0.0, t), axis=-1, keepdims=True))
            dlt = y - half_sum(y) * inv
            var = half_sum(dlt * dlt) * inv
            yn = dlt * lax.rsqrt(var + LN_EPS) * gain_ref[...]
            gr = g_ref[rows(i, u), :].astype(F32)
            o_ref[rows(i, u), :] = (yn * (gr / (1.0 + jnp.exp(-gr)))).astype(BF16)
        return carry

    lax.fori_loop(0, n_iter, out_body, 0)


def _retention_tables():
    H, C = N_HEADS_RET, RET_CHUNK
    f32 = np.float32
    log_g = np.log1p(-(f32(2.0) ** (f32(-5.0) - np.arange(H, dtype=f32)))).astype(f32)
    idx = np.arange(C, dtype=f32)
    diff = idx[:, None] - idx[None, :]
    inner_decay = (np.exp(log_g[:, None, None] * np.maximum(diff, f32(0.0))) * (diff >= 0)).astype(f32)
    k_decay = np.exp(log_g[:, None] * (f32(C - 1) - idx)).astype(f32)
    q_decay = np.exp(log_g[:, None] * (idx + f32(1.0))).astype(f32)
    chunk_decay = np.exp(log_g * f32(C)).astype(f32)
    lanes = lambda t: np.repeat(t.reshape(H // 2, 2, C).transpose(0, 2, 1), HEAD_DIM, axis=2)
    cd = np.ascontiguousarray(np.broadcast_to(
        np.repeat(chunk_decay.reshape(H // 2, 2), HEAD_DIM, axis=1)[:, :, None],
        (H // 2, 2 * HEAD_DIM, 2 * HEAD_DIM)))
    return inner_decay, lanes(k_decay), lanes(q_decay), cd


def _retention(h, gain):
    B, S, _ = h.shape
    lanes = 2 * HEAD_DIM
    col0 = 3 * W_DIL // lanes
    per = W_RET // lanes
    col = lambda g: pl.BlockSpec((None, S, lanes), lambda b, p: (b, 0, col0 + g * per + p))
    din, kd, qd, cd = _retention_tables()
    C = RET_CHUNK
    assert S % (C * RET_UNROLL) == 0
    tab = lambda a: pl.BlockSpec((None, a, lanes), lambda b, p: (p, 0, 0))
    return pl.pallas_call(
        _retention_kernel,
        grid=(B, per),
        in_specs=[col(0), col(1), col(2), col(3),
                  pl.BlockSpec((1, lanes), lambda b, p: (0, p)),
                  pl.BlockSpec((2, C, C), lambda b, p: (p, 0, 0)), tab(C), tab(C), tab(lanes)],
        out_specs=pl.BlockSpec((None, S, lanes), lambda b, p: (b, 0, p)),
        out_shape=jax.ShapeDtypeStruct((B, S, W_RET), BF16),
        scratch_shapes=[pltpu.VMEM((S // C, lanes, lanes), F32), pltpu.VMEM((S // C, lanes, lanes), BF16)],
        compiler_params=_cparams("parallel", "parallel"),
        name="retention",
    )(h, h, h, h, gain.reshape(1, W_RET).astype(F32), din, kd, qd, cd)


def _out_proj_kernel(o1, o4, o16, l1, l4, l16, ret_ref, x_ref, w_ref, g_ref, b_ref, out_ref,
                     so4, so16, sl4, sl16):
    tm = x_ref.shape[0]
    n_slab = W_DIL // 128
    for d, pairs in ((4, ((o4, so4), (l4, sl4))), (16, ((o16, so16), (l16, sl16)))):
        for src, dst in pairs:
            for r in range(d):
                for c in range(n_slab):
                    c0 = r * W_DIL + c * 128
                    dst[c, pl.ds(r, tm // d, stride=d), :] = src[:, c0:c0 + 128]
    atts = []
    for c in range(n_slab):
        cs = slice(c * 128, (c + 1) * 128)
        a1, a4, a16 = l1[:, cs], sl4[c], sl16[c]
        m = jnp.maximum(jnp.maximum(a1, a4), a16)
        e1, e4, e16 = jnp.exp(a1 - m), jnp.exp(a4 - m), jnp.exp(a16 - m)
        att = (e1 * o1[:, cs] + e4 * so4[c] + e16 * so16[c]) / (e1 + e4 + e16)
        atts.append(att.astype(BF16))
    acc = jnp.dot(jnp.concatenate(atts, axis=1), w_ref[:W_DIL, :], preferred_element_type=F32)
    acc += jnp.dot(ret_ref[...], w_ref[W_DIL:, :], preferred_element_type=F32)
    out_ref[...] = _layer_norm(DEEPNORM_ALPHA * x_ref[...] + acc, g_ref[...], b_ref[...])


def _out_proj(os_, ls_, ret, x3, w_bf, g, b):
    B, S, D = x3.shape
    cls = lambda d: pl.BlockSpec((None, TM // d, d * W_DIL), lambda b_, i: (b_, i, 0))
    half = cls(1)
    full = pl.BlockSpec((None, TM, D), lambda b_, i: (b_, i, 0))
    vec = pl.BlockSpec((1, D), lambda b_, i: (0, 0))
    return pl.pallas_call(
        _out_proj_kernel,
        grid=(B, S // TM),
        in_specs=[half, cls(4), cls(16), half, cls(4), cls(16), half, full,
                  pl.BlockSpec((D, D), lambda b_, i: (0, 0)), vec, vec],
        out_specs=full,
        out_shape=jax.ShapeDtypeStruct((B, S, D), F32),
        scratch_shapes=[pltpu.VMEM((W_DIL // 128, TM, 128), F32)] * 4,
        compiler_params=_cparams("parallel", "parallel"),
        name="out_proj_ln1",
    )(*os_, *ls_, ret, x3, w_bf, g.reshape(1, D), b.reshape(1, D))


def _mem_kv_kernel(m_ref, wk_ref, wv_ref, k_ref, v_ref):
    mb = m_ref[...].astype(BF16)
    k_ref[...] = jnp.dot(mb, wk_ref[...], preferred_element_type=F32).astype(BF16)
    v_ref[...] = jnp.dot(mb, wv_ref[...], preferred_element_type=F32).astype(BF16)


def _mem_kv(mem, wk_bf, wv_bf):
    B, M, D = mem.shape
    blk = pl.BlockSpec((None, M, D), lambda b: (b, 0, 0))
    wsp = pl.BlockSpec((D, D), lambda b: (0, 0))
    shp = jax.ShapeDtypeStruct((B, M, D), BF16)
    return pl.pallas_call(
        _mem_kv_kernel, grid=(B,), in_specs=[blk, wsp, wsp], out_specs=[blk, blk], out_shape=[shp, shp],
        compiler_params=_cparams("parallel"), name="mem_kv",
    )(mem, wk_bf, wv_bf)


def _cross_attn_kernel(x_ref, k_ref, v_ref, wq_ref, wo_ref, g_ref, b_ref, out_ref):
    x = x_ref[...]
    hd = x.shape[-1] // N_HEADS_MEM
    q = (jnp.dot(x.astype(BF16), wq_ref[...], preferred_element_type=F32) * (hd ** -0.5)).astype(BF16)
    outs = []
    for hh in range(N_HEADS_MEM):
        cs = slice(hh * hd, (hh + 1) * hd)
        s = _nt_dot(q[:, cs], k_ref[:, cs])
        m = jnp.max(s, axis=-1, keepdims=True)
        p = jnp.exp(s - m)
        l = jnp.sum(p, axis=-1, keepdims=True)
        outs.append((jnp.dot(p.astype(BF16), v_ref[:, cs], preferred_element_type=F32) / l).astype(BF16))
    o = jnp.concatenate(outs, axis=-1)
    c = jnp.dot(o, wo_ref[...], preferred_element_type=F32)
    out_ref[...] = _layer_norm(DEEPNORM_ALPHA * x + c, g_ref[...], b_ref[...])


def _cross_attn(x3, kmem, vmem, wq_bf, wo_bf, g, b):
    B, S, D = x3.shape
    M = kmem.shape[1]
    xs = pl.BlockSpec((None, TM, D), lambda b_, i: (b_, i, 0))
    ms = pl.BlockSpec((None, M, D), lambda b_, i: (b_, 0, 0))
    ws = pl.BlockSpec((D, D), lambda b_, i: (0, 0))
    vs = pl.BlockSpec((1, D), lambda b_, i: (0, 0))
    return pl.pallas_call(
        _cross_attn_kernel,
        grid=(B, S // TM),
        in_specs=[xs, ms, ms, ws, ws, vs, vs],
        out_specs=xs,
        out_shape=jax.ShapeDtypeStruct((B, S, D), F32),
        compiler_params=_cparams("parallel", "parallel"),
        name="cross_attn_ln2",
    )(x3, kmem, vmem, wq_bf, wo_bf, g.reshape(1, D), b.reshape(1, D)).reshape(B * S, D)


def _split_bf16(t):
    hi = t.astype(BF16)
    return hi, (t - hi.astype(F32)).astype(BF16)


def _router_kernel(x_ref, w_ref, b_ref, idx_ref, wt_ref):
    x_hi, x_lo = _split_bf16(x_ref[...])
    by_hi = _nt_dot(w_ref[...], x_hi)
    logits = (by_hi[:N_EXPERTS] + (by_hi[N_EXPERTS:] + _nt_dot(w_ref[:N_EXPERTS, :], x_lo))) + b_ref[:, :1]
    e_iota = lax.broadcasted_iota(jnp.int32, logits.shape, 0)
    cur = logits
    vals, idxs = [], []
    for _ in range(TOP_K):
        m = jnp.max(cur, axis=0, keepdims=True)
        idx = jnp.min(jnp.where(cur == m, e_iota, N_EXPERTS), axis=0, keepdims=True)
        cur = jnp.where(e_iota == idx, -jnp.inf, cur)
        vals.append(m)
        idxs.append(idx)
    es = [jnp.exp(v - vals[0]) for v in vals]
    tot = es[0] + es[1] + es[2] + es[3]
    idx_ref[...] = jnp.concatenate(idxs, axis=0)
    wt_ref[...] = jnp.concatenate([e / tot for e in es], axis=0)


def _router(x2, router_w, router_b):
    N, D = x2.shape
    out = pl.BlockSpec((TOP_K, T_ROUTE), lambda i: (0, i))
    return pl.pallas_call(
        _router_kernel,
        grid=(N // T_ROUTE,),
        in_specs=[pl.BlockSpec((T_ROUTE, D), lambda i: (i, 0)),
                  pl.BlockSpec((2 * N_EXPERTS, D), lambda i: (0, 0)),
                  pl.BlockSpec((N_EXPERTS, 128), lambda i: (0, 0))],
        out_specs=[out, out],
        out_shape=[jax.ShapeDtypeStruct((TOP_K, N), jnp.int32), jax.ShapeDtypeStruct((TOP_K, N), F32)],
        compiler_params=_cparams("parallel"),
        name="router_top4",
    )(x2, jnp.concatenate(_split_bf16(router_w.T.astype(F32)), axis=0),
      jnp.broadcast_to(router_b.astype(F32)[:, None], (N_EXPERTS, 128)))


def _rank_kernel(idx_ref, tri_ref, rank_ref, cnt_ref, carry_ref):
    @pl.when(pl.program_id(0) == 0)
    def _():
        carry_ref[...] = jnp.zeros_like(carry_ref)

    T = idx_ref.shape[1]
    e_iota = lax.broadcasted_iota(jnp.int32, (N_EXPERTS, T), 0)
    hot = [e_iota == idx_ref[k:k + 1, :] for k in range(TOP_K)]
    c = sum(h.astype(F32) for h in hot)
    before = jnp.dot(c.astype(BF16), tri_ref[...], preferred_element_type=F32) + carry_ref[:, :1]
    rank_ref[...] = jnp.concatenate(
        [jnp.sum(jnp.where(h, before, 0.0), axis=0, keepdims=True) for h in hot], axis=0).astype(jnp.int32)
    carry_ref[...] = carry_ref[...] + jnp.sum(c, axis=1, keepdims=True)
    cnt_ref[...] = carry_ref[...].astype(jnp.int32)


def _ranks(idx):
    N = idx.shape[1]
    T = T_RANK
    tri = jnp.asarray(np.arange(T)[:, None] < np.arange(T)[None, :], BF16)
    blk = pl.BlockSpec((TOP_K, T), lambda i: (0, i))
    return pl.pallas_call(
        _rank_kernel,
        grid=(N // T,),
        in_specs=[blk, pl.BlockSpec((T, T), lambda i: (0, 0))],
        out_specs=[blk, pl.BlockSpec((N_EXPERTS, 128), lambda i: (0, 0))],
        out_shape=[jax.ShapeDtypeStruct((TOP_K, N), jnp.int32),
                   jax.ShapeDtypeStruct((N_EXPERTS, 128), jnp.int32)],
        scratch_shapes=[pltpu.VMEM((N_EXPERTS, 128), F32)],
        compiler_params=_cparams("arbitrary"),
        name="expert_ranks",
    )(idx, tri)


def _dest_kernel(idx_ref, rank_ref, start_ref, dest_ref):
    T = idx_ref.shape[1]
    e_iota = lax.broadcasted_iota(jnp.int32, (N_EXPERTS, T), 0)
    start = jnp.tile(start_ref[...], (1, T // 128))
    rows = [jnp.sum(jnp.where(e_iota == idx_ref[k:k + 1, :], start, 0), axis=0, keepdims=True)
            for k in range(TOP_K)]
    dest_ref[...] = jnp.concatenate(rows, axis=0) + rank_ref[...]


def _dests(idx, rank, group_start):
    N = idx.shape[1]
    T = T_ROUTE
    blk = pl.BlockSpec((TOP_K, T), lambda i: (0, i))
    return pl.pallas_call(
        _dest_kernel,
        grid=(N // T,),
        in_specs=[blk, blk, pl.BlockSpec((N_EXPERTS, 128), lambda i: (0, 0))],
        out_specs=blk,
        out_shape=jax.ShapeDtypeStruct((TOP_K, N), jnp.int32),
        compiler_params=_cparams("parallel"),
        name="expert_dests",
    )(idx, rank, jnp.broadcast_to(group_start[:, None], (N_EXPERTS, 128)))


MOVE_UNROLL = 8
SLAB_ROWS = 8


def _for_rows(T, fn):
    def body(g, carry):
        for u in range(MOVE_UNROLL):
            for k in range(TOP_K):
                t = g * MOVE_UNROLL + u
                fn(t, k, g * (MOVE_UNROLL * TOP_K) + (u * TOP_K + k))
        return carry

    lax.fori_loop(0, T // MOVE_UNROLL, body, 0)


def _slab(ref_at, row8):
    return ref_at.at[pl.ds(pl.multiple_of(row8, SLAB_ROWS), SLAB_ROWS), :]


def _dispatch_kernel(dcur_ref, dprev_ref, x_ref, *refs, n_cast):
    w_refs, xs_ref, wb_refs, (pk_ref, sem) = refs[:n_cast], refs[n_cast], refs[n_cast + 1:-2], refs[-2:]
    for w_ref, wb_ref in zip(w_refs, wb_refs):
        wb_ref[...] = w_ref[...].astype(BF16)
    i = pl.program_id(0)
    n = pl.num_programs(0)
    T = x_ref.shape[0]
    slot = i % 2
    for c in range(SLAB_ROWS):
        pk_ref[slot, pl.ds(c, T, stride=SLAB_ROWS), :] = x_ref[:, c * 128:(c + 1) * 128]

    def row_copy(sl, dref, t, j):
        return pltpu.make_async_copy(_slab(pk_ref.at[sl], t * SLAB_ROWS), _slab(xs_ref, dref[j]), sem.at[sl])

    _for_rows(T, lambda t, k, j: row_copy(slot, dcur_ref, t, j).start(priority=k % 2))

    @pl.when(i > 0)
    def _():
        _for_rows(T, lambda t, k, j: row_copy(1 - slot, dprev_ref, t, j).wait())

    @pl.when(i == n - 1)
    def _():
        _for_rows(T, lambda t, k, j: row_copy(slot, dcur_ref, t, j).wait())


CAST_BLOCK_BYTES = 2 * 1024 * 1024


def _dispatch(x2, dest8, P, weights):
    N, D = x2.shape
    assert D == SLAB_ROWS * 128
    T = T_MOVE
    n = N // T
    flat = [w.reshape(-1, w.shape[-1]) for w in weights]
    fused = all(f.shape[0] % n == 0 and (f.shape[0] // n) % 16 == 0
                and (f.shape[0] // n) * f.shape[1] * 4 <= CAST_BLOCK_BYTES for f in flat)
    if not fused:
        flat = []
    w_specs = [pl.BlockSpec((f.shape[0] // n, f.shape[1]), lambda i: (i, 0)) for f in flat]
    smem = lambda f: pl.BlockSpec((TOP_K * T,), f, memory_space=pltpu.SMEM)
    outs = pl.pallas_call(
        functools.partial(_dispatch_kernel, n_cast=len(flat)),
        grid=(n,),
        in_specs=[smem(lambda i: (i,)), smem(lambda i: (jnp.maximum(i - 1, 0),)),
                  pl.BlockSpec((T, D), lambda i: (i, 0))] + w_specs,
        out_specs=[pl.BlockSpec(memory_space=pl.ANY)] + w_specs,
        out_shape=[jax.ShapeDtypeStruct((P * SLAB_ROWS, 128), F32)]
                  + [jax.ShapeDtypeStruct(f.shape, BF16) for f in flat],
        scratch_shapes=[pltpu.VMEM((2, T * SLAB_ROWS, 128), F32), pltpu.SemaphoreType.DMA((2,))],
        compiler_params=_cparams("arbitrary"),
        name="moe_dispatch",
    )(dest8, dest8, x2, *flat)
    if fused:
        return outs[0], [o.reshape(w.shape) for o, w in zip(outs[1:], weights)]
    return outs[0], [w.astype(BF16) for w in weights]


def _expert_kernel(be_ref, bv_ref, xs_ref, wg_ref, wu_ref, wd_ref, bg_ref, bu_ref, bd_ref, ys_ref):
    j = pl.program_id(0)
    valid = bv_ref[j]

    @pl.when(valid > 0)
    def _():
        bm = xs_ref.shape[0] // SLAB_ROWS
        row = lax.broadcasted_iota(jnp.int32, (bm, 128), 0)
        xb = jnp.concatenate(
            [jnp.where(row < valid, xs_ref[pl.ds(c, bm, stride=SLAB_ROWS), :], 0.0).astype(BF16)
             for c in range(SLAB_ROWS)], axis=1)

        def proj(w_ref, b_ref):
            return jnp.dot(xb, w_ref[...], preferred_element_type=F32) + b_ref[...]

        gate = jnp.minimum(proj(wg_ref, bg_ref), SWIGLU_LIMIT)
        up = jnp.clip(proj(wu_ref, bu_ref), -SWIGLU_LIMIT, SWIGLU_LIMIT)
        hmid = gate * (1.0 / (1.0 + jnp.exp(-SWIGLU_ALPHA * gate))) * (up + 1.0)
        out = jnp.dot(hmid.astype(BF16), wd_ref[...], preferred_element_type=F32) + bd_ref[...]
        for c in range(SLAB_ROWS):
            ys_ref[pl.ds(c, bm, stride=SLAB_ROWS), :] = out[:, c * 128:(c + 1) * 128]


def _experts(xs, block_e, block_valid, wg, wu, wd, bg, bu, bd):
    D = SLAB_ROWS * 128
    F = wg.shape[-1]
    wspec = lambda a, c: pl.BlockSpec((None, a, c), lambda j, be, bv: (be[j], 0, 0))
    slabs = pl.BlockSpec((BM * SLAB_ROWS, 128), lambda j, be, bv: (j, 0))
    grid_spec = pltpu.PrefetchScalarGridSpec(
        num_scalar_prefetch=2,
        grid=(xs.shape[0] // (BM * SLAB_ROWS),),
        in_specs=[slabs, wspec(D, F), wspec(D, F), wspec(F, D), wspec(1, F), wspec(1, F), wspec(1, D)],
        out_specs=slabs,
    )
    return pl.pallas_call(
        _expert_kernel,
        grid_spec=grid_spec,
        out_shape=jax.ShapeDtypeStruct(xs.shape, F32),
        compiler_params=_cparams("arbitrary"),
        name="moe_experts",
    )(block_e, block_valid, xs, wg, wu, wd, bg, bu, bd)


def _combine_kernel(dcur_ref, dnext_ref, x_ref, wt_ref, ys_ref, g_ref, b_ref, out_ref, buf_ref, sem):
    i = pl.program_id(0)
    n = pl.num_programs(0)
    T = x_ref.shape[0]
    slot = i % 2

    def row_copy(sl, dref, t, k, j):
        return pltpu.make_async_copy(_slab(ys_ref, dref[j]), _slab(buf_ref.at[sl], (k * T + t) * SLAB_ROWS),
                                     sem.at[sl])

    @pl.when(i == 0)
    def _():
        _for_rows(T, lambda t, k, j: row_copy(slot, dcur_ref, t, k, j).start(priority=k % 2))

    @pl.when(i + 1 < n)
    def _():
        _for_rows(T, lambda t, k, j: row_copy(1 - slot, dnext_ref, t, k, j).start(priority=k % 2))

    _for_rows(T, lambda t, k, j: row_copy(slot, dcur_ref, t, k, j).wait())

    wt = wt_ref[...]
    wb = [jnp.broadcast_to(wt[:, k:k + 1], (T, 128)) for k in range(TOP_K)]
    ys = []
    for c in range(SLAB_ROWS):
        chunk = lambda k: buf_ref[slot, pl.ds(k * T * SLAB_ROWS + c, T, stride=SLAB_ROWS), :]
        yc = wb[0] * chunk(0)
        for k in range(1, TOP_K):
            yc = yc + wb[k] * chunk(k)
        ys.append(yc)
    y = jnp.concatenate(ys, axis=1)
    out_ref[...] = _layer_norm(DEEPNORM_ALPHA * x_ref[...] + y, g_ref[...], b_ref[...])


def _combine(x2, dest_flat, wt_tok, ys, g, b):
    N, D = x2.shape
    T = T_COMBINE
    n = N // T
    vec = pl.BlockSpec((1, D), lambda i: (0, 0))
    smem = lambda f: pl.BlockSpec((TOP_K * T,), f, memory_space=pltpu.SMEM)
    return pl.pallas_call(
        _combine_kernel,
        grid=(n,),
        in_specs=[smem(lambda i: (i,)), smem(lambda i: (jnp.minimum(i + 1, n - 1),)),
                  pl.BlockSpec((T, D), lambda i: (i, 0)),
                  pl.BlockSpec((T, TOP_K), lambda i: (i, 0)),
                  pl.BlockSpec(memory_space=pl.ANY), vec, vec],
        out_specs=pl.BlockSpec((T, D), lambda i: (i, 0)),
        out_shape=jax.ShapeDtypeStruct((N, D), F32),
        scratch_shapes=[pltpu.VMEM((2, TOP_K * T * SLAB_ROWS, 128), F32), pltpu.SemaphoreType.DMA((2,))],
        compiler_params=_cparams("arbitrary"),
        name="moe_combine_ln3",
    )(dest_flat, dest_flat, x2, wt_tok, ys, g.reshape(1, D), b.reshape(1, D))


def _moe(x2, router_w, router_b, w_gate, b_gate, w_up, b_up, w_down, b_down, ln_g, ln_b):
    N, D = x2.shape
    idx, wt = _router(x2, router_w, router_b)
    rank, cnt = _ranks(idx)
    counts = cnt[:, 0]
    blocks = (counts + BM - 1) // BM
    blk_end = jnp.cumsum(blocks)
    blk_start = blk_end - blocks
    dest = _dests(idx, rank, (blk_start * BM).astype(jnp.int32))
    n_blocks = -(-N * TOP_K // BM) + N_EXPERTS
    bi = jnp.arange(n_blocks, dtype=jnp.int32)
    be_raw = jnp.sum(bi[:, None] >= blk_end[None, :], axis=1).astype(jnp.int32)
    last_e = jnp.max(jnp.where(counts > 0, jnp.arange(N_EXPERTS), 0)).astype(jnp.int32)
    block_e = jnp.minimum(be_raw, last_e)
    in_use = bi < blk_end[-1]
    block_valid = jnp.where(in_use, jnp.clip(counts[block_e] - (bi - blk_start[block_e]) * BM, 0, BM), 0)
    dest_flat = (dest * SLAB_ROWS).T.reshape(N * TOP_K)
    xs, (wg_bf, wu_bf, wd_bf) = _dispatch(x2, dest_flat, n_blocks * BM, [w_gate, w_up, w_down])
    f3 = lambda t: t.astype(F32)[:, None, :]
    ys = _experts(xs, block_e, block_valid.astype(jnp.int32), wg_bf, wu_bf, wd_bf,
                  f3(b_gate), f3(b_up), f3(b_down))
    return _combine(x2, dest_flat, wt.T, ys, ln_g, ln_b)


def kernel(x, mem, w_in, ret_norm_g, w_out, ln1_g, ln1_b, mem_wq, mem_wk, mem_wv, mem_wo, ln2_g, ln2_b,
           router_w, router_b, w_gate, b_gate, w_up, b_up, w_down, b_down, ln3_g, ln3_b):
    B, S, D = x.shape
    assert [d for _, d in DILATED_PATTERNS] == [1, 4, 16] and all(w // d == BAND for w, d in DILATED_PATTERNS)
    for l in range(w_in.shape[0]):
        h, qkv4, qkv16 = _in_proj(x, w_in[l].astype(BF16))
        pats = [_band_attn(h, B, S, 1, N_IN // GROUP),
                _band_attn(qkv4, B, S // 4, 4, 3), _band_attn(qkv16, B, S // 16, 16, 3)]
        ret = _retention(h, ret_norm_g[l])
        x1 = _out_proj([p[0] for p in pats], [p[1] for p in pats], ret, x,
                       w_out[l].astype(BF16), ln1_g[l], ln1_b[l])
        kmem, vmem = _mem_kv(mem, mem_wk[l].astype(BF16), mem_wv[l].astype(BF16))
        x2 = _cross_attn(x1, kmem, vmem, mem_wq[l].astype(BF16), mem_wo[l].astype(BF16),
                         ln2_g[l], ln2_b[l])
        x = _moe(x2, router_w[l], router_b[l], w_gate[l], b_gate[l], w_up[l], b_up[l],
                 w_down[l], b_down[l], ln3_g[l], ln3_b[l]).reshape(B, S, D)
    return x
```

```python
import functools

import jax
import jax.numpy as jnp
import numpy as np
from jax import lax
from jax.experimental import pallas as pl
from jax.experimental.pallas import tpu as pltpu

F32 = jnp.float32
BF16 = jnp.bfloat16

HEAD_DIM = 64
N_HEADS_DIL = 8
N_HEADS_RET = 8
W_DIL = N_HEADS_DIL * HEAD_DIM
W_RET = N_HEADS_RET * HEAD_DIM
N_IN = 3 * W_DIL + 4 * W_RET
DILATED_PATTERNS = ((128, 1), (512, 4), (2048, 16))
BAND = 128
ROPE_THETA = 500000.0
ROT_DIM = HEAD_DIM // 4
RET_THETA = 10000.0
RET_CHUNK = 128
N_HEADS_MEM = 4
N_EXPERTS = 32
TOP_K = 4
SWIGLU_LIMIT = 7.0
SWIGLU_ALPHA = 1.702
LN_EPS = 1e-5
DEPTH = 1
DEEPNORM_ALPHA = (2 * DEPTH) ** 0.25

GROUP = 512
TM = 512
TQ = 512
T_ROUTE = 1024
T_RANK = 512
T_MOVE = 512
T_COMBINE = 256
BM = 512
VMEM_LIMIT = 56 * 1024 * 1024
NEG = -1e30


def _cparams(*sem):
    return pltpu.CompilerParams(dimension_semantics=sem, vmem_limit_bytes=VMEM_LIMIT)


def _nt_dot(a, b):
    return lax.dot_general(a, b, (((1,), (1,)), ((), ())), preferred_element_type=F32)


def _layer_norm(y, g, b):
    mu = jnp.mean(y, axis=-1, keepdims=True)
    d = y - mu
    var = jnp.mean(d * d, axis=-1, keepdims=True)
    return d * lax.rsqrt(var + LN_EPS) * g + b


def _in_proj_kernel(x_ref, w_ref, ca_ref, la_ref, ha_ref, cr_ref, lr_ref, hr_ref, o_ref, o4_ref, o16_ref,
                    acc_ref):
    xb = x_ref[...].astype(BF16)
    rep = GROUP // 128
    tm = x_ref.shape[0]
    qkv = 3 * W_DIL

    def rot(acc, c_ref, lo_ref, hi_ref, half):
        c = jnp.tile(c_ref[...], (1, rep))
        lo = jnp.tile(lo_ref[...], (1, rep))
        hi = jnp.tile(hi_ref[...], (1, rep))
        up = pltpu.roll(acc, GROUP - half, axis=1)
        dn = pltpu.roll(acc, half, axis=1)
        return acc * c + up * lo + dn * hi

    for g in range(N_IN // GROUP):
        acc = jnp.dot(xb, w_ref[:, g * GROUP:(g + 1) * GROUP], preferred_element_type=F32)
        if g == 0:
            acc = rot(acc, ca_ref, la_ref, ha_ref, ROT_DIM // 2) * (HEAD_DIM ** -0.5)
        elif g == 1:
            acc = rot(acc, ca_ref, la_ref, ha_ref, ROT_DIM // 2)
        elif g == 3:
            acc = rot(acc, cr_ref, lr_ref, hr_ref, HEAD_DIM // 2)
        elif g == 4:
            acc = rot(acc, cr_ref, lr_ref, hr_ref, HEAD_DIM // 2) * (HEAD_DIM ** -0.5)
        o_ref[:, g * GROUP:(g + 1) * GROUP] = acc.astype(BF16)
        if g < 3:
            for c in range(rep):
                acc_ref[c] = acc[:, c * 128:(c + 1) * 128]
            for d, od_ref in ((4, o4_ref), (16, o16_ref)):
                for r in range(d):
                    for c in range(rep):
                        c0 = r * qkv + g * GROUP + c * 128
                        od_ref[:, c0:c0 + 128] = acc_ref[c, pl.ds(r, tm // d, stride=d), :].astype(BF16)


def _rotary_tables(S, theta, rot_dim):
    half = rot_dim // 2
    f32 = np.float32
    inv = f32(1.0) / (f32(theta) ** (np.arange(half, dtype=f32) / f32(half)))
    ang = np.arange(S, dtype=f32)[:, None] * inv[None, :]
    cos, sin = np.cos(ang).astype(f32), np.sin(ang).astype(f32)
    pad = HEAD_DIM - rot_dim
    c = np.concatenate([cos, cos, np.ones((S, pad), f32)], axis=1)
    lo = np.concatenate([-sin, np.zeros((S, half + pad), f32)], axis=1)
    hi = np.concatenate([np.zeros((S, half), f32), sin, np.zeros((S, pad), f32)], axis=1)
    two = lambda t: np.concatenate([t, t], axis=1)
    return two(c), two(lo), two(hi)


def _in_proj(x3, w_bf):
    B, S, D = x3.shape
    tabs = _rotary_tables(S, ROPE_THETA, ROT_DIM) + _rotary_tables(S, RET_THETA, HEAD_DIM)
    tab_spec = pl.BlockSpec((TM, 128), lambda b, i: (i, 0))
    qkv = 3 * W_DIL
    cls_spec = lambda d: pl.BlockSpec((None, TM // d, d * qkv), lambda b, i: (b, i, 0))
    cls_shape = lambda d: jax.ShapeDtypeStruct((B, S // d, d * qkv), BF16)
    return pl.pallas_call(
        _in_proj_kernel,
        grid=(B, S // TM),
        in_specs=[pl.BlockSpec((None, TM, D), lambda b, i: (b, i, 0)),
                  pl.BlockSpec((D, N_IN), lambda b, i: (0, 0))] + [tab_spec] * 6,
        out_specs=[pl.BlockSpec((None, TM, N_IN), lambda b, i: (b, i, 0)), cls_spec(4), cls_spec(16)],
        out_shape=[jax.ShapeDtypeStruct((B, S, N_IN), BF16), cls_shape(4), cls_shape(16)],
        scratch_shapes=[pltpu.VMEM((GROUP // 128, TM, 128), F32)],
        compiler_params=_cparams("parallel", "parallel"),
        name="in_proj",
    )(x3, w_bf, *tabs)


def _band_attn_kernel(q_ref, kc_ref, kp_ref, vc_ref, vp_ref, o_ref, l_ref,
                      k_all, v_all, s_scr, p_scr, r_scr):
    j = pl.program_id(2)
    tq = q_ref.shape[0]
    nsub = tq // BAND
    pair = 2 * HEAD_DIM
    n_pairs = N_HEADS_DIL // 2
    k_all[0:BAND, :] = kp_ref[...]
    k_all[BAND:, :] = kc_ref[...]
    v_all[0:BAND, :] = vp_ref[...]
    v_all[BAND:, :] = vc_ref[...]
    qi = lax.broadcasted_iota(jnp.int32, (BAND, 2 * BAND), 0)
    kj = lax.broadcasted_iota(jnp.int32, (BAND, 2 * BAND), 1)
    dist = qi + BAND - kj
    band = (dist >= 0) & (dist <= BAND)
    bias = jnp.where(band, 0.0, NEG)
    bias_first = jnp.where(band & ((kj >= BAND) | (j > 0)), 0.0, NEG)
    low = lax.broadcasted_iota(jnp.int32, (BAND, pair), 1) < HEAD_DIM
    zero = jnp.zeros((BAND, pair), BF16)

    for pr in range(n_pairs):
        cs = slice(pr * pair, (pr + 1) * pair)
        for n in range(nsub):
            q = q_ref[n * BAND:(n + 1) * BAND, cs]
            kk = k_all[n * BAND:(n + 2) * BAND, cs]
            b = bias_first if n == 0 else bias
            i = pr * nsub + n
            s_scr[2 * i] = _nt_dot(jnp.where(low, q, zero), kk) + b
            s_scr[2 * i + 1] = _nt_dot(jnp.where(low, zero, q), kk) + b

    for pr in range(n_pairs):
        cs = slice(pr * pair, (pr + 1) * pair)
        for n in range(nsub):
            i = pr * nsub + n
            stats = []
            for a in range(2):
                s = s_scr[2 * i + a]
                m = jnp.max(s, axis=-1, keepdims=True)
                p = jnp.exp(s - m)
                l = jnp.sum(p, axis=-1, keepdims=True)
                p_scr[2 * i + a] = p.astype(BF16)
                stats.append((m, l))
            (m0, l0), (m1, l1) = stats
            r_scr[i] = jnp.where(low, 1.0 / l0, 1.0 / l1)
            l_ref[n * BAND:(n + 1) * BAND, cs] = jnp.where(low, m0 + jnp.log(l0), m1 + jnp.log(l1))

    for pr in range(n_pairs):
        cs = slice(pr * pair, (pr + 1) * pair)
        for n in range(nsub):
            i = pr * nsub + n
            vv = v_all[n * BAND:(n + 2) * BAND, cs]
            o0 = jnp.dot(p_scr[2 * i], vv, preferred_element_type=F32)
            o1 = jnp.dot(p_scr[2 * i + 1], vv, preferred_element_type=F32)
            o_ref[n * BAND:(n + 1) * BAND, cs] = jnp.where(low, o0, o1) * r_scr[i]


def _band_attn(src, B, L, d, gpr):
    tq = min(TQ, L)
    assert L % tq == 0
    sub = tq // BAND
    cur = lambda g: pl.BlockSpec((None, tq, GROUP), lambda b, r, j: (b, j, r * gpr + g))
    prev = lambda g: pl.BlockSpec((None, BAND, GROUP),
                                  lambda b, r, j: (b, jnp.maximum(j * sub - 1, 0), r * gpr + g))
    out_spec = pl.BlockSpec((None, tq, W_DIL), lambda b, r, j: (b, j, r))
    shp = jax.ShapeDtypeStruct((B, L, d * W_DIL), F32)
    n_blk = (N_HEADS_DIL // 2) * sub
    return pl.pallas_call(
        _band_attn_kernel,
        grid=(B, d, L // tq),
        in_specs=[cur(0), cur(1), prev(1), cur(2), prev(2)],
        out_specs=[out_spec, out_spec],
        out_shape=[shp, shp],
        scratch_shapes=[pltpu.VMEM((BAND + tq, GROUP), BF16), pltpu.VMEM((BAND + tq, GROUP), BF16),
                        pltpu.VMEM((2 * n_blk, BAND, 2 * BAND), F32),
                        pltpu.VMEM((2 * n_blk, BAND, 2 * BAND), BF16),
                        pltpu.VMEM((n_blk, BAND, 2 * HEAD_DIM), F32)],
        compiler_params=_cparams("parallel", "parallel", "parallel"),
        name=f"band_attn_d{d}",
    )(src, src, src, src, src)


RET_UNROLL = 16


def _retention_kernel(q_ref, k_ref, v_ref, g_ref, gain_ref, din_ref, kd_ref, qd_ref, cd_ref,
                      o_ref, kv_ref, st_ref):
    C = RET_CHUNK
    n_iter = q_ref.shape[0] // (C * RET_UNROLL)
    pair = 2 * HEAD_DIM
    low = lax.broadcasted_iota(jnp.int32, (C, pair), 1) < HEAD_DIM

    def rows(i, u):
        return pl.ds(pl.multiple_of(i * (C * RET_UNROLL), C * RET_UNROLL) + u * C, C)

    def kv_body(i, carry):
        for u in range(RET_UNROLL):
            kd = (k_ref[rows(i, u), :].astype(F32) * kd_ref[...]).astype(BF16)
            kv_ref[i * RET_UNROLL + u] = lax.dot_general(kd, v_ref[rows(i, u), :], (((0,), (0,)), ((), ())),
                                                         preferred_element_type=F32)
        return carry

    lax.fori_loop(0, n_iter, kv_body, 0)

    def scan_body(c, state):
        st_ref[c] = state.astype(BF16)
        return state * cd_ref[...] + kv_ref[c]

    lax.fori_loop(0, n_iter * RET_UNROLL, scan_body, jnp.zeros((pair, pair), F32))

    def out_body(i, carry):
        for u in range(RET_UNROLL):
            q = q_ref[rows(i, u), :]
            k = k_ref[rows(i, u), :]
            v = v_ref[rows(i, u), :]
            prev = st_ref[i * RET_UNROLL + u]
            ys = []
            for a in range(2):
                qa = jnp.where(low, q, jnp.zeros_like(q)) if a == 0 else jnp.where(low, jnp.zeros_like(q), q)
                s = _nt_dot(qa, k) * din_ref[a]
                inner = jnp.dot(s.astype(BF16), v, preferred_element_type=F32)
                qd = (qa.astype(F32) * qd_ref[...]).astype(BF16)
                ys.append(inner + jnp.dot(qd, prev, preferred_element_type=F32))
            y = jnp.where(low, ys[0], ys[1])
            inv = 1.0 / HEAD_DIM
            half_sum = lambda t: jnp.where(low, jnp.sum(jnp.where(low, t, 0.0), axis=-1, keepdims=True),
                                           jnp.sum(jnp.where(low, 0.0, t), axis=-1, keepdims=True))
            dlt = y - half_sum(y) * inv
            var = half_sum(dlt * dlt) * inv
            yn = dlt * lax.rsqrt(var + LN_EPS) * gain_ref[...]
            gr = g_ref[rows(i, u), :].astype(F32)
            o_ref[rows(i, u), :] = (yn * (gr / (1.0 + jnp.exp(-gr)))).astype(BF16)
        return carry

    lax.fori_loop(0, n_iter, out_body, 0)


def _retention_tables():
    H, C = N_HEADS_RET, RET_CHUNK
    f32 = np.float32
    log_g = np.log1p(-(f32(2.0) ** (f32(-5.0) - np.arange(H, dtype=f32)))).astype(f32)
    idx = np.arange(C, dtype=f32)
    diff = idx[:, None] - idx[None, :]
    inner_decay = (np.exp(log_g[:, None, None] * np.maximum(diff, f32(0.0))) * (diff >= 0)).astype(f32)
    k_decay = np.exp(log_g[:, None] * (f32(C - 1) - idx)).astype(f32)
    q_decay = np.exp(log_g[:, None] * (idx + f32(1.0))).astype(f32)
    chunk_decay = np.exp(log_g * f32(C)).astype(f32)
    lanes = lambda t: np.repeat(t.reshape(H // 2, 2, C).transpose(0, 2, 1), HEAD_DIM, axis=2)
    cd = np.ascontiguousarray(np.broadcast_to(
        np.repeat(chunk_decay.reshape(H // 2, 2), HEAD_DIM, axis=1)[:, :, None],
        (H // 2, 2 * HEAD_DIM, 2 * HEAD_DIM)))
    return inner_decay, lanes(k_decay), lanes(q_decay), cd


def _retention(h, gain):
    B, S, _ = h.shape
    lanes = 2 * HEAD_DIM
    col0 = 3 * W_DIL // lanes
    per = W_RET // lanes
    col = lambda g: pl.BlockSpec((None, S, lanes), lambda b, p: (b, 0, col0 + g * per + p))
    din, kd, qd, cd = _retention_tables()
    C = RET_CHUNK
    assert S % (C * RET_UNROLL) == 0
    tab = lambda a: pl.BlockSpec((None, a, lanes), lambda b, p: (p, 0, 0))
    return pl.pallas_call(
        _retention_kernel,
        grid=(B, per),
        in_specs=[col(0), col(1), col(2), col(3),
                  pl.BlockSpec((1, lanes), lambda b, p: (0, p)),
                  pl.BlockSpec((2, C, C), lambda b, p: (p, 0, 0)), tab(C), tab(C), tab(lanes)],
        out_specs=pl.BlockSpec((None, S, lanes), lambda b, p: (b, 0, p)),
        out_shape=jax.ShapeDtypeStruct((B, S, W_RET), BF16),
        scratch_shapes=[pltpu.VMEM((S // C, lanes, lanes), F32), pltpu.VMEM((S // C, lanes, lanes), BF16)],
        compiler_params=_cparams("parallel", "parallel"),
        name="retention",
    )(h, h, h, h, gain.reshape(1, W_RET).astype(F32), din, kd, qd, cd)


def _out_proj_kernel(o1, o4, o16, l1, l4, l16, ret_ref, x_ref, w_ref, g_ref, b_ref, out_ref,
                     so4, so16, sl4, sl16):
    tm = x_ref.shape[0]
    n_slab = W_DIL // 128
    for d, pairs in ((4, ((o4, so4), (l4, sl4))), (16, ((o16, so16), (l16, sl16)))):
        for src, dst in pairs:
            for r in range(d):
                for c in range(n_slab):
                    c0 = r * W_DIL + c * 128
                    dst[c, pl.ds(r, tm // d, stride=d), :] = src[:, c0:c0 + 128]
    atts = []
    for c in range(n_slab):
        cs = slice(c * 128, (c + 1) * 128)
        a1, a4, a16 = l1[:, cs], sl4[c], sl16[c]
        m = jnp.maximum(jnp.maximum(a1, a4), a16)
        e1, e4, e16 = jnp.exp(a1 - m), jnp.exp(a4 - m), jnp.exp(a16 - m)
        att = (e1 * o1[:, cs] + e4 * so4[c] + e16 * so16[c]) / (e1 + e4 + e16)
        atts.append(att.astype(BF16))
    acc = jnp.dot(jnp.concatenate(atts, axis=1), w_ref[:W_DIL, :], preferred_element_type=F32)
    acc += jnp.dot(ret_ref[...], w_ref[W_DIL:, :], preferred_element_type=F32)
    out_ref[...] = _layer_norm(DEEPNORM_ALPHA * x_ref[...] + acc, g_ref[...], b_ref[...])


def _out_proj(os_, ls_, ret, x3, w_bf, g, b):
    B, S, D = x3.shape
    cls = lambda d: pl.BlockSpec((None, TM // d, d * W_DIL), lambda b_, i: (b_, i, 0))
    half = cls(1)
    full = pl.BlockSpec((None, TM, D), lambda b_, i: (b_, i, 0))
    vec = pl.BlockSpec((1, D), lambda b_, i: (0, 0))
    return pl.pallas_call(
        _out_proj_kernel,
        grid=(B, S // TM),
        in_specs=[half, cls(4), cls(16), half, cls(4), cls(16), half, full,
                  pl.BlockSpec((D, D), lambda b_, i: (0, 0)), vec, vec],
        out_specs=full,
        out_shape=jax.ShapeDtypeStruct((B, S, D), F32),
        scratch_shapes=[pltpu.VMEM((W_DIL // 128, TM, 128), F32)] * 4,
        compiler_params=_cparams("parallel", "parallel"),
        name="out_proj_ln1",
    )(*os_, *ls_, ret, x3, w_bf, g.reshape(1, D), b.reshape(1, D))


def _mem_kv_kernel(m_ref, wk_ref, wv_ref, k_ref, v_ref):
    mb = m_ref[...].astype(BF16)
    k_ref[...] = jnp.dot(mb, wk_ref[...], preferred_element_type=F32).astype(BF16)
    v_ref[...] = jnp.dot(mb, wv_ref[...], preferred_element_type=F32).astype(BF16)


def _mem_kv(mem, wk_bf, wv_bf):
    B, M, D = mem.shape
    blk = pl.BlockSpec((None, M, D), lambda b: (b, 0, 0))
    wsp = pl.BlockSpec((D, D), lambda b: (0, 0))
    shp = jax.ShapeDtypeStruct((B, M, D), BF16)
    return pl.pallas_call(
        _mem_kv_kernel, grid=(B,), in_specs=[blk, wsp, wsp], out_specs=[blk, blk], out_shape=[shp, shp],
        compiler_params=_cparams("parallel"), name="mem_kv",
    )(mem, wk_bf, wv_bf)


def _cross_attn_kernel(x_ref, k_ref, v_ref, wq_ref, wo_ref, g_ref, b_ref, out_ref):
    x = x_ref[...]
    hd = x.shape[-1] // N_HEADS_MEM
    q = (jnp.dot(x.astype(BF16), wq_ref[...], preferred_element_type=F32) * (hd ** -0.5)).astype(BF16)
    outs = []
    for hh in range(N_HEADS_MEM):
        cs = slice(hh * hd, (hh + 1) * hd)
        s = _nt_dot(q[:, cs], k_ref[:, cs])
        m = jnp.max(s, axis=-1, keepdims=True)
        p = jnp.exp(s - m)
        l = jnp.sum(p, axis=-1, keepdims=True)
        outs.append((jnp.dot(p.astype(BF16), v_ref[:, cs], preferred_element_type=F32) / l).astype(BF16))
    o = jnp.concatenate(outs, axis=-1)
    c = jnp.dot(o, wo_ref[...], preferred_element_type=F32)
    out_ref[...] = _layer_norm(DEEPNORM_ALPHA * x + c, g_ref[...], b_ref[...])


def _cross_attn(x3, kmem, vmem, wq_bf, wo_bf, g, b):
    B, S, D = x3.shape
    M = kmem.shape[1]
    xs = pl.BlockSpec((None, TM, D), lambda b_, i: (b_, i, 0))
    ms = pl.BlockSpec((None, M, D), lambda b_, i: (b_, 0, 0))
    ws = pl.BlockSpec((D, D), lambda b_, i: (0, 0))
    vs = pl.BlockSpec((1, D), lambda b_, i: (0, 0))
    return pl.pallas_call(
        _cross_attn_kernel,
        grid=(B, S // TM),
        in_specs=[xs, ms, ms, ws, ws, vs, vs],
        out_specs=xs,
        out_shape=jax.ShapeDtypeStruct((B, S, D), F32),
        compiler_params=_cparams("parallel", "parallel"),
        name="cross_attn_ln2",
    )(x3, kmem, vmem, wq_bf, wo_bf, g.reshape(1, D), b.reshape(1, D)).reshape(B * S, D)


def _split_bf16(t):
    hi = t.astype(BF16)
    return hi, (t - hi.astype(F32)).astype(BF16)


def _router_kernel(x_ref, w_ref, b_ref, idx_ref, wt_ref):
    x_hi, x_lo = _split_bf16(x_ref[...])
    by_hi = _nt_dot(w_ref[...], x_hi)
    logits = (by_hi[:N_EXPERTS] + (by_hi[N_EXPERTS:] + _nt_dot(w_ref[:N_EXPERTS, :], x_lo))) + b_ref[:, :1]
    e_iota = lax.broadcasted_iota(jnp.int32, logits.shape, 0)
    cur = logits
    vals, idxs = [], []
    for _ in range(TOP_K):
        m = jnp.max(cur, axis=0, keepdims=True)
        idx = jnp.min(jnp.where(cur == m, e_iota, N_EXPERTS), axis=0, keepdims=True)
        cur = jnp.where(e_iota == idx, -jnp.inf, cur)
        vals.append(m)
        idxs.append(idx)
    es = [jnp.exp(v - vals[0]) for v in vals]
    tot = es[0] + es[1] + es[2] + es[3]
    idx_ref[...] = jnp.concatenate(idxs, axis=0)
    wt_ref[...] = jnp.concatenate([e / tot for e in es], axis=0)


def _router(x2, router_w, router_b):
    N, D = x2.shape
    out = pl.BlockSpec((TOP_K, T_ROUTE), lambda i: (0, i))
    return pl.pallas_call(
        _router_kernel,
        grid=(N // T_ROUTE,),
        in_specs=[pl.BlockSpec((T_ROUTE, D), lambda i: (i, 0)),
                  pl.BlockSpec((2 * N_EXPERTS, D), lambda i: (0, 0)),
                  pl.BlockSpec((N_EXPERTS, 128), lambda i: (0, 0))],
        out_specs=[out, out],
        out_shape=[jax.ShapeDtypeStruct((TOP_K, N), jnp.int32), jax.ShapeDtypeStruct((TOP_K, N), F32)],
        compiler_params=_cparams("parallel"),
        name="router_top4",
    )(x2, jnp.concatenate(_split_bf16(router_w.T.astype(F32)), axis=0),
      jnp.broadcast_to(router_b.astype(F32)[:, None], (N_EXPERTS, 128)))


def _rank_kernel(idx_ref, tri_ref, rank_ref, cnt_ref, carry_ref):
    @pl.when(pl.program_id(0) == 0)
    def _():
        carry_ref[...] = jnp.zeros_like(carry_ref)

    T = idx_ref.shape[1]
    e_iota = lax.broadcasted_iota(jnp.int32, (N_EXPERTS, T), 0)
    hot = [e_iota == idx_ref[k:k + 1, :] for k in range(TOP_K)]
    c = sum(h.astype(F32) for h in hot)
    before = jnp.dot(c.astype(BF16), tri_ref[...], preferred_element_type=F32) + carry_ref[:, :1]
    rank_ref[...] = jnp.concatenate(
        [jnp.sum(jnp.where(h, before, 0.0), axis=0, keepdims=True) for h in hot], axis=0).astype(jnp.int32)
    carry_ref[...] = carry_ref[...] + jnp.sum(c, axis=1, keepdims=True)
    cnt_ref[...] = carry_ref[...].astype(jnp.int32)


def _ranks(idx):
    N = idx.shape[1]
    T = T_RANK
    tri = jnp.asarray(np.arange(T)[:, None] < np.arange(T)[None, :], BF16)
    blk = pl.BlockSpec((TOP_K, T), lambda i: (0, i))
    return pl.pallas_call(
        _rank_kernel,
        grid=(N // T,),
        in_specs=[blk, pl.BlockSpec((T, T), lambda i: (0, 0))],
        out_specs=[blk, pl.BlockSpec((N_EXPERTS, 128), lambda i: (0, 0))],
        out_shape=[jax.ShapeDtypeStruct((TOP_K, N), jnp.int32),
                   jax.ShapeDtypeStruct((N_EXPERTS, 128), jnp.int32)],
        scratch_shapes=[pltpu.VMEM((N_EXPERTS, 128), F32)],
        compiler_params=_cparams("arbitrary"),
        name="expert_ranks",
    )(idx, tri)


def _dest_kernel(idx_ref, rank_ref, start_ref, dest_ref):
    T = idx_ref.shape[1]
    e_iota = lax.broadcasted_iota(jnp.int32, (N_EXPERTS, T), 0)
    start = jnp.tile(start_ref[...], (1, T // 128))
    rows = [jnp.sum(jnp.where(e_iota == idx_ref[k:k + 1, :], start, 0), axis=0, keepdims=True)
            for k in range(TOP_K)]
    dest_ref[...] = jnp.concatenate(rows, axis=0) + rank_ref[...]


def _dests(idx, rank, group_start):
    N = idx.shape[1]
    T = T_ROUTE
    blk = pl.BlockSpec((TOP_K, T), lambda i: (0, i))
    return pl.pallas_call(
        _dest_kernel,
        grid=(N // T,),
        in_specs=[blk, blk, pl.BlockSpec((N_EXPERTS, 128), lambda i: (0, 0))],
        out_specs=blk,
        out_shape=jax.ShapeDtypeStruct((TOP_K, N), jnp.int32),
        compiler_params=_cparams("parallel"),
        name="expert_dests",
    )(idx, rank, jnp.broadcast_to(group_start[:, None], (N_EXPERTS, 128)))


MOVE_UNROLL = 8
SLAB_ROWS = 8


def _for_rows(T, fn):
    def body(g, carry):
        for u in range(MOVE_UNROLL):
            for k in range(TOP_K):
                t = g * MOVE_UNROLL + u
                fn(t, k, k * T + t)
        return carry

    lax.fori_loop(0, T // MOVE_UNROLL, body, 0)


def _slab(ref_at, row8):
    return ref_at.at[pl.ds(pl.multiple_of(row8, SLAB_ROWS), SLAB_ROWS), :]


def _dispatch_kernel(dcur_ref, dprev_ref, x_ref, *refs, n_cast):
    w_refs, xs_ref, wb_refs, (pk_ref, sem) = refs[:n_cast], refs[n_cast], refs[n_cast + 1:-2], refs[-2:]
    for w_ref, wb_ref in zip(w_refs, wb_refs):
        wb_ref[...] = w_ref[...].astype(BF16)
    i = pl.program_id(0)
    n = pl.num_programs(0)
    T = x_ref.shape[0]
    slot = i % 2
    for c in range(SLAB_ROWS):
        pk_ref[slot, pl.ds(c, T, stride=SLAB_ROWS), :] = x_ref[:, c * 128:(c + 1) * 128]

    def row_copy(sl, dref, t, j):
        return pltpu.make_async_copy(_slab(pk_ref.at[sl], t * SLAB_ROWS), _slab(xs_ref, dref[j]), sem.at[sl])

    _for_rows(T, lambda t, k, j: row_copy(slot, dcur_ref, t, j).start(priority=k % 2))

    @pl.when(i > 0)
    def _():
        _for_rows(T, lambda t, k, j: row_copy(1 - slot, dprev_ref, t, j).wait())

    @pl.when(i == n - 1)
    def _():
        _for_rows(T, lambda t, k, j: row_copy(slot, dcur_ref, t, j).wait())


CAST_BLOCK_BYTES = 2 * 1024 * 1024


def _dispatch(x2, dest8, P, weights):
    N, D = x2.shape
    assert D == SLAB_ROWS * 128
    T = T_MOVE
    n = N // T
    flat = [w.reshape(-1, w.shape[-1]) for w in weights]
    fused = all(f.shape[0] % n == 0 and (f.shape[0] // n) % 16 == 0
                and (f.shape[0] // n) * f.shape[1] * 4 <= CAST_BLOCK_BYTES for f in flat)
    if not fused:
        flat = []
    w_specs = [pl.BlockSpec((f.shape[0] // n, f.shape[1]), lambda i: (i, 0)) for f in flat]
    smem = lambda f: pl.BlockSpec((TOP_K * T,), f, memory_space=pltpu.SMEM)
    outs = pl.pallas_call(
        functools.partial(_dispatch_kernel, n_cast=len(flat)),
        grid=(n,),
        in_specs=[smem(lambda i: (i,)), smem(lambda i: (jnp.maximum(i - 1, 0),)),
                  pl.BlockSpec((T, D), lambda i: (i, 0))] + w_specs,
        out_specs=[pl.BlockSpec(memory_space=pl.ANY)] + w_specs,
        out_shape=[jax.ShapeDtypeStruct((P * SLAB_ROWS, 128), F32)]
                  + [jax.ShapeDtypeStruct(f.shape, BF16) for f in flat],
        scratch_shapes=[pltpu.VMEM((2, T * SLAB_ROWS, 128), F32), pltpu.SemaphoreType.DMA((2,))],
        compiler_params=_cparams("arbitrary"),
        name="moe_dispatch",
    )(dest8, dest8, x2, *flat)
    if fused:
        return outs[0], [o.reshape(w.shape) for o, w in zip(outs[1:], weights)]
    return outs[0], [w.astype(BF16) for w in weights]


def _expert_kernel(be_ref, bv_ref, xs_ref, wg_ref, wu_ref, wd_ref, bg_ref, bu_ref, bd_ref, ys_ref):
    j = pl.program_id(0)
    valid = bv_ref[j]

    @pl.when(valid > 0)
    def _():
        bm = xs_ref.shape[0] // SLAB_ROWS
        row = lax.broadcasted_iota(jnp.int32, (bm, 128), 0)
        xb = jnp.concatenate(
            [jnp.where(row < valid, xs_ref[pl.ds(c, bm, stride=SLAB_ROWS), :], 0.0).astype(BF16)
             for c in range(SLAB_ROWS)], axis=1)

        def proj(w_ref, b_ref):
            return jnp.dot(xb, w_ref[...], preferred_element_type=F32) + b_ref[...]

        gate = jnp.minimum(proj(wg_ref, bg_ref), SWIGLU_LIMIT)
        up = jnp.clip(proj(wu_ref, bu_ref), -SWIGLU_LIMIT, SWIGLU_LIMIT)
        hmid = gate * (1.0 / (1.0 + jnp.exp(-SWIGLU_ALPHA * gate))) * (up + 1.0)
        out = jnp.dot(hmid.astype(BF16), wd_ref[...], preferred_element_type=F32) + bd_ref[...]
        for c in range(SLAB_ROWS):
            ys_ref[pl.ds(c, bm, stride=SLAB_ROWS), :] = out[:, c * 128:(c + 1) * 128]


def _experts(xs, block_e, block_valid, wg, wu, wd, bg, bu, bd):
    D = SLAB_ROWS * 128
    F = wg.shape[-1]
    wspec = lambda a, c: pl.BlockSpec((None, a, c), lambda j, be, bv: (be[j], 0, 0))
    slabs = pl.BlockSpec((BM * SLAB_ROWS, 128), lambda j, be, bv: (j, 0))
    grid_spec = pltpu.PrefetchScalarGridSpec(
        num_scalar_prefetch=2,
        grid=(xs.shape[0] // (BM * SLAB_ROWS),),
        in_specs=[slabs, wspec(D, F), wspec(D, F), wspec(F, D), wspec(1, F), wspec(1, F), wspec(1, D)],
        out_specs=slabs,
    )
    return pl.pallas_call(
        _expert_kernel,
        grid_spec=grid_spec,
        out_shape=jax.ShapeDtypeStruct(xs.shape, F32),
        compiler_params=_cparams("arbitrary"),
        name="moe_experts",
    )(block_e, block_valid, xs, wg, wu, wd, bg, bu, bd)


def _combine_kernel(dcur_ref, dnext_ref, x_ref, wt_ref, ys_ref, g_ref, b_ref, out_ref, buf_ref, sem):
    i = pl.program_id(0)
    n = pl.num_programs(0)
    T = x_ref.shape[0]
    slot = i % 2

    def row_copy(sl, dref, t, k, j):
        return pltpu.make_async_copy(_slab(ys_ref, dref[j]), _slab(buf_ref.at[sl], (k * T + t) * SLAB_ROWS),
                                     sem.at[sl])

    @pl.when(i == 0)
    def _():
        _for_rows(T, lambda t, k, j: row_copy(slot, dcur_ref, t, k, j).start(priority=k % 2))

    @pl.when(i + 1 < n)
    def _():
        _for_rows(T, lambda t, k, j: row_copy(1 - slot, dnext_ref, t, k, j).start(priority=k % 2))

    _for_rows(T, lambda t, k, j: row_copy(slot, dcur_ref, t, k, j).wait())

    wt = wt_ref[...]
    wb = [jnp.broadcast_to(wt[:, k:k + 1], (T, 128)) for k in range(TOP_K)]
    ys = []
    for c in range(SLAB_ROWS):
        chunk = lambda k: buf_ref[slot, pl.ds(k * T * SLAB_ROWS + c, T, stride=SLAB_ROWS), :]
        yc = wb[0] * chunk(0)
        for k in range(1, TOP_K):
            yc = yc + wb[k] * chunk(k)
        ys.append(yc)
    y = jnp.concatenate(ys, axis=1)
    out_ref[...] = _layer_norm(DEEPNORM_ALPHA * x_ref[...] + y, g_ref[...], b_ref[...])


def _combine(x2, dest_flat, wt_tok, ys, g, b):
    N, D = x2.shape
    T = T_COMBINE
    n = N // T
    vec = pl.BlockSpec((1, D), lambda i: (0, 0))
    smem = lambda f: pl.BlockSpec((TOP_K * T,), f, memory_space=pltpu.SMEM)
    return pl.pallas_call(
        _combine_kernel,
        grid=(n,),
        in_specs=[smem(lambda i: (i,)), smem(lambda i: (jnp.minimum(i + 1, n - 1),)),
                  pl.BlockSpec((T, D), lambda i: (i, 0)),
                  pl.BlockSpec((T, TOP_K), lambda i: (i, 0)),
                  pl.BlockSpec(memory_space=pl.ANY), vec, vec],
        out_specs=pl.BlockSpec((T, D), lambda i: (i, 0)),
        out_shape=jax.ShapeDtypeStruct((N, D), F32),
        scratch_shapes=[pltpu.VMEM((2, TOP_K * T * SLAB_ROWS, 128), F32), pltpu.SemaphoreType.DMA((2,))],
        compiler_params=_cparams("arbitrary"),
        name="moe_combine_ln3",
    )(dest_flat, dest_flat, x2, wt_tok, ys, g.reshape(1, D), b.reshape(1, D))


def _tile_major(a, T):
    K, N = a.shape
    return a.reshape(K, N // T, T).transpose(1, 0, 2).reshape(N * K)


def _moe(x2, router_w, router_b, w_gate, b_gate, w_up, b_up, w_down, b_down, ln_g, ln_b):
    N, D = x2.shape
    idx, wt = _router(x2, router_w, router_b)
    rank, cnt = _ranks(idx)
    counts = cnt[:, 0]
    blocks = (counts + BM - 1) // BM
    blk_end = jnp.cumsum(blocks)
    blk_start = blk_end - blocks
    dest = _dests(idx, rank, (blk_start * BM).astype(jnp.int32))
    n_blocks = -(-N * TOP_K // BM) + N_EXPERTS
    bi = jnp.arange(n_blocks, dtype=jnp.int32)
    be_raw = jnp.sum(bi[:, None] >= blk_end[None, :], axis=1).astype(jnp.int32)
    last_e = jnp.max(jnp.where(counts > 0, jnp.arange(N_EXPERTS), 0)).astype(jnp.int32)
    block_e = jnp.minimum(be_raw, last_e)
    in_use = bi < blk_end[-1]
    block_valid = jnp.where(in_use, jnp.clip(counts[block_e] - (bi - blk_start[block_e]) * BM, 0, BM), 0)
    dest8 = dest * SLAB_ROWS
    xs, (wg_bf, wu_bf, wd_bf) = _dispatch(x2, _tile_major(dest8, T_MOVE), n_blocks * BM, [w_gate, w_up, w_down])
    f3 = lambda t: t.astype(F32)[:, None, :]
    ys = _experts(xs, block_e, block_valid.astype(jnp.int32), wg_bf, wu_bf, wd_bf,
                  f3(b_gate), f3(b_up), f3(b_down))
    return _combine(x2, _tile_major(dest8, T_COMBINE), wt.T, ys, ln_g, ln_b)


def kernel(x, mem, w_in, ret_norm_g, w_out, ln1_g, ln1_b, mem_wq, mem_wk, mem_wv, mem_wo, ln2_g, ln2_b,
           router_w, router_b, w_gate, b_gate, w_up, b_up, w_down, b_down, ln3_g, ln3_b):
    B, S, D = x.shape
    assert [d for _, d in DILATED_PATTERNS] == [1, 4, 16] and all(w // d == BAND for w, d in DILATED_PATTERNS)
    for l in range(w_in.shape[0]):
        h, qkv4, qkv16 = _in_proj(x, w_in[l].astype(BF16))
        pats = [_band_attn(h, B, S, 1, N_IN // GROUP),
                _band_attn(qkv4, B, S // 4, 4, 3), _band_attn(qkv16, B, S // 16, 16, 3)]
        ret = _retention(h, ret_norm_g[l])
        x1 = _out_proj([p[0] for p in pats], [p[1] for p in pats], ret, x,
                       w_out[l].astype(BF16), ln1_g[l], ln1_b[l])
        kmem, vmem = _mem_kv(mem, mem_wk[l].astype(BF16), mem_wv[l].astype(BF16))
        x2 = _cross_attn(x1, kmem, vmem, mem_wq[l].astype(BF16), mem_wo[l].astype(BF16),
                         ln2_g[l], ln2_b[l])
        x = _moe(x2, router_w[l], router_b[l], w_gate[l], b_gate[l], w_up[l], b_up[l],
                 w_down[l], b_down[l], ln3_g[l], ln3_b[l]).reshape(B, S, D)
    return x
```

```python
import functools

import jax
import jax.numpy as jnp
import numpy as np
from jax import lax
from jax.experimental import pallas as pl
from jax.experimental.pallas import tpu as pltpu

F32 = jnp.float32
BF16 = jnp.bfloat16

HEAD_DIM = 64
N_HEADS_DIL = 8
N_HEADS_RET = 8
W_DIL = N_HEADS_DIL * HEAD_DIM
W_RET = N_HEADS_RET * HEAD_DIM
N_IN = 3 * W_DIL + 4 * W_RET
DILATED_PATTERNS = ((128, 1), (512, 4), (2048, 16))
BAND = 128
ROPE_THETA = 500000.0
ROT_DIM = HEAD_DIM // 4
RET_THETA = 10000.0
RET_CHUNK = 128
N_HEADS_MEM = 4
N_EXPERTS = 32
TOP_K = 4
SWIGLU_LIMIT = 7.0
SWIGLU_ALPHA = 1.702
LN_EPS = 1e-5
DEPTH = 1
DEEPNORM_ALPHA = (2 * DEPTH) ** 0.25

GROUP = 512
TM = 512
TQ = 512
T_ROUTE = 1024
T_RANK = 512
T_MOVE = 512
T_COMBINE = 256
BM = 512
CROSS_SPLIT = 2
VMEM_LIMIT = 56 * 1024 * 1024
NEG = -1e30


def _cparams(*sem):
    return pltpu.CompilerParams(dimension_semantics=sem, vmem_limit_bytes=VMEM_LIMIT)


def _nt_dot(a, b):
    return lax.dot_general(a, b, (((1,), (1,)), ((), ())), preferred_element_type=F32)


def _layer_norm(y, g, b):
    mu = jnp.mean(y, axis=-1, keepdims=True)
    d = y - mu
    var = jnp.mean(d * d, axis=-1, keepdims=True)
    return d * lax.rsqrt(var + LN_EPS) * g + b


def _in_proj_kernel(x_ref, w_ref, ca_ref, la_ref, ha_ref, cr_ref, lr_ref, hr_ref, o_ref, o4_ref, o16_ref,
                    acc_ref):
    xb = x_ref[...].astype(BF16)
    rep = GROUP // 128
    tm = x_ref.shape[0]
    qkv = 3 * W_DIL

    def rot(acc, c_ref, lo_ref, hi_ref, half):
        c = jnp.tile(c_ref[...], (1, rep))
        lo = jnp.tile(lo_ref[...], (1, rep))
        hi = jnp.tile(hi_ref[...], (1, rep))
        up = pltpu.roll(acc, GROUP - half, axis=1)
        dn = pltpu.roll(acc, half, axis=1)
        return acc * c + up * lo + dn * hi

    for g in range(N_IN // GROUP):
        acc = jnp.dot(xb, w_ref[:, g * GROUP:(g + 1) * GROUP], preferred_element_type=F32)
        if g == 0:
            acc = rot(acc, ca_ref, la_ref, ha_ref, ROT_DIM // 2) * (HEAD_DIM ** -0.5)
        elif g == 1:
            acc = rot(acc, ca_ref, la_ref, ha_ref, ROT_DIM // 2)
        elif g == 3:
            acc = rot(acc, cr_ref, lr_ref, hr_ref, HEAD_DIM // 2)
        elif g == 4:
            acc = rot(acc, cr_ref, lr_ref, hr_ref, HEAD_DIM // 2) * (HEAD_DIM ** -0.5)
        o_ref[:, g * GROUP:(g + 1) * GROUP] = acc.astype(BF16)
        if g < 3:
            for c in range(rep):
                acc_ref[c] = acc[:, c * 128:(c + 1) * 128]
            for d, od_ref in ((4, o4_ref), (16, o16_ref)):
                for r in range(d):
                    for c in range(rep):
                        c0 = r * qkv + g * GROUP + c * 128
                        od_ref[:, c0:c0 + 128] = acc_ref[c, pl.ds(r, tm // d, stride=d), :].astype(BF16)


def _rotary_tables(S, theta, rot_dim):
    half = rot_dim // 2
    f32 = np.float32
    inv = f32(1.0) / (f32(theta) ** (np.arange(half, dtype=f32) / f32(half)))
    ang = np.arange(S, dtype=f32)[:, None] * inv[None, :]
    cos, sin = np.cos(ang).astype(f32), np.sin(ang).astype(f32)
    pad = HEAD_DIM - rot_dim
    c = np.concatenate([cos, cos, np.ones((S, pad), f32)], axis=1)
    lo = np.concatenate([-sin, np.zeros((S, half + pad), f32)], axis=1)
    hi = np.concatenate([np.zeros((S, half), f32), sin, np.zeros((S, pad), f32)], axis=1)
    two = lambda t: np.concatenate([t, t], axis=1)
    return two(c), two(lo), two(hi)


def _in_proj(x3, w_bf):
    B, S, D = x3.shape
    tabs = _rotary_tables(S, ROPE_THETA, ROT_DIM) + _rotary_tables(S, RET_THETA, HEAD_DIM)
    tab_spec = pl.BlockSpec((TM, 128), lambda b, i: (i, 0))
    qkv = 3 * W_DIL
    cls_spec = lambda d: pl.BlockSpec((None, TM // d, d * qkv), lambda b, i: (b, i, 0))
    cls_shape = lambda d: jax.ShapeDtypeStruct((B, S // d, d * qkv), BF16)
    return pl.pallas_call(
        _in_proj_kernel,
        grid=(B, S // TM),
        in_specs=[pl.BlockSpec((None, TM, D), lambda b, i: (b, i, 0)),
                  pl.BlockSpec((D, N_IN), lambda b, i: (0, 0))] + [tab_spec] * 6,
        out_specs=[pl.BlockSpec((None, TM, N_IN), lambda b, i: (b, i, 0)), cls_spec(4), cls_spec(16)],
        out_shape=[jax.ShapeDtypeStruct((B, S, N_IN), BF16), cls_shape(4), cls_shape(16)],
        scratch_shapes=[pltpu.VMEM((GROUP // 128, TM, 128), F32)],
        compiler_params=_cparams("parallel", "parallel"),
        name="in_proj",
    )(x3, w_bf, *tabs)


def _band_attn_kernel(q_ref, kc_ref, kp_ref, vc_ref, vp_ref, o_ref, l_ref,
                      k_all, v_all, s_scr, p_scr, r_scr):
    j = pl.program_id(2)
    tq = q_ref.shape[0]
    nsub = tq // BAND
    pair = 2 * HEAD_DIM
    n_pairs = N_HEADS_DIL // 2
    k_all[0:BAND, :] = kp_ref[...]
    k_all[BAND:, :] = kc_ref[...]
    v_all[0:BAND, :] = vp_ref[...]
    v_all[BAND:, :] = vc_ref[...]
    qi = lax.broadcasted_iota(jnp.int32, (BAND, 2 * BAND), 0)
    kj = lax.broadcasted_iota(jnp.int32, (BAND, 2 * BAND), 1)
    dist = qi + BAND - kj
    band = (dist >= 0) & (dist <= BAND)
    bias = jnp.where(band, 0.0, NEG)
    bias_first = jnp.where(band & ((kj >= BAND) | (j > 0)), 0.0, NEG)
    low = lax.broadcasted_iota(jnp.int32, (BAND, pair), 1) < HEAD_DIM
    zero = jnp.zeros((BAND, pair), BF16)

    for pr in range(n_pairs):
        cs = slice(pr * pair, (pr + 1) * pair)
        for n in range(nsub):
            q = q_ref[n * BAND:(n + 1) * BAND, cs]
            kk = k_all[n * BAND:(n + 2) * BAND, cs]
            b = bias_first if n == 0 else bias
            i = pr * nsub + n
            s_scr[2 * i] = _nt_dot(jnp.where(low, q, zero), kk) + b
            s_scr[2 * i + 1] = _nt_dot(jnp.where(low, zero, q), kk) + b

    for pr in range(n_pairs):
        cs = slice(pr * pair, (pr + 1) * pair)
        for n in range(nsub):
            i = pr * nsub + n
            stats = []
            for a in range(2):
                s = s_scr[2 * i + a]
                m = jnp.max(s, axis=-1, keepdims=True)
                p = jnp.exp(s - m)
                l = jnp.sum(p, axis=-1, keepdims=True)
                p_scr[2 * i + a] = p.astype(BF16)
                stats.append((m, l))
            (m0, l0), (m1, l1) = stats
            r_scr[i] = jnp.where(low, 1.0 / l0, 1.0 / l1)
            l_ref[n * BAND:(n + 1) * BAND, cs] = jnp.where(low, m0 + jnp.log(l0), m1 + jnp.log(l1))

    for pr in range(n_pairs):
        cs = slice(pr * pair, (pr + 1) * pair)
        for n in range(nsub):
            i = pr * nsub + n
            vv = v_all[n * BAND:(n + 2) * BAND, cs]
            o0 = jnp.dot(p_scr[2 * i], vv, preferred_element_type=F32)
            o1 = jnp.dot(p_scr[2 * i + 1], vv, preferred_element_type=F32)
            o_ref[n * BAND:(n + 1) * BAND, cs] = jnp.where(low, o0, o1) * r_scr[i]


def _band_attn(src, B, L, d, gpr):
    tq = min(TQ, L)
    assert L % tq == 0
    sub = tq // BAND
    cur = lambda g: pl.BlockSpec((None, tq, GROUP), lambda b, r, j: (b, j, r * gpr + g))
    prev = lambda g: pl.BlockSpec((None, BAND, GROUP),
                                  lambda b, r, j: (b, jnp.maximum(j * sub - 1, 0), r * gpr + g))
    out_spec = pl.BlockSpec((None, tq, W_DIL), lambda b, r, j: (b, j, r))
    shp = jax.ShapeDtypeStruct((B, L, d * W_DIL), F32)
    n_blk = (N_HEADS_DIL // 2) * sub
    return pl.pallas_call(
        _band_attn_kernel,
        grid=(B, d, L // tq),
        in_specs=[cur(0), cur(1), prev(1), cur(2), prev(2)],
        out_specs=[out_spec, out_spec],
        out_shape=[shp, shp],
        scratch_shapes=[pltpu.VMEM((BAND + tq, GROUP), BF16), pltpu.VMEM((BAND + tq, GROUP), BF16),
                        pltpu.VMEM((2 * n_blk, BAND, 2 * BAND), F32),
                        pltpu.VMEM((2 * n_blk, BAND, 2 * BAND), BF16),
                        pltpu.VMEM((n_blk, BAND, 2 * HEAD_DIM), F32)],
        compiler_params=_cparams("parallel", "parallel", "parallel"),
        name=f"band_attn_d{d}",
    )(src, src, src, src, src)


RET_UNROLL = 16


def _retention_kernel(q_ref, k_ref, v_ref, g_ref, gain_ref, din_ref, kd_ref, qd_ref, cd_ref,
                      o_ref, kv_ref, st_ref):
    C = RET_CHUNK
    n_iter = q_ref.shape[0] // (C * RET_UNROLL)
    pair = 2 * HEAD_DIM
    low = lax.broadcasted_iota(jnp.int32, (C, pair), 1) < HEAD_DIM

    def rows(i, u):
        return pl.ds(pl.multiple_of(i * (C * RET_UNROLL), C * RET_UNROLL) + u * C, C)

    def kv_body(i, carry):
        for u in range(RET_UNROLL):
            kd = (k_ref[rows(i, u), :].astype(F32) * kd_ref[...]).astype(BF16)
            kv_ref[i * RET_UNROLL + u] = lax.dot_general(kd, v_ref[rows(i, u), :], (((0,), (0,)), ((), ())),
                                                         preferred_element_type=F32)
        return carry

    lax.fori_loop(0, n_iter, kv_body, 0)

    def scan_body(c, state):
        st_ref[c] = state.astype(BF16)
        return state * cd_ref[...] + kv_ref[c]

    lax.fori_loop(0, n_iter * RET_UNROLL, scan_body, jnp.zeros((pair, pair), F32))

    def out_body(i, carry):
        for u in range(RET_UNROLL):
            q = q_ref[rows(i, u), :]
            k = k_ref[rows(i, u), :]
            v = v_ref[rows(i, u), :]
            prev = st_ref[i * RET_UNROLL + u]
            ys = []
            for a in range(2):
                qa = jnp.where(low, q, jnp.zeros_like(q)) if a == 0 else jnp.where(low, jnp.zeros_like(q), q)
                s = _nt_dot(qa, k) * din_ref[a]
                inner = jnp.dot(s.astype(BF16), v, preferred_element_type=F32)
                qd = (qa.astype(F32) * qd_ref[...]).astype(BF16)
                ys.append(inner + jnp.dot(qd, prev, preferred_element_type=F32))
            y = jnp.where(low, ys[0], ys[1])
            inv = 1.0 / HEAD_DIM
            half_sum = lambda t: jnp.where(low, jnp.sum(jnp.where(low, t, 0.0), axis=-1, keepdims=True),
                                           jnp.sum(jnp.where(low, 0.0, t), axis=-1, keepdims=True))
            dlt = y - half_sum(y) * inv
            var = half_sum(dlt * dlt) * inv
            yn = dlt * lax.rsqrt(var + LN_EPS) * gain_ref[...]
            gr = g_ref[rows(i, u), :].astype(F32)
            o_ref[rows(i, u), :] = (yn * (gr / (1.0 + jnp.exp(-gr)))).astype(BF16)
        return carry

    lax.fori_loop(0, n_iter, out_body, 0)


def _retention_tables():
    H, C = N_HEADS_RET, RET_CHUNK
    f32 = np.float32
    log_g = np.log1p(-(f32(2.0) ** (f32(-5.0) - np.arange(H, dtype=f32)))).astype(f32)
    idx = np.arange(C, dtype=f32)
    diff = idx[:, None] - idx[None, :]
    inner_decay = (np.exp(log_g[:, None, None] * np.maximum(diff, f32(0.0))) * (diff >= 0)).astype(f32)
    k_decay = np.exp(log_g[:, None] * (f32(C - 1) - idx)).astype(f32)
    q_decay = np.exp(log_g[:, None] * (idx + f32(1.0))).astype(f32)
    chunk_decay = np.exp(log_g * f32(C)).astype(f32)
    lanes = lambda t: np.repeat(t.reshape(H // 2, 2, C).transpose(0, 2, 1), HEAD_DIM, axis=2)
    cd = np.ascontiguousarray(np.broadcast_to(
        np.repeat(chunk_decay.reshape(H // 2, 2), HEAD_DIM, axis=1)[:, :, None],
        (H // 2, 2 * HEAD_DIM, 2 * HEAD_DIM)))
    return inner_decay, lanes(k_decay), lanes(q_decay), cd


def _retention(h, gain):
    B, S, _ = h.shape
    lanes = 2 * HEAD_DIM
    col0 = 3 * W_DIL // lanes
    per = W_RET // lanes
    col = lambda g: pl.BlockSpec((None, S, lanes), lambda b, p: (b, 0, col0 + g * per + p))
    din, kd, qd, cd = _retention_tables()
    C = RET_CHUNK
    assert S % (C * RET_UNROLL) == 0
    tab = lambda a: pl.BlockSpec((None, a, lanes), lambda b, p: (p, 0, 0))
    return pl.pallas_call(
        _retention_kernel,
        grid=(B, per),
        in_specs=[col(0), col(1), col(2), col(3),
                  pl.BlockSpec((1, lanes), lambda b, p: (0, p)),
                  pl.BlockSpec((2, C, C), lambda b, p: (p, 0, 0)), tab(C), tab(C), tab(lanes)],
        out_specs=pl.BlockSpec((None, S, lanes), lambda b, p: (b, 0, p)),
        out_shape=jax.ShapeDtypeStruct((B, S, W_RET), BF16),
        scratch_shapes=[pltpu.VMEM((S // C, lanes, lanes), F32), pltpu.VMEM((S // C, lanes, lanes), BF16)],
        compiler_params=_cparams("parallel", "parallel"),
        name="retention",
    )(h, h, h, h, gain.reshape(1, W_RET).astype(F32), din, kd, qd, cd)


def _out_proj_kernel(o1, o4, o16, l1, l4, l16, ret_ref, x_ref, w_ref, g_ref, b_ref, out_ref,
                     so4, so16, sl4, sl16):
    tm = x_ref.shape[0]
    n_slab = W_DIL // 128
    for d, pairs in ((4, ((o4, so4), (l4, sl4))), (16, ((o16, so16), (l16, sl16)))):
        for src, dst in pairs:
            for r in range(d):
                for c in range(n_slab):
                    c0 = r * W_DIL + c * 128
                    dst[c, pl.ds(r, tm // d, stride=d), :] = src[:, c0:c0 + 128]
    atts = []
    for c in range(n_slab):
        cs = slice(c * 128, (c + 1) * 128)
        a1, a4, a16 = l1[:, cs], sl4[c], sl16[c]
        m = jnp.maximum(jnp.maximum(a1, a4), a16)
        e1, e4, e16 = jnp.exp(a1 - m), jnp.exp(a4 - m), jnp.exp(a16 - m)
        att = (e1 * o1[:, cs] + e4 * so4[c] + e16 * so16[c]) / (e1 + e4 + e16)
        atts.append(att.astype(BF16))
    acc = jnp.dot(jnp.concatenate(atts, axis=1), w_ref[:W_DIL, :], preferred_element_type=F32)
    acc += jnp.dot(ret_ref[...], w_ref[W_DIL:, :], preferred_element_type=F32)
    out_ref[...] = _layer_norm(DEEPNORM_ALPHA * x_ref[...] + acc, g_ref[...], b_ref[...])


def _out_proj(os_, ls_, ret, x3, w_bf, g, b):
    B, S, D = x3.shape
    cls = lambda d: pl.BlockSpec((None, TM // d, d * W_DIL), lambda b_, i: (b_, i, 0))
    half = cls(1)
    full = pl.BlockSpec((None, TM, D), lambda b_, i: (b_, i, 0))
    vec = pl.BlockSpec((1, D), lambda b_, i: (0, 0))
    return pl.pallas_call(
        _out_proj_kernel,
        grid=(B, S // TM),
        in_specs=[half, cls(4), cls(16), half, cls(4), cls(16), half, full,
                  pl.BlockSpec((D, D), lambda b_, i: (0, 0)), vec, vec],
        out_specs=full,
        out_shape=jax.ShapeDtypeStruct((B, S, D), F32),
        scratch_shapes=[pltpu.VMEM((W_DIL // 128, TM, 128), F32)] * 4,
        compiler_params=_cparams("parallel", "parallel"),
        name="out_proj_ln1",
    )(*os_, *ls_, ret, x3, w_bf, g.reshape(1, D), b.reshape(1, D))


def _mem_kv_kernel(m_ref, wk_ref, wv_ref, k_ref, v_ref):
    mb = m_ref[...].astype(BF16)
    k_ref[...] = jnp.dot(mb, wk_ref[...], preferred_element_type=F32).astype(BF16)
    v_ref[...] = jnp.dot(mb, wv_ref[...], preferred_element_type=F32).astype(BF16)


def _mem_kv(mem, wk_bf, wv_bf):
    B, M, D = mem.shape
    blk = pl.BlockSpec((None, M, D), lambda b: (b, 0, 0))
    wsp = pl.BlockSpec((D, D), lambda b: (0, 0))
    shp = jax.ShapeDtypeStruct((B, M, D), BF16)
    return pl.pallas_call(
        _mem_kv_kernel, grid=(B,), in_specs=[blk, wsp, wsp], out_specs=[blk, blk], out_shape=[shp, shp],
        compiler_params=_cparams("parallel"), name="mem_kv",
    )(mem, wk_bf, wv_bf)


def _cross_attn_kernel(x_ref, k_ref, v_ref, wq_ref, wo_ref, g_ref, b_ref, out_ref):
    hd = x_ref.shape[-1] // N_HEADS_MEM
    rows = x_ref.shape[0] // CROSS_SPLIT
    for part in range(CROSS_SPLIT):
        rs = slice(part * rows, (part + 1) * rows)
        x = x_ref[rs, :]
        q = (jnp.dot(x.astype(BF16), wq_ref[...], preferred_element_type=F32) * (hd ** -0.5)).astype(BF16)
        outs = []
        for hh in range(N_HEADS_MEM):
            cs = slice(hh * hd, (hh + 1) * hd)
            s = _nt_dot(q[:, cs], k_ref[:, cs])
            m = jnp.max(s, axis=-1, keepdims=True)
            p = jnp.exp(s - m)
            l = jnp.sum(p, axis=-1, keepdims=True)
            outs.append((jnp.dot(p.astype(BF16), v_ref[:, cs], preferred_element_type=F32) / l).astype(BF16))
        o = jnp.concatenate(outs, axis=-1)
        c = jnp.dot(o, wo_ref[...], preferred_element_type=F32)
        out_ref[rs, :] = _layer_norm(DEEPNORM_ALPHA * x + c, g_ref[...], b_ref[...])


def _cross_attn(x3, kmem, vmem, wq_bf, wo_bf, g, b):
    B, S, D = x3.shape
    M = kmem.shape[1]
    xs = pl.BlockSpec((None, TM, D), lambda b_, i: (b_, i, 0))
    ms = pl.BlockSpec((None, M, D), lambda b_, i: (b_, 0, 0))
    ws = pl.BlockSpec((D, D), lambda b_, i: (0, 0))
    vs = pl.BlockSpec((1, D), lambda b_, i: (0, 0))
    return pl.pallas_call(
        _cross_attn_kernel,
        grid=(B, S // TM),
        in_specs=[xs, ms, ms, ws, ws, vs, vs],
        out_specs=xs,
        out_shape=jax.ShapeDtypeStruct((B, S, D), F32),
        compiler_params=_cparams("parallel", "parallel"),
        name="cross_attn_ln2",
    )(x3, kmem, vmem, wq_bf, wo_bf, g.reshape(1, D), b.reshape(1, D)).reshape(B * S, D)


def _split_bf16(t):
    hi = t.astype(BF16)
    return hi, (t - hi.astype(F32)).astype(BF16)


def _router_kernel(x_ref, w_ref, b_ref, idx_ref, wt_ref):
    x_hi, x_lo = _split_bf16(x_ref[...])
    by_hi = _nt_dot(w_ref[...], x_hi)
    logits = (by_hi[:N_EXPERTS] + (by_hi[N_EXPERTS:] + _nt_dot(w_ref[:N_EXPERTS, :], x_lo))) + b_ref[:, :1]
    e_iota = lax.broadcasted_iota(jnp.int32, logits.shape, 0)
    cur = logits
    vals, idxs = [], []
    for _ in range(TOP_K):
        m = jnp.max(cur, axis=0, keepdims=True)
        idx = jnp.min(jnp.where(cur == m, e_iota, N_EXPERTS), axis=0, keepdims=True)
        cur = jnp.where(e_iota == idx, -jnp.inf, cur)
        vals.append(m)
        idxs.append(idx)
    es = [jnp.exp(v - vals[0]) for v in vals]
    tot = es[0] + es[1] + es[2] + es[3]
    idx_ref[...] = jnp.concatenate(idxs, axis=0)
    wt_ref[...] = jnp.concatenate([e / tot for e in es], axis=0)


def _router(x2, router_w, router_b):
    N, D = x2.shape
    out = pl.BlockSpec((TOP_K, T_ROUTE), lambda i: (0, i))
    return pl.pallas_call(
        _router_kernel,
        grid=(N // T_ROUTE,),
        in_specs=[pl.BlockSpec((T_ROUTE, D), lambda i: (i, 0)),
                  pl.BlockSpec((2 * N_EXPERTS, D), lambda i: (0, 0)),
                  pl.BlockSpec((N_EXPERTS, 128), lambda i: (0, 0))],
        out_specs=[out, out],
        out_shape=[jax.ShapeDtypeStruct((TOP_K, N), jnp.int32), jax.ShapeDtypeStruct((TOP_K, N), F32)],
        compiler_params=_cparams("parallel"),
        name="router_top4",
    )(x2, jnp.concatenate(_split_bf16(router_w.T.astype(F32)), axis=0),
      jnp.broadcast_to(router_b.astype(F32)[:, None], (N_EXPERTS, 128)))


def _rank_kernel(idx_ref, tri_ref, rank_ref, cnt_ref, carry_ref):
    @pl.when(pl.program_id(0) == 0)
    def _():
        carry_ref[...] = jnp.zeros_like(carry_ref)

    T = idx_ref.shape[1]
    e_iota = lax.broadcasted_iota(jnp.int32, (N_EXPERTS, T), 0)
    hot = [e_iota == idx_ref[k:k + 1, :] for k in range(TOP_K)]
    c = sum(h.astype(F32) for h in hot)
    before = jnp.dot(c.astype(BF16), tri_ref[...], preferred_element_type=F32) + carry_ref[:, :1]
    rank_ref[...] = jnp.concatenate(
        [jnp.sum(jnp.where(h, before, 0.0), axis=0, keepdims=True) for h in hot], axis=0).astype(jnp.int32)
    carry_ref[...] = carry_ref[...] + jnp.sum(c, axis=1, keepdims=True)
    cnt_ref[...] = carry_ref[...].astype(jnp.int32)


def _ranks(idx):
    N = idx.shape[1]
    T = T_RANK
    tri = jnp.asarray(np.arange(T)[:, None] < np.arange(T)[None, :], BF16)
    blk = pl.BlockSpec((TOP_K, T), lambda i: (0, i))
    return pl.pallas_call(
        _rank_kernel,
        grid=(N // T,),
        in_specs=[blk, pl.BlockSpec((T, T), lambda i: (0, 0))],
        out_specs=[blk, pl.BlockSpec((N_EXPERTS, 128), lambda i: (0, 0))],
        out_shape=[jax.ShapeDtypeStruct((TOP_K, N), jnp.int32),
                   jax.ShapeDtypeStruct((N_EXPERTS, 128), jnp.int32)],
        scratch_shapes=[pltpu.VMEM((N_EXPERTS, 128), F32)],
        compiler_params=_cparams("arbitrary"),
        name="expert_ranks",
    )(idx, tri)


def _dest_kernel(idx_ref, rank_ref, start_ref, dest_ref):
    T = idx_ref.shape[1]
    e_iota = lax.broadcasted_iota(jnp.int32, (N_EXPERTS, T), 0)
    start = jnp.tile(start_ref[...], (1, T // 128))
    rows = [jnp.sum(jnp.where(e_iota == idx_ref[k:k + 1, :], start, 0), axis=0, keepdims=True)
            for k in range(TOP_K)]
    dest_ref[...] = jnp.concatenate(rows, axis=0) + rank_ref[...]


def _dests(idx, rank, group_start):
    N = idx.shape[1]
    T = T_ROUTE
    blk = pl.BlockSpec((TOP_K, T), lambda i: (0, i))
    return pl.pallas_call(
        _dest_kernel,
        grid=(N // T,),
        in_specs=[blk, blk, pl.BlockSpec((N_EXPERTS, 128), lambda i: (0, 0))],
        out_specs=blk,
        out_shape=jax.ShapeDtypeStruct((TOP_K, N), jnp.int32),
        compiler_params=_cparams("parallel"),
        name="expert_dests",
    )(idx, rank, jnp.broadcast_to(group_start[:, None], (N_EXPERTS, 128)))


MOVE_UNROLL = 8
SLAB_ROWS = 8


def _for_rows(T, fn):
    def body(g, carry):
        for u in range(MOVE_UNROLL):
            for k in range(TOP_K):
                t = g * MOVE_UNROLL + u
                fn(t, k, k * T + t)
        return carry

    lax.fori_loop(0, T // MOVE_UNROLL, body, 0)


def _slab(ref_at, row8):
    return ref_at.at[pl.ds(pl.multiple_of(row8, SLAB_ROWS), SLAB_ROWS), :]


def _dispatch_kernel(dcur_ref, dprev_ref, x_ref, *refs, n_cast):
    w_refs, xs_ref, wb_refs, (pk_ref, sem) = refs[:n_cast], refs[n_cast], refs[n_cast + 1:-2], refs[-2:]
    for w_ref, wb_ref in zip(w_refs, wb_refs):
        wb_ref[...] = w_ref[...].astype(BF16)
    i = pl.program_id(0)
    n = pl.num_programs(0)
    T = x_ref.shape[0]
    slot = i % 2
    for c in range(SLAB_ROWS):
        pk_ref[slot, pl.ds(c, T, stride=SLAB_ROWS), :] = x_ref[:, c * 128:(c + 1) * 128]

    def row_copy(sl, dref, t, j):
        return pltpu.make_async_copy(_slab(pk_ref.at[sl], t * SLAB_ROWS), _slab(xs_ref, dref[j]), sem.at[sl])

    _for_rows(T, lambda t, k, j: row_copy(slot, dcur_ref, t, j).start(priority=k % 2))

    @pl.when(i > 0)
    def _():
        _for_rows(T, lambda t, k, j: row_copy(1 - slot, dprev_ref, t, j).wait())

    @pl.when(i == n - 1)
    def _():
        _for_rows(T, lambda t, k, j: row_copy(slot, dcur_ref, t, j).wait())


CAST_BLOCK_BYTES = 2 * 1024 * 1024


def _dispatch(x2, dest8, P, weights):
    N, D = x2.shape
    assert D == SLAB_ROWS * 128
    T = T_MOVE
    n = N // T
    flat = [w.reshape(-1, w.shape[-1]) for w in weights]
    fused = all(f.shape[0] % n == 0 and (f.shape[0] // n) % 16 == 0
                and (f.shape[0] // n) * f.shape[1] * 4 <= CAST_BLOCK_BYTES for f in flat)
    if not fused:
        flat = []
    w_specs = [pl.BlockSpec((f.shape[0] // n, f.shape[1]), lambda i: (i, 0)) for f in flat]
    smem = lambda f: pl.BlockSpec((TOP_K * T,), f, memory_space=pltpu.SMEM)
    outs = pl.pallas_call(
        functools.partial(_dispatch_kernel, n_cast=len(flat)),
        grid=(n,),
        in_specs=[smem(lambda i: (i,)), smem(lambda i: (jnp.maximum(i - 1, 0),)),
                  pl.BlockSpec((T, D), lambda i: (i, 0))] + w_specs,
        out_specs=[pl.BlockSpec(memory_space=pl.ANY)] + w_specs,
        out_shape=[jax.ShapeDtypeStruct((P * SLAB_ROWS, 128), F32)]
                  + [jax.ShapeDtypeStruct(f.shape, BF16) for f in flat],
        scratch_shapes=[pltpu.VMEM((2, T * SLAB_ROWS, 128), F32), pltpu.SemaphoreType.DMA((2,))],
        compiler_params=_cparams("arbitrary"),
        name="moe_dispatch",
    )(dest8, dest8, x2, *flat)
    if fused:
        return outs[0], [o.reshape(w.shape) for o, w in zip(outs[1:], weights)]
    return outs[0], [w.astype(BF16) for w in weights]


def _expert_kernel(be_ref, bv_ref, xs_ref, wg_ref, wu_ref, wd_ref, bg_ref, bu_ref, bd_ref, ys_ref):
    j = pl.program_id(0)
    valid = bv_ref[j]

    @pl.when(valid > 0)
    def _():
        bm = xs_ref.shape[0] // SLAB_ROWS
        row = lax.broadcasted_iota(jnp.int32, (bm, 128), 0)
        xb = jnp.concatenate(
            [jnp.where(row < valid, xs_ref[pl.ds(c, bm, stride=SLAB_ROWS), :], 0.0).astype(BF16)
             for c in range(SLAB_ROWS)], axis=1)

        def proj(w_ref, b_ref):
            return jnp.dot(xb, w_ref[...], preferred_element_type=F32) + b_ref[...]

        gate = jnp.minimum(proj(wg_ref, bg_ref), SWIGLU_LIMIT)
        up = jnp.clip(proj(wu_ref, bu_ref), -SWIGLU_LIMIT, SWIGLU_LIMIT)
        hmid = gate * (1.0 / (1.0 + jnp.exp(-SWIGLU_ALPHA * gate))) * (up + 1.0)
        out = jnp.dot(hmid.astype(BF16), wd_ref[...], preferred_element_type=F32) + bd_ref[...]
        for c in range(SLAB_ROWS):
            ys_ref[pl.ds(c, bm, stride=SLAB_ROWS), :] = out[:, c * 128:(c + 1) * 128]


def _experts(xs, block_e, block_valid, wg, wu, wd, bg, bu, bd):
    D = SLAB_ROWS * 128
    F = wg.shape[-1]
    wspec = lambda a, c: pl.BlockSpec((None, a, c), lambda j, be, bv: (be[j], 0, 0))
    slabs = pl.BlockSpec((BM * SLAB_ROWS, 128), lambda j, be, bv: (j, 0))
    grid_spec = pltpu.PrefetchScalarGridSpec(
        num_scalar_prefetch=2,
        grid=(xs.shape[0] // (BM * SLAB_ROWS),),
        in_specs=[slabs, wspec(D, F), wspec(D, F), wspec(F, D), wspec(1, F), wspec(1, F), wspec(1, D)],
        out_specs=slabs,
    )
    return pl.pallas_call(
        _expert_kernel,
        grid_spec=grid_spec,
        out_shape=jax.ShapeDtypeStruct(xs.shape, F32),
        compiler_params=_cparams("arbitrary"),
        name="moe_experts",
    )(block_e, block_valid, xs, wg, wu, wd, bg, bu, bd)


def _combine_kernel(dcur_ref, dnext_ref, x_ref, wt_ref, ys_ref, g_ref, b_ref, out_ref, buf_ref, sem):
    i = pl.program_id(0)
    n = pl.num_programs(0)
    T = x_ref.shape[0]
    slot = i % 2

    def row_copy(sl, dref, t, k, j):
        return pltpu.make_async_copy(_slab(ys_ref, dref[j]), _slab(buf_ref.at[sl], (k * T + t) * SLAB_ROWS),
                                     sem.at[sl])

    @pl.when(i == 0)
    def _():
        _for_rows(T, lambda t, k, j: row_copy(slot, dcur_ref, t, k, j).start(priority=k % 2))

    @pl.when(i + 1 < n)
    def _():
        _for_rows(T, lambda t, k, j: row_copy(1 - slot, dnext_ref, t, k, j).start(priority=k % 2))

    _for_rows(T, lambda t, k, j: row_copy(slot, dcur_ref, t, k, j).wait())

    wt = wt_ref[...]
    wb = [jnp.broadcast_to(wt[:, k:k + 1], (T, 128)) for k in range(TOP_K)]
    ys = []
    for c in range(SLAB_ROWS):
        chunk = lambda k: buf_ref[slot, pl.ds(k * T * SLAB_ROWS + c, T, stride=SLAB_ROWS), :]
        yc = wb[0] * chunk(0)
        for k in range(1, TOP_K):
            yc = yc + wb[k] * chunk(k)
        ys.append(yc)
    y = jnp.concatenate(ys, axis=1)
    out_ref[...] = _layer_norm(DEEPNORM_ALPHA * x_ref[...] + y, g_ref[...], b_ref[...])


def _combine(x2, dest_flat, wt_tok, ys, g, b):
    N, D = x2.shape
    T = T_COMBINE
    n = N // T
    vec = pl.BlockSpec((1, D), lambda i: (0, 0))
    smem = lambda f: pl.BlockSpec((TOP_K * T,), f, memory_space=pltpu.SMEM)
    return pl.pallas_call(
        _combine_kernel,
        grid=(n,),
        in_specs=[smem(lambda i: (i,)), smem(lambda i: (jnp.minimum(i + 1, n - 1),)),
                  pl.BlockSpec((T, D), lambda i: (i, 0)),
                  pl.BlockSpec((T, TOP_K), lambda i: (i, 0)),
                  pl.BlockSpec(memory_space=pl.ANY), vec, vec],
        out_specs=pl.BlockSpec((T, D), lambda i: (i, 0)),
        out_shape=jax.ShapeDtypeStruct((N, D), F32),
        scratch_shapes=[pltpu.VMEM((2, TOP_K * T * SLAB_ROWS, 128), F32), pltpu.SemaphoreType.DMA((2,))],
        compiler_params=_cparams("arbitrary"),
        name="moe_combine_ln3",
    )(dest_flat, dest_flat, x2, wt_tok, ys, g.reshape(1, D), b.reshape(1, D))


def _tile_major(a, T):
    K, N = a.shape
    return a.reshape(K, N // T, T).transpose(1, 0, 2).reshape(N * K)


def _moe(x2, router_w, router_b, w_gate, b_gate, w_up, b_up, w_down, b_down, ln_g, ln_b):
    N, D = x2.shape
    idx, wt = _router(x2, router_w, router_b)
    rank, cnt = _ranks(idx)
    counts = cnt[:, 0]
    blocks = (counts + BM - 1) // BM
    blk_end = jnp.cumsum(blocks)
    blk_start = blk_end - blocks
    dest = _dests(idx, rank, (blk_start * BM).astype(jnp.int32))
    n_blocks = -(-N * TOP_K // BM) + N_EXPERTS
    bi = jnp.arange(n_blocks, dtype=jnp.int32)
    be_raw = jnp.sum(bi[:, None] >= blk_end[None, :], axis=1).astype(jnp.int32)
    last_e = jnp.max(jnp.where(counts > 0, jnp.arange(N_EXPERTS), 0)).astype(jnp.int32)
    block_e = jnp.minimum(be_raw, last_e)
    in_use = bi < blk_end[-1]
    block_valid = jnp.where(in_use, jnp.clip(counts[block_e] - (bi - blk_start[block_e]) * BM, 0, BM), 0)
    dest8 = dest * SLAB_ROWS
    xs, (wg_bf, wu_bf, wd_bf) = _dispatch(x2, _tile_major(dest8, T_MOVE), n_blocks * BM, [w_gate, w_up, w_down])
    f3 = lambda t: t.astype(F32)[:, None, :]
    ys = _experts(xs, block_e, block_valid.astype(jnp.int32), wg_bf, wu_bf, wd_bf,
                  f3(b_gate), f3(b_up), f3(b_down))
    return _combine(x2, _tile_major(dest8, T_COMBINE), wt.T, ys, ln_g, ln_b)


def kernel(x, mem, w_in, ret_norm_g, w_out, ln1_g, ln1_b, mem_wq, mem_wk, mem_wv, mem_wo, ln2_g, ln2_b,
           router_w, router_b, w_gate, b_gate, w_up, b_up, w_down, b_down, ln3_g, ln3_b):
    B, S, D = x.shape
    assert [d for _, d in DILATED_PATTERNS] == [1, 4, 16] and all(w // d == BAND for w, d in DILATED_PATTERNS)
    for l in range(w_in.shape[0]):
        h, qkv4, qkv16 = _in_proj(x, w_in[l].astype(BF16))
        pats = [_band_attn(h, B, S, 1, N_IN // GROUP),
                _band_attn(qkv4, B, S // 4, 4, 3), _band_attn(qkv16, B, S // 16, 16, 3)]
        ret = _retention(h, ret_norm_g[l])
        x1 = _out_proj([p[0] for p in pats], [p[1] for p in pats], ret, x,
                       w_out[l].astype(BF16), ln1_g[l], ln1_b[l])
        kmem, vmem = _mem_kv(mem, mem_wk[l].astype(BF16), mem_wv[l].astype(BF16))
        x2 = _cross_attn(x1, kmem, vmem, mem_wq[l].astype(BF16), mem_wo[l].astype(BF16),
                         ln2_g[l], ln2_b[l])
        x = _moe(x2, router_w[l], router_b[l], w_gate[l], b_gate[l], w_up[l], b_up[l],
                 w_down[l], b_down[l], ln3_g[l], ln3_b[l]).reshape(B, S, D)
    return x
```

```python
import functools

import jax
import jax.numpy as jnp
import numpy as np
from jax import lax
from jax.experimental import pallas as pl
from jax.experimental.pallas import tpu as pltpu

F32 = jnp.float32
BF16 = jnp.bfloat16

HEAD_DIM = 64
N_HEADS_DIL = 8
N_HEADS_RET = 8
W_DIL = N_HEADS_DIL * HEAD_DIM
W_RET = N_HEADS_RET * HEAD_DIM
N_IN = 3 * W_DIL + 4 * W_RET
DILATED_PATTERNS = ((128, 1), (512, 4), (2048, 16))
BAND = 128
ROPE_THETA = 500000.0
ROT_DIM = HEAD_DIM // 4
RET_THETA = 10000.0
RET_CHUNK = 128
N_HEADS_MEM = 4
N_EXPERTS = 32
TOP_K = 4
SWIGLU_LIMIT = 7.0
SWIGLU_ALPHA = 1.702
LN_EPS = 1e-5
DEPTH = 1
DEEPNORM_ALPHA = (2 * DEPTH) ** 0.25

GROUP = 512
TM = 512
TQ = 512
T_ROUTE = 1024
T_RANK = 512
T_MOVE = 512
T_COMBINE = 256
BM = 1024
VMEM_LIMIT = 56 * 1024 * 1024
NEG = -1e30


def _cparams(*sem):
    return pltpu.CompilerParams(dimension_semantics=sem, vmem_limit_bytes=VMEM_LIMIT)


def _nt_dot(a, b):
    return lax.dot_general(a, b, (((1,), (1,)), ((), ())), preferred_element_type=F32)


def _layer_norm(y, g, b):
    mu = jnp.mean(y, axis=-1, keepdims=True)
    d = y - mu
    var = jnp.mean(d * d, axis=-1, keepdims=True)
    return d * lax.rsqrt(var + LN_EPS) * g + b


def _in_proj_kernel(x_ref, w_ref, ca_ref, la_ref, ha_ref, cr_ref, lr_ref, hr_ref, o_ref, o4_ref, o16_ref,
                    acc_ref):
    xb = x_ref[...].astype(BF16)
    rep = GROUP // 128
    tm = x_ref.shape[0]
    qkv = 3 * W_DIL

    def rot(acc, c_ref, lo_ref, hi_ref, half):
        c = jnp.tile(c_ref[...], (1, rep))
        lo = jnp.tile(lo_ref[...], (1, rep))
        hi = jnp.tile(hi_ref[...], (1, rep))
        up = pltpu.roll(acc, GROUP - half, axis=1)
        dn = pltpu.roll(acc, half, axis=1)
        return acc * c + up * lo + dn * hi

    for g in range(N_IN // GROUP):
        acc = jnp.dot(xb, w_ref[:, g * GROUP:(g + 1) * GROUP], preferred_element_type=F32)
        if g == 0:
            acc = rot(acc, ca_ref, la_ref, ha_ref, ROT_DIM // 2) * (HEAD_DIM ** -0.5)
        elif g == 1:
            acc = rot(acc, ca_ref, la_ref, ha_ref, ROT_DIM // 2)
        elif g == 3:
            acc = rot(acc, cr_ref, lr_ref, hr_ref, HEAD_DIM // 2)
        elif g == 4:
            acc = rot(acc, cr_ref, lr_ref, hr_ref, HEAD_DIM // 2) * (HEAD_DIM ** -0.5)
        o_ref[:, g * GROUP:(g + 1) * GROUP] = acc.astype(BF16)
        if g < 3:
            for c in range(rep):
                acc_ref[c] = acc[:, c * 128:(c + 1) * 128]
            for d, od_ref in ((4, o4_ref), (16, o16_ref)):
                for r in range(d):
                    for c in range(rep):
                        c0 = r * qkv + g * GROUP + c * 128
                        od_ref[:, c0:c0 + 128] = acc_ref[c, pl.ds(r, tm // d, stride=d), :].astype(BF16)


def _rotary_tables(S, theta, rot_dim):
    half = rot_dim // 2
    f32 = np.float32
    inv = f32(1.0) / (f32(theta) ** (np.arange(half, dtype=f32) / f32(half)))
    ang = np.arange(S, dtype=f32)[:, None] * inv[None, :]
    cos, sin = np.cos(ang).astype(f32), np.sin(ang).astype(f32)
    pad = HEAD_DIM - rot_dim
    c = np.concatenate([cos, cos, np.ones((S, pad), f32)], axis=1)
    lo = np.concatenate([-sin, np.zeros((S, half + pad), f32)], axis=1)
    hi = np.concatenate([np.zeros((S, half), f32), sin, np.zeros((S, pad), f32)], axis=1)
    two = lambda t: np.concatenate([t, t], axis=1)
    return two(c), two(lo), two(hi)


def _in_proj(x3, w_bf):
    B, S, D = x3.shape
    tabs = _rotary_tables(S, ROPE_THETA, ROT_DIM) + _rotary_tables(S, RET_THETA, HEAD_DIM)
    tab_spec = pl.BlockSpec((TM, 128), lambda b, i: (i, 0))
    qkv = 3 * W_DIL
    cls_spec = lambda d: pl.BlockSpec((None, TM // d, d * qkv), lambda b, i: (b, i, 0))
    cls_shape = lambda d: jax.ShapeDtypeStruct((B, S // d, d * qkv), BF16)
    return pl.pallas_call(
        _in_proj_kernel,
        grid=(B, S // TM),
        in_specs=[pl.BlockSpec((None, TM, D), lambda b, i: (b, i, 0)),
                  pl.BlockSpec((D, N_IN), lambda b, i: (0, 0))] + [tab_spec] * 6,
        out_specs=[pl.BlockSpec((None, TM, N_IN), lambda b, i: (b, i, 0)), cls_spec(4), cls_spec(16)],
        out_shape=[jax.ShapeDtypeStruct((B, S, N_IN), BF16), cls_shape(4), cls_shape(16)],
        scratch_shapes=[pltpu.VMEM((GROUP // 128, TM, 128), F32)],
        compiler_params=_cparams("parallel", "parallel"),
        name="in_proj",
    )(x3, w_bf, *tabs)


def _band_attn_kernel(q_ref, kc_ref, kp_ref, vc_ref, vp_ref, o_ref, l_ref,
                      k_all, v_all, s_scr, p_scr, r_scr):
    j = pl.program_id(2)
    tq = q_ref.shape[0]
    nsub = tq // BAND
    pair = 2 * HEAD_DIM
    n_pairs = N_HEADS_DIL // 2
    k_all[0:BAND, :] = kp_ref[...]
    k_all[BAND:, :] = kc_ref[...]
    v_all[0:BAND, :] = vp_ref[...]
    v_all[BAND:, :] = vc_ref[...]
    qi = lax.broadcasted_iota(jnp.int32, (BAND, 2 * BAND), 0)
    kj = lax.broadcasted_iota(jnp.int32, (BAND, 2 * BAND), 1)
    dist = qi + BAND - kj
    band = (dist >= 0) & (dist <= BAND)
    bias = jnp.where(band, 0.0, NEG)
    bias_first = jnp.where(band & ((kj >= BAND) | (j > 0)), 0.0, NEG)
    low = lax.broadcasted_iota(jnp.int32, (BAND, pair), 1) < HEAD_DIM
    zero = jnp.zeros((BAND, pair), BF16)

    for pr in range(n_pairs):
        cs = slice(pr * pair, (pr + 1) * pair)
        for n in range(nsub):
            q = q_ref[n * BAND:(n + 1) * BAND, cs]
            kk = k_all[n * BAND:(n + 2) * BAND, cs]
            b = bias_first if n == 0 else bias
            i = pr * nsub + n
            s_scr[2 * i] = _nt_dot(jnp.where(low, q, zero), kk) + b
            s_scr[2 * i + 1] = _nt_dot(jnp.where(low, zero, q), kk) + b

    for pr in range(n_pairs):
        cs = slice(pr * pair, (pr + 1) * pair)
        for n in range(nsub):
            i = pr * nsub + n
            stats = []
            for a in range(2):
                s = s_scr[2 * i + a]
                m = jnp.max(s, axis=-1, keepdims=True)
                p = jnp.exp(s - m)
                l = jnp.sum(p, axis=-1, keepdims=True)
                p_scr[2 * i + a] = p.astype(BF16)
                stats.append((m, l))
            (m0, l0), (m1, l1) = stats
            r_scr[i] = jnp.where(low, 1.0 / l0, 1.0 / l1)
            l_ref[n * BAND:(n + 1) * BAND, cs] = jnp.where(low, m0 + jnp.log(l0), m1 + jnp.log(l1))

    for pr in range(n_pairs):
        cs = slice(pr * pair, (pr + 1) * pair)
        for n in range(nsub):
            i = pr * nsub + n
            vv = v_all[n * BAND:(n + 2) * BAND, cs]
            o0 = jnp.dot(p_scr[2 * i], vv, preferred_element_type=F32)
            o1 = jnp.dot(p_scr[2 * i + 1], vv, preferred_element_type=F32)
            o_ref[n * BAND:(n + 1) * BAND, cs] = jnp.where(low, o0, o1) * r_scr[i]


def _band_attn(src, B, L, d, gpr):
    tq = min(TQ, L)
    assert L % tq == 0
    sub = tq // BAND
    cur = lambda g: pl.BlockSpec((None, tq, GROUP), lambda b, r, j: (b, j, r * gpr + g))
    prev = lambda g: pl.BlockSpec((None, BAND, GROUP),
                                  lambda b, r, j: (b, jnp.maximum(j * sub - 1, 0), r * gpr + g))
    out_spec = pl.BlockSpec((None, tq, W_DIL), lambda b, r, j: (b, j, r))
    shp = jax.ShapeDtypeStruct((B, L, d * W_DIL), F32)
    n_blk = (N_HEADS_DIL // 2) * sub
    return pl.pallas_call(
        _band_attn_kernel,
        grid=(B, d, L // tq),
        in_specs=[cur(0), cur(1), prev(1), cur(2), prev(2)],
        out_specs=[out_spec, out_spec],
        out_shape=[shp, shp],
        scratch_shapes=[pltpu.VMEM((BAND + tq, GROUP), BF16), pltpu.VMEM((BAND + tq, GROUP), BF16),
                        pltpu.VMEM((2 * n_blk, BAND, 2 * BAND), F32),
                        pltpu.VMEM((2 * n_blk, BAND, 2 * BAND), BF16),
                        pltpu.VMEM((n_blk, BAND, 2 * HEAD_DIM), F32)],
        compiler_params=_cparams("parallel", "parallel", "parallel"),
        name=f"band_attn_d{d}",
    )(src, src, src, src, src)


RET_UNROLL = 16


def _retention_kernel(q_ref, k_ref, v_ref, g_ref, gain_ref, din_ref, kd_ref, qd_ref, cd_ref,
                      o_ref, kv_ref, st_ref):
    C = RET_CHUNK
    n_iter = q_ref.shape[0] // (C * RET_UNROLL)
    pair = 2 * HEAD_DIM
    low = lax.broadcasted_iota(jnp.int32, (C, pair), 1) < HEAD_DIM

    def rows(i, u):
        return pl.ds(pl.multiple_of(i * (C * RET_UNROLL), C * RET_UNROLL) + u * C, C)

    def kv_body(i, carry):
        for u in range(RET_UNROLL):
            kd = (k_ref[rows(i, u), :].astype(F32) * kd_ref[...]).astype(BF16)
            kv_ref[i * RET_UNROLL + u] = lax.dot_general(kd, v_ref[rows(i, u), :], (((0,), (0,)), ((), ())),
                                                         preferred_element_type=F32)
        return carry

    lax.fori_loop(0, n_iter, kv_body, 0)

    def scan_body(c, state):
        st_ref[c] = state.astype(BF16)
        return state * cd_ref[...] + kv_ref[c]

    lax.fori_loop(0, n_iter * RET_UNROLL, scan_body, jnp.zeros((pair, pair), F32))

    def out_body(i, carry):
        for u in range(RET_UNROLL):
            q = q_ref[rows(i, u), :]
            k = k_ref[rows(i, u), :]
            v = v_ref[rows(i, u), :]
            prev = st_ref[i * RET_UNROLL + u]
            ys = []
            for a in range(2):
                qa = jnp.where(low, q, jnp.zeros_like(q)) if a == 0 else jnp.where(low, jnp.zeros_like(q), q)
                s = _nt_dot(qa, k) * din_ref[a]
                inner = jnp.dot(s.astype(BF16), v, preferred_element_type=F32)
                qd = (qa.astype(F32) * qd_ref[...]).astype(BF16)
                ys.append(inner + jnp.dot(qd, prev, preferred_element_type=F32))
            y = jnp.where(low, ys[0], ys[1])
            inv = 1.0 / HEAD_DIM
            half_sum = lambda t: jnp.where(low, jnp.sum(jnp.where(low, t, 0.0), axis=-1, keepdims=True),
                                           jnp.sum(jnp.where(low, 0.0, t), axis=-1, keepdims=True))
            dlt = y - half_sum(y) * inv
            var = half_sum(dlt * dlt) * inv
            yn = dlt * lax.rsqrt(var + LN_EPS) * gain_ref[...]
            gr = g_ref[rows(i, u), :].astype(F32)
            o_ref[rows(i, u), :] = (yn * (gr / (1.0 + jnp.exp(-gr)))).astype(BF16)
        return carry

    lax.fori_loop(0, n_iter, out_body, 0)


def _retention_tables():
    H, C = N_HEADS_RET, RET_CHUNK
    f32 = np.float32
    log_g = np.log1p(-(f32(2.0) ** (f32(-5.0) - np.arange(H, dtype=f32)))).astype(f32)
    idx = np.arange(C, dtype=f32)
    diff = idx[:, None] - idx[None, :]
    inner_decay = (np.exp(log_g[:, None, None] * np.maximum(diff, f32(0.0))) * (diff >= 0)).astype(f32)
    k_decay = np.exp(log_g[:, None] * (f32(C - 1) - idx)).astype(f32)
    q_decay = np.exp(log_g[:, None] * (idx + f32(1.0))).astype(f32)
    chunk_decay = np.exp(log_g * f32(C)).astype(f32)
    lanes = lambda t: np.repeat(t.reshape(H // 2, 2, C).transpose(0, 2, 1), HEAD_DIM, axis=2)
    cd = np.ascontiguousarray(np.broadcast_to(
        np.repeat(chunk_decay.reshape(H // 2, 2), HEAD_DIM, axis=1)[:, :, None],
        (H // 2, 2 * HEAD_DIM, 2 * HEAD_DIM)))
    return inner_decay, lanes(k_decay), lanes(q_decay), cd


def _retention(h, gain):
    B, S, _ = h.shape
    lanes = 2 * HEAD_DIM
    col0 = 3 * W_DIL // lanes
    per = W_RET // lanes
    col = lambda g: pl.BlockSpec((None, S, lanes), lambda b, p: (b, 0, col0 + g * per + p))
    din, kd, qd, cd = _retention_tables()
    C = RET_CHUNK
    assert S % (C * RET_UNROLL) == 0
    tab = lambda a: pl.BlockSpec((None, a, lanes), lambda b, p: (p, 0, 0))
    return pl.pallas_call(
        _retention_kernel,
        grid=(B, per),
        in_specs=[col(0), col(1), col(2), col(3),
                  pl.BlockSpec((1, lanes), lambda b, p: (0, p)),
                  pl.BlockSpec((2, C, C), lambda b, p: (p, 0, 0)), tab(C), tab(C), tab(lanes)],
        out_specs=pl.BlockSpec((None, S, lanes), lambda b, p: (b, 0, p)),
        out_shape=jax.ShapeDtypeStruct((B, S, W_RET), BF16),
        scratch_shapes=[pltpu.VMEM((S // C, lanes, lanes), F32), pltpu.VMEM((S // C, lanes, lanes), BF16)],
        compiler_params=_cparams("parallel", "parallel"),
        name="retention",
    )(h, h, h, h, gain.reshape(1, W_RET).astype(F32), din, kd, qd, cd)


def _out_proj_kernel(o1, o4, o16, l1, l4, l16, ret_ref, x_ref, w_ref, g_ref, b_ref, out_ref,
                     so4, so16, sl4, sl16):
    tm = x_ref.shape[0]
    n_slab = W_DIL // 128
    for d, pairs in ((4, ((o4, so4), (l4, sl4))), (16, ((o16, so16), (l16, sl16)))):
        for src, dst in pairs:
            for r in range(d):
                for c in range(n_slab):
                    c0 = r * W_DIL + c * 128
                    dst[c, pl.ds(r, tm // d, stride=d), :] = src[:, c0:c0 + 128]
    atts = []
    for c in range(n_slab):
        cs = slice(c * 128, (c + 1) * 128)
        a1, a4, a16 = l1[:, cs], sl4[c], sl16[c]
        m = jnp.maximum(jnp.maximum(a1, a4), a16)
        e1, e4, e16 = jnp.exp(a1 - m), jnp.exp(a4 - m), jnp.exp(a16 - m)
        att = (e1 * o1[:, cs] + e4 * so4[c] + e16 * so16[c]) / (e1 + e4 + e16)
        atts.append(att.astype(BF16))
    acc = jnp.dot(jnp.concatenate(atts, axis=1), w_ref[:W_DIL, :], preferred_element_type=F32)
    acc += jnp.dot(ret_ref[...], w_ref[W_DIL:, :], preferred_element_type=F32)
    out_ref[...] = _layer_norm(DEEPNORM_ALPHA * x_ref[...] + acc, g_ref[...], b_ref[...])


def _out_proj(os_, ls_, ret, x3, w_bf, g, b):
    B, S, D = x3.shape
    cls = lambda d: pl.BlockSpec((None, TM // d, d * W_DIL), lambda b_, i: (b_, i, 0))
    half = cls(1)
    full = pl.BlockSpec((None, TM, D), lambda b_, i: (b_, i, 0))
    vec = pl.BlockSpec((1, D), lambda b_, i: (0, 0))
    return pl.pallas_call(
        _out_proj_kernel,
        grid=(B, S // TM),
        in_specs=[half, cls(4), cls(16), half, cls(4), cls(16), half, full,
                  pl.BlockSpec((D, D), lambda b_, i: (0, 0)), vec, vec],
        out_specs=full,
        out_shape=jax.ShapeDtypeStruct((B, S, D), F32),
        scratch_shapes=[pltpu.VMEM((W_DIL // 128, TM, 128), F32)] * 4,
        compiler_params=_cparams("parallel", "parallel"),
        name="out_proj_ln1",
    )(*os_, *ls_, ret, x3, w_bf, g.reshape(1, D), b.reshape(1, D))


def _mem_kv_kernel(m_ref, wk_ref, wv_ref, k_ref, v_ref):
    mb = m_ref[...].astype(BF16)
    k_ref[...] = jnp.dot(mb, wk_ref[...], preferred_element_type=F32).astype(BF16)
    v_ref[...] = jnp.dot(mb, wv_ref[...], preferred_element_type=F32).astype(BF16)


def _mem_kv(mem, wk_bf, wv_bf):
    B, M, D = mem.shape
    blk = pl.BlockSpec((None, M, D), lambda b: (b, 0, 0))
    wsp = pl.BlockSpec((D, D), lambda b: (0, 0))
    shp = jax.ShapeDtypeStruct((B, M, D), BF16)
    return pl.pallas_call(
        _mem_kv_kernel, grid=(B,), in_specs=[blk, wsp, wsp], out_specs=[blk, blk], out_shape=[shp, shp],
        compiler_params=_cparams("parallel"), name="mem_kv",
    )(mem, wk_bf, wv_bf)


def _cross_attn_kernel(x_ref, k_ref, v_ref, wq_ref, wo_ref, g_ref, b_ref, out_ref):
    x = x_ref[...]
    hd = x.shape[-1] // N_HEADS_MEM
    q = (jnp.dot(x.astype(BF16), wq_ref[...], preferred_element_type=F32) * (hd ** -0.5)).astype(BF16)
    outs = []
    for hh in range(N_HEADS_MEM):
        cs = slice(hh * hd, (hh + 1) * hd)
        s = _nt_dot(q[:, cs], k_ref[:, cs])
        m = jnp.max(s, axis=-1, keepdims=True)
        p = jnp.exp(s - m)
        l = jnp.sum(p, axis=-1, keepdims=True)
        outs.append((jnp.dot(p.astype(BF16), v_ref[:, cs], preferred_element_type=F32) / l).astype(BF16))
    o = jnp.concatenate(outs, axis=-1)
    c = jnp.dot(o, wo_ref[...], preferred_element_type=F32)
    out_ref[...] = _layer_norm(DEEPNORM_ALPHA * x + c, g_ref[...], b_ref[...])


def _cross_attn(x3, kmem, vmem, wq_bf, wo_bf, g, b):
    B, S, D = x3.shape
    M = kmem.shape[1]
    xs = pl.BlockSpec((None, TM, D), lambda b_, i: (b_, i, 0))
    ms = pl.BlockSpec((None, M, D), lambda b_, i: (b_, 0, 0))
    ws = pl.BlockSpec((D, D), lambda b_, i: (0, 0))
    vs = pl.BlockSpec((1, D), lambda b_, i: (0, 0))
    return pl.pallas_call(
        _cross_attn_kernel,
        grid=(B, S // TM),
        in_specs=[xs, ms, ms, ws, ws, vs, vs],
        out_specs=xs,
        out_shape=jax.ShapeDtypeStruct((B, S, D), F32),
        compiler_params=_cparams("parallel", "parallel"),
        name="cross_attn_ln2",
    )(x3, kmem, vmem, wq_bf, wo_bf, g.reshape(1, D), b.reshape(1, D)).reshape(B * S, D)


def _split_bf16(t):
    hi = t.astype(BF16)
    return hi, (t - hi.astype(F32)).astype(BF16)


def _router_kernel(x_ref, w_ref, b_ref, idx_ref, wt_ref):
    x_hi, x_lo = _split_bf16(x_ref[...])
    by_hi = _nt_dot(w_ref[...], x_hi)
    logits = (by_hi[:N_EXPERTS] + (by_hi[N_EXPERTS:] + _nt_dot(w_ref[:N_EXPERTS, :], x_lo))) + b_ref[:, :1]
    e_iota = lax.broadcasted_iota(jnp.int32, logits.shape, 0)
    cur = logits
    vals, idxs = [], []
    for _ in range(TOP_K):
        m = jnp.max(cur, axis=0, keepdims=True)
        idx = jnp.min(jnp.where(cur == m, e_iota, N_EXPERTS), axis=0, keepdims=True)
        cur = jnp.where(e_iota == idx, -jnp.inf, cur)
        vals.append(m)
        idxs.append(idx)
    es = [jnp.exp(v - vals[0]) for v in vals]
    tot = es[0] + es[1] + es[2] + es[3]
    idx_ref[...] = jnp.concatenate(idxs, axis=0)
    wt_ref[...] = jnp.concatenate([e / tot for e in es], axis=0)


def _router(x2, router_w, router_b):
    N, D = x2.shape
    out = pl.BlockSpec((TOP_K, T_ROUTE), lambda i: (0, i))
    return pl.pallas_call(
        _router_kernel,
        grid=(N // T_ROUTE,),
        in_specs=[pl.BlockSpec((T_ROUTE, D), lambda i: (i, 0)),
                  pl.BlockSpec((2 * N_EXPERTS, D), lambda i: (0, 0)),
                  pl.BlockSpec((N_EXPERTS, 128), lambda i: (0, 0))],
        out_specs=[out, out],
        out_shape=[jax.ShapeDtypeStruct((TOP_K, N), jnp.int32), jax.ShapeDtypeStruct((TOP_K, N), F32)],
        compiler_params=_cparams("parallel"),
        name="router_top4",
    )(x2, jnp.concatenate(_split_bf16(router_w.T.astype(F32)), axis=0),
      jnp.broadcast_to(router_b.astype(F32)[:, None], (N_EXPERTS, 128)))


def _rank_kernel(idx_ref, tri_ref, rank_ref, cnt_ref, carry_ref):
    @pl.when(pl.program_id(0) == 0)
    def _():
        carry_ref[...] = jnp.zeros_like(carry_ref)

    T = idx_ref.shape[1]
    e_iota = lax.broadcasted_iota(jnp.int32, (N_EXPERTS, T), 0)
    hot = [e_iota == idx_ref[k:k + 1, :] for k in range(TOP_K)]
    c = sum(h.astype(F32) for h in hot)
    before = jnp.dot(c.astype(BF16), tri_ref[...], preferred_element_type=F32) + carry_ref[:, :1]
    rank_ref[...] = jnp.concatenate(
        [jnp.sum(jnp.where(h, before, 0.0), axis=0, keepdims=True) for h in hot], axis=0).astype(jnp.int32)
    carry_ref[...] = carry_ref[...] + jnp.sum(c, axis=1, keepdims=True)
    cnt_ref[...] = carry_ref[...].astype(jnp.int32)


def _ranks(idx):
    N = idx.shape[1]
    T = T_RANK
    tri = jnp.asarray(np.arange(T)[:, None] < np.arange(T)[None, :], BF16)
    blk = pl.BlockSpec((TOP_K, T), lambda i: (0, i))
    return pl.pallas_call(
        _rank_kernel,
        grid=(N // T,),
        in_specs=[blk, pl.BlockSpec((T, T), lambda i: (0, 0))],
        out_specs=[blk, pl.BlockSpec((N_EXPERTS, 128), lambda i: (0, 0))],
        out_shape=[jax.ShapeDtypeStruct((TOP_K, N), jnp.int32),
                   jax.ShapeDtypeStruct((N_EXPERTS, 128), jnp.int32)],
        scratch_shapes=[pltpu.VMEM((N_EXPERTS, 128), F32)],
        compiler_params=_cparams("arbitrary"),
        name="expert_ranks",
    )(idx, tri)


def _dest_kernel(idx_ref, rank_ref, start_ref, dest_ref):
    T = idx_ref.shape[1]
    e_iota = lax.broadcasted_iota(jnp.int32, (N_EXPERTS, T), 0)
    start = jnp.tile(start_ref[...], (1, T // 128))
    rows = [jnp.sum(jnp.where(e_iota == idx_ref[k:k + 1, :], start, 0), axis=0, keepdims=True)
            for k in range(TOP_K)]
    dest_ref[...] = jnp.concatenate(rows, axis=0) + rank_ref[...]


def _dests(idx, rank, group_start):
    N = idx.shape[1]
    T = T_ROUTE
    blk = pl.BlockSpec((TOP_K, T), lambda i: (0, i))
    return pl.pallas_call(
        _dest_kernel,
        grid=(N // T,),
        in_specs=[blk, blk, pl.BlockSpec((N_EXPERTS, 128), lambda i: (0, 0))],
        out_specs=blk,
        out_shape=jax.ShapeDtypeStruct((TOP_K, N), jnp.int32),
        compiler_params=_cparams("parallel"),
        name="expert_dests",
    )(idx, rank, jnp.broadcast_to(group_start[:, None], (N_EXPERTS, 128)))


MOVE_UNROLL = 8
SLAB_ROWS = 8


def _for_rows(T, fn):
    def body(g, carry):
        for u in range(MOVE_UNROLL):
            for k in range(TOP_K):
                t = g * MOVE_UNROLL + u
                fn(t, k, k * T + t)
        return carry

    lax.fori_loop(0, T // MOVE_UNROLL, body, 0)


def _slab(ref_at, row8):
    return ref_at.at[pl.ds(pl.multiple_of(row8, SLAB_ROWS), SLAB_ROWS), :]


def _dispatch_kernel(dcur_ref, dprev_ref, x_ref, *refs, n_cast):
    w_refs, xs_ref, wb_refs, (pk_ref, sem) = refs[:n_cast], refs[n_cast], refs[n_cast + 1:-2], refs[-2:]
    for w_ref, wb_ref in zip(w_refs, wb_refs):
        wb_ref[...] = w_ref[...].astype(BF16)
    i = pl.program_id(0)
    n = pl.num_programs(0)
    T = x_ref.shape[0]
    slot = i % 2
    for c in range(SLAB_ROWS):
        pk_ref[slot, pl.ds(c, T, stride=SLAB_ROWS), :] = x_ref[:, c * 128:(c + 1) * 128]

    def row_copy(sl, dref, t, j):
        return pltpu.make_async_copy(_slab(pk_ref.at[sl], t * SLAB_ROWS), _slab(xs_ref, dref[j]), sem.at[sl])

    _for_rows(T, lambda t, k, j: row_copy(slot, dcur_ref, t, j).start(priority=k % 2))

    @pl.when(i > 0)
    def _():
        _for_rows(T, lambda t, k, j: row_copy(1 - slot, dprev_ref, t, j).wait())

    @pl.when(i == n - 1)
    def _():
        _for_rows(T, lambda t, k, j: row_copy(slot, dcur_ref, t, j).wait())


CAST_BLOCK_BYTES = 2 * 1024 * 1024


def _dispatch(x2, dest8, P, weights):
    N, D = x2.shape
    assert D == SLAB_ROWS * 128
    T = T_MOVE
    n = N // T
    flat = [w.reshape(-1, w.shape[-1]) for w in weights]
    fused = all(f.shape[0] % n == 0 and (f.shape[0] // n) % 16 == 0
                and (f.shape[0] // n) * f.shape[1] * 4 <= CAST_BLOCK_BYTES for f in flat)
    if not fused:
        flat = []
    w_specs = [pl.BlockSpec((f.shape[0] // n, f.shape[1]), lambda i: (i, 0)) for f in flat]
    smem = lambda f: pl.BlockSpec((TOP_K * T,), f, memory_space=pltpu.SMEM)
    outs = pl.pallas_call(
        functools.partial(_dispatch_kernel, n_cast=len(flat)),
        grid=(n,),
        in_specs=[smem(lambda i: (i,)), smem(lambda i: (jnp.maximum(i - 1, 0),)),
                  pl.BlockSpec((T, D), lambda i: (i, 0))] + w_specs,
        out_specs=[pl.BlockSpec(memory_space=pl.ANY)] + w_specs,
        out_shape=[jax.ShapeDtypeStruct((P * SLAB_ROWS, 128), F32)]
                  + [jax.ShapeDtypeStruct(f.shape, BF16) for f in flat],
        scratch_shapes=[pltpu.VMEM((2, T * SLAB_ROWS, 128), F32), pltpu.SemaphoreType.DMA((2,))],
        compiler_params=_cparams("arbitrary"),
        name="moe_dispatch",
    )(dest8, dest8, x2, *flat)
    if fused:
        return outs[0], [o.reshape(w.shape) for o, w in zip(outs[1:], weights)]
    return outs[0], [w.astype(BF16) for w in weights]


def _expert_kernel(be_ref, bv_ref, xs_ref, wg_ref, wu_ref, wd_ref, bg_ref, bu_ref, bd_ref, ys_ref):
    j = pl.program_id(0)
    valid = bv_ref[j]

    @pl.when(valid > 0)
    def _():
        bm = xs_ref.shape[0] // SLAB_ROWS
        row = lax.broadcasted_iota(jnp.int32, (bm, 128), 0)
        xb = jnp.concatenate(
            [jnp.where(row < valid, xs_ref[pl.ds(c, bm, stride=SLAB_ROWS), :], 0.0).astype(BF16)
             for c in range(SLAB_ROWS)], axis=1)

        def proj(w_ref, b_ref):
            return jnp.dot(xb, w_ref[...], preferred_element_type=F32) + b_ref[...]

        gate = jnp.minimum(proj(wg_ref, bg_ref), SWIGLU_LIMIT)
        up = jnp.clip(proj(wu_ref, bu_ref), -SWIGLU_LIMIT, SWIGLU_LIMIT)
        hmid = gate * (1.0 / (1.0 + jnp.exp(-SWIGLU_ALPHA * gate))) * (up + 1.0)
        out = jnp.dot(hmid.astype(BF16), wd_ref[...], preferred_element_type=F32) + bd_ref[...]
        for c in range(SLAB_ROWS):
            ys_ref[pl.ds(c, bm, stride=SLAB_ROWS), :] = out[:, c * 128:(c + 1) * 128]


def _experts(xs, block_e, block_valid, wg, wu, wd, bg, bu, bd):
    D = SLAB_ROWS * 128
    F = wg.shape[-1]
    wspec = lambda a, c: pl.BlockSpec((None, a, c), lambda j, be, bv: (be[j], 0, 0))
    slabs = pl.BlockSpec((BM * SLAB_ROWS, 128), lambda j, be, bv: (j, 0))
    grid_spec = pltpu.PrefetchScalarGridSpec(
        num_scalar_prefetch=2,
        grid=(xs.shape[0] // (BM * SLAB_ROWS),),
        in_specs=[slabs, wspec(D, F), wspec(D, F), wspec(F, D), wspec(1, F), wspec(1, F), wspec(1, D)],
        out_specs=slabs,
    )
    return pl.pallas_call(
        _expert_kernel,
        grid_spec=grid_spec,
        out_shape=jax.ShapeDtypeStruct(xs.shape, F32),
        compiler_params=_cparams("arbitrary"),
        name="moe_experts",
    )(block_e, block_valid, xs, wg, wu, wd, bg, bu, bd)


def _combine_kernel(dcur_ref, dnext_ref, x_ref, wt_ref, ys_ref, g_ref, b_ref, out_ref, buf_ref, sem):
    i = pl.program_id(0)
    n = pl.num_programs(0)
    T = x_ref.shape[0]
    slot = i % 2

    def row_copy(sl, dref, t, k, j):
        return pltpu.make_async_copy(_slab(ys_ref, dref[j]), _slab(buf_ref.at[sl], (k * T + t) * SLAB_ROWS),
                                     sem.at[sl])

    @pl.when(i == 0)
    def _():
        _for_rows(T, lambda t, k, j: row_copy(slot, dcur_ref, t, k, j).start(priority=k % 2))

    @pl.when(i + 1 < n)
    def _():
        _for_rows(T, lambda t, k, j: row_copy(1 - slot, dnext_ref, t, k, j).start(priority=k % 2))

    _for_rows(T, lambda t, k, j: row_copy(slot, dcur_ref, t, k, j).wait())

    wt = wt_ref[...]
    wb = [jnp.broadcast_to(wt[:, k:k + 1], (T, 128)) for k in range(TOP_K)]
    ys = []
    for c in range(SLAB_ROWS):
        chunk = lambda k: buf_ref[slot, pl.ds(k * T * SLAB_ROWS + c, T, stride=SLAB_ROWS), :]
        yc = wb[0] * chunk(0)
        for k in range(1, TOP_K):
            yc = yc + wb[k] * chunk(k)
        ys.append(yc)
    y = jnp.concatenate(ys, axis=1)
    out_ref[...] = _layer_norm(DEEPNORM_ALPHA * x_ref[...] + y, g_ref[...], b_ref[...])


def _combine(x2, dest_flat, wt_tok, ys, g, b):
    N, D = x2.shape
    T = T_COMBINE
    n = N // T
    vec = pl.BlockSpec((1, D), lambda i: (0, 0))
    smem = lambda f: pl.BlockSpec((TOP_K * T,), f, memory_space=pltpu.SMEM)
    return pl.pallas_call(
        _combine_kernel,
        grid=(n,),
        in_specs=[smem(lambda i: (i,)), smem(lambda i: (jnp.minimum(i + 1, n - 1),)),
                  pl.BlockSpec((T, D), lambda i: (i, 0)),
                  pl.BlockSpec((T, TOP_K), lambda i: (i, 0)),
                  pl.BlockSpec(memory_space=pl.ANY), vec, vec],
        out_specs=pl.BlockSpec((T, D), lambda i: (i, 0)),
        out_shape=jax.ShapeDtypeStruct((N, D), F32),
        scratch_shapes=[pltpu.VMEM((2, TOP_K * T * SLAB_ROWS, 128), F32), pltpu.SemaphoreType.DMA((2,))],
        compiler_params=_cparams("arbitrary"),
        name="moe_combine_ln3",
    )(dest_flat, dest_flat, x2, wt_tok, ys, g.reshape(1, D), b.reshape(1, D))


def _tile_major(a, T):
    K, N = a.shape
    return a.reshape(K, N // T, T).transpose(1, 0, 2).reshape(N * K)


def _moe(x2, router_w, router_b, w_gate, b_gate, w_up, b_up, w_down, b_down, ln_g, ln_b):
    N, D = x2.shape
    idx, wt = _router(x2, router_w, router_b)
    rank, cnt = _ranks(idx)
    counts = cnt[:, 0]
    blocks = (counts + BM - 1) // BM
    blk_end = jnp.cumsum(blocks)
    blk_start = blk_end - blocks
    dest = _dests(idx, rank, (blk_start * BM).astype(jnp.int32))
    n_blocks = -(-N * TOP_K // BM) + N_EXPERTS
    bi = jnp.arange(n_blocks, dtype=jnp.int32)
    be_raw = jnp.sum(bi[:, None] >= blk_end[None, :], axis=1).astype(jnp.int32)
    last_e = jnp.max(jnp.where(counts > 0, jnp.arange(N_EXPERTS), 0)).astype(jnp.int32)
    block_e = jnp.minimum(be_raw, last_e)
    in_use = bi < blk_end[-1]
    block_valid = jnp.where(in_use, jnp.clip(counts[block_e] - (bi - blk_start[block_e]) * BM, 0, BM), 0)
    dest8 = dest * SLAB_ROWS
    xs, (wg_bf, wu_bf, wd_bf) = _dispatch(x2, _tile_major(dest8, T_MOVE), n_blocks * BM, [w_gate, w_up, w_down])
    f3 = lambda t: t.astype(F32)[:, None, :]
    ys = _experts(xs, block_e, block_valid.astype(jnp.int32), wg_bf, wu_bf, wd_bf,
                  f3(b_gate), f3(b_up), f3(b_down))
    return _combine(x2, _tile_major(dest8, T_COMBINE), wt.T, ys, ln_g, ln_b)


def kernel(x, mem, w_in, ret_norm_g, w_out, ln1_g, ln1_b, mem_wq, mem_wk, mem_wv, mem_wo, ln2_g, ln2_b,
           router_w, router_b, w_gate, b_gate, w_up, b_up, w_down, b_down, ln3_g, ln3_b):
    B, S, D = x.shape
    assert [d for _, d in DILATED_PATTERNS] == [1, 4, 16] and all(w // d == BAND for w, d in DILATED_PATTERNS)
    for l in range(w_in.shape[0]):
        h, qkv4, qkv16 = _in_proj(x, w_in[l].astype(BF16))
        pats = [_band_attn(h, B, S, 1, N_IN // GROUP),
                _band_attn(qkv4, B, S // 4, 4, 3), _band_attn(qkv16, B, S // 16, 16, 3)]
        ret = _retention(h, ret_norm_g[l])
        x1 = _out_proj([p[0] for p in pats], [p[1] for p in pats], ret, x,
                       w_out[l].astype(BF16), ln1_g[l], ln1_b[l])
        kmem, vmem = _mem_kv(mem, mem_wk[l].astype(BF16), mem_wv[l].astype(BF16))
        x2 = _cross_attn(x1, kmem, vmem, mem_wq[l].astype(BF16), mem_wo[l].astype(BF16),
                         ln2_g[l], ln2_b[l])
        x = _moe(x2, router_w[l], router_b[l], w_gate[l], b_gate[l], w_up[l], b_up[l],
                 w_down[l], b_down[l], ln3_g[l], ln3_b[l]).reshape(B, S, D)
    return x
```

```python
import functools

import jax
import jax.numpy as jnp
import numpy as np
from jax import lax
from jax.experimental import pallas as pl
from jax.experimental.pallas import tpu as pltpu

F32 = jnp.float32
BF16 = jnp.bfloat16

HEAD_DIM = 64
N_HEADS_DIL = 8
N_HEADS_RET = 8
W_DIL = N_HEADS_DIL * HEAD_DIM
W_RET = N_HEADS_RET * HEAD_DIM
N_IN = 3 * W_DIL + 4 * W_RET
DILATED_PATTERNS = ((128, 1), (512, 4), (2048, 16))
BAND = 128
ROPE_THETA = 500000.0
ROT_DIM = HEAD_DIM // 4
RET_THETA = 10000.0
RET_CHUNK = 128
N_HEADS_MEM = 4
N_EXPERTS = 32
TOP_K = 4
SWIGLU_LIMIT = 7.0
SWIGLU_ALPHA = 1.702
LN_EPS = 1e-5
DEPTH = 1
DEEPNORM_ALPHA = (2 * DEPTH) ** 0.25

GROUP = 512
TM = 512
TQ = 512
T_ROUTE = 1024
T_RANK = 512
T_MOVE = 512
T_COMBINE = 256
BM = 1024
VMEM_LIMIT = 56 * 1024 * 1024
NEG = -1e30


def _cparams(*sem):
    return pltpu.CompilerParams(dimension_semantics=sem, vmem_limit_bytes=VMEM_LIMIT)


def _nt_dot(a, b):
    return lax.dot_general(a, b, (((1,), (1,)), ((), ())), preferred_element_type=F32)


def _layer_norm(y, g, b):
    mu = jnp.mean(y, axis=-1, keepdims=True)
    d = y - mu
    var = jnp.mean(d * d, axis=-1, keepdims=True)
    return d * lax.rsqrt(var + LN_EPS) * g + b


def _in_proj_kernel(x_ref, w_ref, ca_ref, la_ref, ha_ref, cr_ref, lr_ref, hr_ref, o_ref, o4_ref, o16_ref,
                    acc_ref):
    xb = x_ref[...].astype(BF16)
    rep = GROUP // 128
    tm = x_ref.shape[0]
    qkv = 3 * W_DIL

    def rot(acc, c_ref, lo_ref, hi_ref, half):
        c = jnp.tile(c_ref[...], (1, rep))
        lo = jnp.tile(lo_ref[...], (1, rep))
        hi = jnp.tile(hi_ref[...], (1, rep))
        up = pltpu.roll(acc, GROUP - half, axis=1)
        dn = pltpu.roll(acc, half, axis=1)
        return acc * c + up * lo + dn * hi

    for g in range(N_IN // GROUP):
        acc = jnp.dot(xb, w_ref[:, g * GROUP:(g + 1) * GROUP], preferred_element_type=F32)
        if g == 0:
            acc = rot(acc, ca_ref, la_ref, ha_ref, ROT_DIM // 2) * (HEAD_DIM ** -0.5)
        elif g == 1:
            acc = rot(acc, ca_ref, la_ref, ha_ref, ROT_DIM // 2)
        elif g == 3:
            acc = rot(acc, cr_ref, lr_ref, hr_ref, HEAD_DIM // 2)
        elif g == 4:
            acc = rot(acc, cr_ref, lr_ref, hr_ref, HEAD_DIM // 2) * (HEAD_DIM ** -0.5)
        o_ref[:, g * GROUP:(g + 1) * GROUP] = acc.astype(BF16)
        if g < 3:
            for c in range(rep):
                acc_ref[c] = acc[:, c * 128:(c + 1) * 128]
            for d, od_ref in ((4, o4_ref), (16, o16_ref)):
                for r in range(d):
                    for c in range(rep):
                        c0 = r * qkv + g * GROUP + c * 128
                        od_ref[:, c0:c0 + 128] = acc_ref[c, pl.ds(r, tm // d, stride=d), :].astype(BF16)


def _rotary_tables(S, theta, rot_dim):
    half = rot_dim // 2
    f32 = np.float32
    inv = f32(1.0) / (f32(theta) ** (np.arange(half, dtype=f32) / f32(half)))
    ang = np.arange(S, dtype=f32)[:, None] * inv[None, :]
    cos, sin = np.cos(ang).astype(f32), np.sin(ang).astype(f32)
    pad = HEAD_DIM - rot_dim
    c = np.concatenate([cos, cos, np.ones((S, pad), f32)], axis=1)
    lo = np.concatenate([-sin, np.zeros((S, half + pad), f32)], axis=1)
    hi = np.concatenate([np.zeros((S, half), f32), sin, np.zeros((S, pad), f32)], axis=1)
    two = lambda t: np.concatenate([t, t], axis=1)
    return two(c), two(lo), two(hi)


def _in_proj(x3, w_bf):
    B, S, D = x3.shape
    tabs = _rotary_tables(S, ROPE_THETA, ROT_DIM) + _rotary_tables(S, RET_THETA, HEAD_DIM)
    tab_spec = pl.BlockSpec((TM, 128), lambda b, i: (i, 0))
    qkv = 3 * W_DIL
    cls_spec = lambda d: pl.BlockSpec((None, TM // d, d * qkv), lambda b, i: (b, i, 0))
    cls_shape = lambda d: jax.ShapeDtypeStruct((B, S // d, d * qkv), BF16)
    return pl.pallas_call(
        _in_proj_kernel,
        grid=(B, S // TM),
        in_specs=[pl.BlockSpec((None, TM, D), lambda b, i: (b, i, 0)),
                  pl.BlockSpec((D, N_IN), lambda b, i: (0, 0))] + [tab_spec] * 6,
        out_specs=[pl.BlockSpec((None, TM, N_IN), lambda b, i: (b, i, 0)), cls_spec(4), cls_spec(16)],
        out_shape=[jax.ShapeDtypeStruct((B, S, N_IN), BF16), cls_shape(4), cls_shape(16)],
        scratch_shapes=[pltpu.VMEM((GROUP // 128, TM, 128), F32)],
        compiler_params=_cparams("parallel", "parallel"),
        name="in_proj",
    )(x3, w_bf, *tabs)


def _band_attn_kernel(q_ref, kc_ref, kp_ref, vc_ref, vp_ref, o_ref, l_ref,
                      k_all, v_all, s_scr, p_scr, r_scr):
    j = pl.program_id(2)
    tq = q_ref.shape[0]
    nsub = tq // BAND
    pair = 2 * HEAD_DIM
    n_pairs = N_HEADS_DIL // 2
    k_all[0:BAND, :] = kp_ref[...]
    k_all[BAND:, :] = kc_ref[...]
    v_all[0:BAND, :] = vp_ref[...]
    v_all[BAND:, :] = vc_ref[...]
    qi = lax.broadcasted_iota(jnp.int32, (BAND, 2 * BAND), 0)
    kj = lax.broadcasted_iota(jnp.int32, (BAND, 2 * BAND), 1)
    dist = qi + BAND - kj
    band = (dist >= 0) & (dist <= BAND)
    bias = jnp.where(band, 0.0, NEG)
    bias_first = jnp.where(band & ((kj >= BAND) | (j > 0)), 0.0, NEG)
    low = lax.broadcasted_iota(jnp.int32, (BAND, pair), 1) < HEAD_DIM
    zero = jnp.zeros((BAND, pair), BF16)

    for pr in range(n_pairs):
        cs = slice(pr * pair, (pr + 1) * pair)
        for n in range(nsub):
            q = q_ref[n * BAND:(n + 1) * BAND, cs]
            kk = k_all[n * BAND:(n + 2) * BAND, cs]
            b = bias_first if n == 0 else bias
            i = pr * nsub + n
            s_scr[2 * i] = _nt_dot(jnp.where(low, q, zero), kk) + b
            s_scr[2 * i + 1] = _nt_dot(jnp.where(low, zero, q), kk) + b

    for pr in range(n_pairs):
        cs = slice(pr * pair, (pr + 1) * pair)
        for n in range(nsub):
            i = pr * nsub + n
            stats = []
            for a in range(2):
                s = s_scr[2 * i + a]
                m = jnp.max(s, axis=-1, keepdims=True)
                p = jnp.exp(s - m)
                l = jnp.sum(p, axis=-1, keepdims=True)
                p_scr[2 * i + a] = p.astype(BF16)
                stats.append((m, l))
            (m0, l0), (m1, l1) = stats
            r_scr[i] = jnp.where(low, 1.0 / l0, 1.0 / l1)
            l_ref[n * BAND:(n + 1) * BAND, cs] = jnp.where(low, m0 + jnp.log(l0), m1 + jnp.log(l1))

    for pr in range(n_pairs):
        cs = slice(pr * pair, (pr + 1) * pair)
        for n in range(nsub):
            i = pr * nsub + n
            vv = v_all[n * BAND:(n + 2) * BAND, cs]
            o0 = jnp.dot(p_scr[2 * i], vv, preferred_element_type=F32)
            o1 = jnp.dot(p_scr[2 * i + 1], vv, preferred_element_type=F32)
            o_ref[n * BAND:(n + 1) * BAND, cs] = jnp.where(low, o0, o1) * r_scr[i]


def _band_attn(src, B, L, d, gpr):
    tq = min(TQ, L)
    assert L % tq == 0
    sub = tq // BAND
    cur = lambda g: pl.BlockSpec((None, tq, GROUP), lambda b, r, j: (b, j, r * gpr + g))
    prev = lambda g: pl.BlockSpec((None, BAND, GROUP),
                                  lambda b, r, j: (b, jnp.maximum(j * sub - 1, 0), r * gpr + g))
    out_spec = pl.BlockSpec((None, tq, W_DIL), lambda b, r, j: (b, j, r))
    shp = jax.ShapeDtypeStruct((B, L, d * W_DIL), F32)
    n_blk = (N_HEADS_DIL // 2) * sub
    return pl.pallas_call(
        _band_attn_kernel,
        grid=(B, d, L // tq),
        in_specs=[cur(0), cur(1), prev(1), cur(2), prev(2)],
        out_specs=[out_spec, out_spec],
        out_shape=[shp, shp],
        scratch_shapes=[pltpu.VMEM((BAND + tq, GROUP), BF16), pltpu.VMEM((BAND + tq, GROUP), BF16),
                        pltpu.VMEM((2 * n_blk, BAND, 2 * BAND), F32),
                        pltpu.VMEM((2 * n_blk, BAND, 2 * BAND), BF16),
                        pltpu.VMEM((n_blk, BAND, 2 * HEAD_DIM), F32)],
        compiler_params=_cparams("parallel", "parallel", "parallel"),
        name=f"band_attn_d{d}",
    )(src, src, src, src, src)


RET_UNROLL = 32


def _retention_kernel(q_ref, k_ref, v_ref, g_ref, gain_ref, din_ref, kd_ref, qd_ref, cd_ref,
                      o_ref, kv_ref, st_ref):
    C = RET_CHUNK
    U = min(RET_UNROLL, q_ref.shape[0] // C)
    n_iter = q_ref.shape[0] // (C * U)
    pair = 2 * HEAD_DIM
    low = lax.broadcasted_iota(jnp.int32, (C, pair), 1) < HEAD_DIM

    def rows(i, u):
        return pl.ds(pl.multiple_of(i * (C * U), C * U) + u * C, C)

    def kv_body(i, carry):
        for u in range(U):
            kd = (k_ref[rows(i, u), :].astype(F32) * kd_ref[...]).astype(BF16)
            kv_ref[i * U + u] = lax.dot_general(kd, v_ref[rows(i, u), :], (((0,), (0,)), ((), ())),
                                                preferred_element_type=F32)
        return carry

    lax.fori_loop(0, n_iter, kv_body, 0)

    def scan_body(c, state):
        st_ref[c] = state.astype(BF16)
        return state * cd_ref[...] + kv_ref[c]

    lax.fori_loop(0, n_iter * U, scan_body, jnp.zeros((pair, pair), F32))

    def out_body(i, carry):
        for u in range(U):
            q = q_ref[rows(i, u), :]
            k = k_ref[rows(i, u), :]
            v = v_ref[rows(i, u), :]
            prev = st_ref[i * U + u]
            ys = []
            for a in range(2):
                qa = jnp.where(low, q, jnp.zeros_like(q)) if a == 0 else jnp.where(low, jnp.zeros_like(q), q)
                s = _nt_dot(qa, k) * din_ref[a]
                inner = jnp.dot(s.astype(BF16), v, preferred_element_type=F32)
                qd = (qa.astype(F32) * qd_ref[...]).astype(BF16)
                ys.append(inner + jnp.dot(qd, prev, preferred_element_type=F32))
            y = jnp.where(low, ys[0], ys[1])
            inv = 1.0 / HEAD_DIM
            half_sum = lambda t: jnp.where(low, jnp.sum(jnp.where(low, t, 0.0), axis=-1, keepdims=True),
                                           jnp.sum(jnp.where(low, 0.0, t), axis=-1, keepdims=True))
            dlt = y - half_sum(y) * inv
            var = half_sum(dlt * dlt) * inv
            yn = dlt * lax.rsqrt(var + LN_EPS) * gain_ref[...]
            gr = g_ref[rows(i, u), :].astype(F32)
            o_ref[rows(i, u), :] = (yn * (gr / (1.0 + jnp.exp(-gr)))).astype(BF16)
        return carry

    lax.fori_loop(0, n_iter, out_body, 0)


def _retention_tables():
    H, C = N_HEADS_RET, RET_CHUNK
    f32 = np.float32
    log_g = np.log1p(-(f32(2.0) ** (f32(-5.0) - np.arange(H, dtype=f32)))).astype(f32)
    idx = np.arange(C, dtype=f32)
    diff = idx[:, None] - idx[None, :]
    inner_decay = (np.exp(log_g[:, None, None] * np.maximum(diff, f32(0.0))) * (diff >= 0)).astype(f32)
    k_decay = np.exp(log_g[:, None] * (f32(C - 1) - idx)).astype(f32)
    q_decay = np.exp(log_g[:, None] * (idx + f32(1.0))).astype(f32)
    chunk_decay = np.exp(log_g * f32(C)).astype(f32)
    lanes = lambda t: np.repeat(t.reshape(H // 2, 2, C).transpose(0, 2, 1), HEAD_DIM, axis=2)
    cd = np.ascontiguousarray(np.broadcast_to(
        np.repeat(chunk_decay.reshape(H // 2, 2), HEAD_DIM, axis=1)[:, :, None],
        (H // 2, 2 * HEAD_DIM, 2 * HEAD_DIM)))
    return inner_decay, lanes(k_decay), lanes(q_decay), cd


def _retention(h, gain):
    B, S, _ = h.shape
    lanes = 2 * HEAD_DIM
    col0 = 3 * W_DIL // lanes
    per = W_RET // lanes
    col = lambda g: pl.BlockSpec((None, S, lanes), lambda b, p: (b, 0, col0 + g * per + p))
    din, kd, qd, cd = _retention_tables()
    C = RET_CHUNK
    assert S % (C * min(RET_UNROLL, S // C)) == 0
    tab = lambda a: pl.BlockSpec((None, a, lanes), lambda b, p: (p, 0, 0))
    return pl.pallas_call(
        _retention_kernel,
        grid=(B, per),
        in_specs=[col(0), col(1), col(2), col(3),
                  pl.BlockSpec((1, lanes), lambda b, p: (0, p)),
                  pl.BlockSpec((2, C, C), lambda b, p: (p, 0, 0)), tab(C), tab(C), tab(lanes)],
        out_specs=pl.BlockSpec((None, S, lanes), lambda b, p: (b, 0, p)),
        out_shape=jax.ShapeDtypeStruct((B, S, W_RET), BF16),
        scratch_shapes=[pltpu.VMEM((S // C, lanes, lanes), F32), pltpu.VMEM((S // C, lanes, lanes), BF16)],
        compiler_params=_cparams("parallel", "parallel"),
        name="retention",
    )(h, h, h, h, gain.reshape(1, W_RET).astype(F32), din, kd, qd, cd)


def _out_proj_kernel(o1, o4, o16, l1, l4, l16, ret_ref, x_ref, w_ref, g_ref, b_ref, out_ref,
                     so4, so16, sl4, sl16):
    tm = x_ref.shape[0]
    n_slab = W_DIL // 128
    for d, pairs in ((4, ((o4, so4), (l4, sl4))), (16, ((o16, so16), (l16, sl16)))):
        for src, dst in pairs:
            for r in range(d):
                for c in range(n_slab):
                    c0 = r * W_DIL + c * 128
                    dst[c, pl.ds(r, tm // d, stride=d), :] = src[:, c0:c0 + 128]
    atts = []
    for c in range(n_slab):
        cs = slice(c * 128, (c + 1) * 128)
        a1, a4, a16 = l1[:, cs], sl4[c], sl16[c]
        m = jnp.maximum(jnp.maximum(a1, a4), a16)
        e1, e4, e16 = jnp.exp(a1 - m), jnp.exp(a4 - m), jnp.exp(a16 - m)
        att = (e1 * o1[:, cs] + e4 * so4[c] + e16 * so16[c]) / (e1 + e4 + e16)
        atts.append(att.astype(BF16))
    acc = jnp.dot(jnp.concatenate(atts, axis=1), w_ref[:W_DIL, :], preferred_element_type=F32)
    acc += jnp.dot(ret_ref[...], w_ref[W_DIL:, :], preferred_element_type=F32)
    out_ref[...] = _layer_norm(DEEPNORM_ALPHA * x_ref[...] + acc, g_ref[...], b_ref[...])


def _out_proj(os_, ls_, ret, x3, w_bf, g, b):
    B, S, D = x3.shape
    cls = lambda d: pl.BlockSpec((None, TM // d, d * W_DIL), lambda b_, i: (b_, i, 0))
    half = cls(1)
    full = pl.BlockSpec((None, TM, D), lambda b_, i: (b_, i, 0))
    vec = pl.BlockSpec((1, D), lambda b_, i: (0, 0))
    return pl.pallas_call(
        _out_proj_kernel,
        grid=(B, S // TM),
        in_specs=[half, cls(4), cls(16), half, cls(4), cls(16), half, full,
                  pl.BlockSpec((D, D), lambda b_, i: (0, 0)), vec, vec],
        out_specs=full,
        out_shape=jax.ShapeDtypeStruct((B, S, D), F32),
        scratch_shapes=[pltpu.VMEM((W_DIL // 128, TM, 128), F32)] * 4,
        compiler_params=_cparams("parallel", "parallel"),
        name="out_proj_ln1",
    )(*os_, *ls_, ret, x3, w_bf, g.reshape(1, D), b.reshape(1, D))


def _mem_kv_kernel(m_ref, wk_ref, wv_ref, k_ref, v_ref):
    mb = m_ref[...].astype(BF16)
    k_ref[...] = jnp.dot(mb, wk_ref[...], preferred_element_type=F32).astype(BF16)
    v_ref[...] = jnp.dot(mb, wv_ref[...], preferred_element_type=F32).astype(BF16)


def _mem_kv(mem, wk_bf, wv_bf):
    B, M, D = mem.shape
    blk = pl.BlockSpec((None, M, D), lambda b: (b, 0, 0))
    wsp = pl.BlockSpec((D, D), lambda b: (0, 0))
    shp = jax.ShapeDtypeStruct((B, M, D), BF16)
    return pl.pallas_call(
        _mem_kv_kernel, grid=(B,), in_specs=[blk, wsp, wsp], out_specs=[blk, blk], out_shape=[shp, shp],
        compiler_params=_cparams("parallel"), name="mem_kv",
    )(mem, wk_bf, wv_bf)


def _cross_attn_kernel(x_ref, k_ref, v_ref, wq_ref, wo_ref, g_ref, b_ref, out_ref):
    x = x_ref[...]
    hd = x.shape[-1] // N_HEADS_MEM
    q = (jnp.dot(x.astype(BF16), wq_ref[...], preferred_element_type=F32) * (hd ** -0.5)).astype(BF16)
    outs = []
    for hh in range(N_HEADS_MEM):
        cs = slice(hh * hd, (hh + 1) * hd)
        s = _nt_dot(q[:, cs], k_ref[:, cs])
        m = jnp.max(s, axis=-1, keepdims=True)
        p = jnp.exp(s - m)
        l = jnp.sum(p, axis=-1, keepdims=True)
        outs.append((jnp.dot(p.astype(BF16), v_ref[:, cs], preferred_element_type=F32) / l).astype(BF16))
    o = jnp.concatenate(outs, axis=-1)
    c = jnp.dot(o, wo_ref[...], preferred_element_type=F32)
    out_ref[...] = _layer_norm(DEEPNORM_ALPHA * x + c, g_ref[...], b_ref[...])


def _cross_attn(x3, kmem, vmem, wq_bf, wo_bf, g, b):
    B, S, D = x3.shape
    M = kmem.shape[1]
    xs = pl.BlockSpec((None, TM, D), lambda b_, i: (b_, i, 0))
    ms = pl.BlockSpec((None, M, D), lambda b_, i: (b_, 0, 0))
    ws = pl.BlockSpec((D, D), lambda b_, i: (0, 0))
    vs = pl.BlockSpec((1, D), lambda b_, i: (0, 0))
    return pl.pallas_call(
        _cross_attn_kernel,
        grid=(B, S // TM),
        in_specs=[xs, ms, ms, ws, ws, vs, vs],
        out_specs=xs,
        out_shape=jax.ShapeDtypeStruct((B, S, D), F32),
        compiler_params=_cparams("parallel", "parallel"),
        name="cross_attn_ln2",
    )(x3, kmem, vmem, wq_bf, wo_bf, g.reshape(1, D), b.reshape(1, D)).reshape(B * S, D)


def _split_bf16(t):
    hi = t.astype(BF16)
    return hi, (t - hi.astype(F32)).astype(BF16)


def _router_kernel(x_ref, w_ref, b_ref, idx_ref, wt_ref):
    x_hi, x_lo = _split_bf16(x_ref[...])
    by_hi = _nt_dot(w_ref[...], x_hi)
    logits = (by_hi[:N_EXPERTS] + (by_hi[N_EXPERTS:] + _nt_dot(w_ref[:N_EXPERTS, :], x_lo))) + b_ref[:, :1]
    e_iota = lax.broadcasted_iota(jnp.int32, logits.shape, 0)
    cur = logits
    vals, idxs = [], []
    for _ in range(TOP_K):
        m = jnp.max(cur, axis=0, keepdims=True)
        idx = jnp.min(jnp.where(cur == m, e_iota, N_EXPERTS), axis=0, keepdims=True)
        cur = jnp.where(e_iota == idx, -jnp.inf, cur)
        vals.append(m)
        idxs.append(idx)
    es = [jnp.exp(v - vals[0]) for v in vals]
    tot = es[0] + es[1] + es[2] + es[3]
    idx_ref[...] = jnp.concatenate(idxs, axis=0)
    wt_ref[...] = jnp.concatenate([e / tot for e in es], axis=0)


def _router(x2, router_w, router_b):
    N, D = x2.shape
    out = pl.BlockSpec((TOP_K, T_ROUTE), lambda i: (0, i))
    return pl.pallas_call(
        _router_kernel,
        grid=(N // T_ROUTE,),
        in_specs=[pl.BlockSpec((T_ROUTE, D), lambda i: (i, 0)),
                  pl.BlockSpec((2 * N_EXPERTS, D), lambda i: (0, 0)),
                  pl.BlockSpec((N_EXPERTS, 128), lambda i: (0, 0))],
        out_specs=[out, out],
        out_shape=[jax.ShapeDtypeStruct((TOP_K, N), jnp.int32), jax.ShapeDtypeStruct((TOP_K, N), F32)],
        compiler_params=_cparams("parallel"),
        name="router_top4",
    )(x2, jnp.concatenate(_split_bf16(router_w.T.astype(F32)), axis=0),
      jnp.broadcast_to(router_b.astype(F32)[:, None], (N_EXPERTS, 128)))


def _rank_kernel(idx_ref, tri_ref, rank_ref, cnt_ref, carry_ref):
    @pl.when(pl.program_id(0) == 0)
    def _():
        carry_ref[...] = jnp.zeros_like(carry_ref)

    T = idx_ref.shape[1]
    e_iota = lax.broadcasted_iota(jnp.int32, (N_EXPERTS, T), 0)
    hot = [e_iota == idx_ref[k:k + 1, :] for k in range(TOP_K)]
    c = sum(h.astype(F32) for h in hot)
    before = jnp.dot(c.astype(BF16), tri_ref[...], preferred_element_type=F32) + carry_ref[:, :1]
    rank_ref[...] = jnp.concatenate(
        [jnp.sum(jnp.where(h, before, 0.0), axis=0, keepdims=True) for h in hot], axis=0).astype(jnp.int32)
    carry_ref[...] = carry_ref[...] + jnp.sum(c, axis=1, keepdims=True)
    cnt_ref[...] = carry_ref[...].astype(jnp.int32)


def _ranks(idx):
    N = idx.shape[1]
    T = T_RANK
    tri = jnp.asarray(np.arange(T)[:, None] < np.arange(T)[None, :], BF16)
    blk = pl.BlockSpec((TOP_K, T), lambda i: (0, i))
    return pl.pallas_call(
        _rank_kernel,
        grid=(N // T,),
        in_specs=[blk, pl.BlockSpec((T, T), lambda i: (0, 0))],
        out_specs=[blk, pl.BlockSpec((N_EXPERTS, 128), lambda i: (0, 0))],
        out_shape=[jax.ShapeDtypeStruct((TOP_K, N), jnp.int32),
                   jax.ShapeDtypeStruct((N_EXPERTS, 128), jnp.int32)],
        scratch_shapes=[pltpu.VMEM((N_EXPERTS, 128), F32)],
        compiler_params=_cparams("arbitrary"),
        name="expert_ranks",
    )(idx, tri)


def _dest_kernel(idx_ref, rank_ref, start_ref, dest_ref):
    T = idx_ref.shape[1]
    e_iota = lax.broadcasted_iota(jnp.int32, (N_EXPERTS, T), 0)
    start = jnp.tile(start_ref[...], (1, T // 128))
    rows = [jnp.sum(jnp.where(e_iota == idx_ref[k:k + 1, :], start, 0), axis=0, keepdims=True)
            for k in range(TOP_K)]
    dest_ref[...] = jnp.concatenate(rows, axis=0) + rank_ref[...]


def _dests(idx, rank, group_start):
    N = idx.shape[1]
    T = T_ROUTE
    blk = pl.BlockSpec((TOP_K, T), lambda i: (0, i))
    return pl.pallas_call(
        _dest_kernel,
        grid=(N // T,),
        in_specs=[blk, blk, pl.BlockSpec((N_EXPERTS, 128), lambda i: (0, 0))],
        out_specs=blk,
        out_shape=jax.ShapeDtypeStruct((TOP_K, N), jnp.int32),
        compiler_params=_cparams("parallel"),
        name="expert_dests",
    )(idx, rank, jnp.broadcast_to(group_start[:, None], (N_EXPERTS, 128)))


MOVE_UNROLL = 8
SLAB_ROWS = 8


def _for_rows(T, fn):
    def body(g, carry):
        for u in range(MOVE_UNROLL):
            for k in range(TOP_K):
                t = g * MOVE_UNROLL + u
                fn(t, k, k * T + t)
        return carry

    lax.fori_loop(0, T // MOVE_UNROLL, body, 0)


def _slab(ref_at, row8):
    return ref_at.at[pl.ds(pl.multiple_of(row8, SLAB_ROWS), SLAB_ROWS), :]


def _dispatch_kernel(dcur_ref, dprev_ref, x_ref, *refs, n_cast):
    w_refs, xs_ref, wb_refs, (pk_ref, sem) = refs[:n_cast], refs[n_cast], refs[n_cast + 1:-2], refs[-2:]
    for w_ref, wb_ref in zip(w_refs, wb_refs):
        wb_ref[...] = w_ref[...].astype(BF16)
    i = pl.program_id(0)
    n = pl.num_programs(0)
    T = x_ref.shape[0]
    slot = i % 2
    for c in range(SLAB_ROWS):
        pk_ref[slot, pl.ds(c, T, stride=SLAB_ROWS), :] = x_ref[:, c * 128:(c + 1) * 128]

    def row_copy(sl, dref, t, j):
        return pltpu.make_async_copy(_slab(pk_ref.at[sl], t * SLAB_ROWS), _slab(xs_ref, dref[j]), sem.at[sl])

    _for_rows(T, lambda t, k, j: row_copy(slot, dcur_ref, t, j).start(priority=k % 2))

    @pl.when(i > 0)
    def _():
        _for_rows(T, lambda t, k, j: row_copy(1 - slot, dprev_ref, t, j).wait())

    @pl.when(i == n - 1)
    def _():
        _for_rows(T, lambda t, k, j: row_copy(slot, dcur_ref, t, j).wait())


CAST_BLOCK_BYTES = 2 * 1024 * 1024


def _dispatch(x2, dest8, P, weights):
    N, D = x2.shape
    assert D == SLAB_ROWS * 128
    T = T_MOVE
    n = N // T
    flat = [w.reshape(-1, w.shape[-1]) for w in weights]
    fused = all(f.shape[0] % n == 0 and (f.shape[0] // n) % 16 == 0
                and (f.shape[0] // n) * f.shape[1] * 4 <= CAST_BLOCK_BYTES for f in flat)
    if not fused:
        flat = []
    w_specs = [pl.BlockSpec((f.shape[0] // n, f.shape[1]), lambda i: (i, 0)) for f in flat]
    smem = lambda f: pl.BlockSpec((TOP_K * T,), f, memory_space=pltpu.SMEM)
    outs = pl.pallas_call(
        functools.partial(_dispatch_kernel, n_cast=len(flat)),
        grid=(n,),
        in_specs=[smem(lambda i: (i,)), smem(lambda i: (jnp.maximum(i - 1, 0),)),
                  pl.BlockSpec((T, D), lambda i: (i, 0))] + w_specs,
        out_specs=[pl.BlockSpec(memory_space=pl.ANY)] + w_specs,
        out_shape=[jax.ShapeDtypeStruct((P * SLAB_ROWS, 128), F32)]
                  + [jax.ShapeDtypeStruct(f.shape, BF16) for f in flat],
        scratch_shapes=[pltpu.VMEM((2, T * SLAB_ROWS, 128), F32), pltpu.SemaphoreType.DMA((2,))],
        compiler_params=_cparams("arbitrary"),
        name="moe_dispatch",
    )(dest8, dest8, x2, *flat)
    if fused:
        return outs[0], [o.reshape(w.shape) for o, w in zip(outs[1:], weights)]
    return outs[0], [w.astype(BF16) for w in weights]


def _expert_kernel(be_ref, bv_ref, xs_ref, wg_ref, wu_ref, wd_ref, bg_ref, bu_ref, bd_ref, ys_ref):
    j = pl.program_id(0)
    valid = bv_ref[j]

    @pl.when(valid > 0)
    def _():
        bm = xs_ref.shape[0] // SLAB_ROWS
        row = lax.broadcasted_iota(jnp.int32, (bm, 128), 0)
        xb = jnp.concatenate(
            [jnp.where(row < valid, xs_ref[pl.ds(c, bm, stride=SLAB_ROWS), :], 0.0).astype(BF16)
             for c in range(SLAB_ROWS)], axis=1)

        def proj(w_ref, b_ref):
            return jnp.dot(xb, w_ref[...], preferred_element_type=F32) + b_ref[...]

        gate = jnp.minimum(proj(wg_ref, bg_ref), SWIGLU_LIMIT)
        up = jnp.clip(proj(wu_ref, bu_ref), -SWIGLU_LIMIT, SWIGLU_LIMIT)
        hmid = gate * (1.0 / (1.0 + jnp.exp(-SWIGLU_ALPHA * gate))) * (up + 1.0)
        out = jnp.dot(hmid.astype(BF16), wd_ref[...], preferred_element_type=F32) + bd_ref[...]
        for c in range(SLAB_ROWS):
            ys_ref[pl.ds(c, bm, stride=SLAB_ROWS), :] = out[:, c * 128:(c + 1) * 128]


def _experts(xs, block_e, block_valid, wg, wu, wd, bg, bu, bd):
    D = SLAB_ROWS * 128
    F = wg.shape[-1]
    wspec = lambda a, c: pl.BlockSpec((None, a, c), lambda j, be, bv: (be[j], 0, 0))
    slabs = pl.BlockSpec((BM * SLAB_ROWS, 128), lambda j, be, bv: (j, 0))
    grid_spec = pltpu.PrefetchScalarGridSpec(
        num_scalar_prefetch=2,
        grid=(xs.shape[0] // (BM * SLAB_ROWS),),
        in_specs=[slabs, wspec(D, F), wspec(D, F), wspec(F, D), wspec(1, F), wspec(1, F), wspec(1, D)],
        out_specs=slabs,
    )
    return pl.pallas_call(
        _expert_kernel,
        grid_spec=grid_spec,
        out_shape=jax.ShapeDtypeStruct(xs.shape, F32),
        compiler_params=_cparams("arbitrary"),
        name="moe_experts",
    )(block_e, block_valid, xs, wg, wu, wd, bg, bu, bd)


def _combine_kernel(dcur_ref, dnext_ref, x_ref, wt_ref, ys_ref, g_ref, b_ref, out_ref, buf_ref, sem):
    i = pl.program_id(0)
    n = pl.num_programs(0)
    T = x_ref.shape[0]
    slot = i % 2

    def row_copy(sl, dref, t, k, j):
        return pltpu.make_async_copy(_slab(ys_ref, dref[j]), _slab(buf_ref.at[sl], (k * T + t) * SLAB_ROWS),
                                     sem.at[sl])

    @pl.when(i == 0)
    def _():
        _for_rows(T, lambda t, k, j: row_copy(slot, dcur_ref, t, k, j).start(priority=k % 2))

    @pl.when(i + 1 < n)
    def _():
        _for_rows(T, lambda t, k, j: row_copy(1 - slot, dnext_ref, t, k, j).start(priority=k % 2))

    _for_rows(T, lambda t, k, j: row_copy(slot, dcur_ref, t, k, j).wait())

    wt = wt_ref[...]
    wb = [jnp.broadcast_to(wt[:, k:k + 1], (T, 128)) for k in range(TOP_K)]
    ys = []
    for c in range(SLAB_ROWS):
        chunk = lambda k: buf_ref[slot, pl.ds(k * T * SLAB_ROWS + c, T, stride=SLAB_ROWS), :]
        yc = wb[0] * chunk(0)
        for k in range(1, TOP_K):
            yc = yc + wb[k] * chunk(k)
        ys.append(yc)
    y = jnp.concatenate(ys, axis=1)
    out_ref[...] = _layer_norm(DEEPNORM_ALPHA * x_ref[...] + y, g_ref[...], b_ref[...])


def _combine(x2, dest_flat, wt_tok, ys, g, b):
    N, D = x2.shape
    T = T_COMBINE
    n = N // T
    vec = pl.BlockSpec((1, D), lambda i: (0, 0))
    smem = lambda f: pl.BlockSpec((TOP_K * T,), f, memory_space=pltpu.SMEM)
    return pl.pallas_call(
        _combine_kernel,
        grid=(n,),
        in_specs=[smem(lambda i: (i,)), smem(lambda i: (jnp.minimum(i + 1, n - 1),)),
                  pl.BlockSpec((T, D), lambda i: (i, 0)),
                  pl.BlockSpec((T, TOP_K), lambda i: (i, 0)),
                  pl.BlockSpec(memory_space=pl.ANY), vec, vec],
        out_specs=pl.BlockSpec((T, D), lambda i: (i, 0)),
        out_shape=jax.ShapeDtypeStruct((N, D), F32),
        scratch_shapes=[pltpu.VMEM((2, TOP_K * T * SLAB_ROWS, 128), F32), pltpu.SemaphoreType.DMA((2,))],
        compiler_params=_cparams("arbitrary"),
        name="moe_combine_ln3",
    )(dest_flat, dest_flat, x2, wt_tok, ys, g.reshape(1, D), b.reshape(1, D))


def _tile_major(a, T):
    K, N = a.shape
    return a.reshape(K, N // T, T).transpose(1, 0, 2).reshape(N * K)


def _moe(x2, router_w, router_b, w_gate, b_gate, w_up, b_up, w_down, b_down, ln_g, ln_b):
    N, D = x2.shape
    idx, wt = _router(x2, router_w, router_b)
    rank, cnt = _ranks(idx)
    counts = cnt[:, 0]
    blocks = (counts + BM - 1) // BM
    blk_end = jnp.cumsum(blocks)
    blk_start = blk_end - blocks
    dest = _dests(idx, rank, (blk_start * BM).astype(jnp.int32))
    n_blocks = -(-N * TOP_K // BM) + N_EXPERTS
    bi = jnp.arange(n_blocks, dtype=jnp.int32)
    be_raw = jnp.sum(bi[:, None] >= blk_end[None, :], axis=1).astype(jnp.int32)
    last_e = jnp.max(jnp.where(counts > 0, jnp.arange(N_EXPERTS), 0)).astype(jnp.int32)
    block_e = jnp.minimum(be_raw, last_e)
    in_use = bi < blk_end[-1]
    block_valid = jnp.where(in_use, jnp.clip(counts[block_e] - (bi - blk_start[block_e]) * BM, 0, BM), 0)
    dest8 = dest * SLAB_ROWS
    xs, (wg_bf, wu_bf, wd_bf) = _dispatch(x2, _tile_major(dest8, T_MOVE), n_blocks * BM, [w_gate, w_up, w_down])
    f3 = lambda t: t.astype(F32)[:, None, :]
    ys = _experts(xs, block_e, block_valid.astype(jnp.int32), wg_bf, wu_bf, wd_bf,
                  f3(b_gate), f3(b_up), f3(b_down))
    return _combine(x2, _tile_major(dest8, T_COMBINE), wt.T, ys, ln_g, ln_b)


def kernel(x, mem, w_in, ret_norm_g, w_out, ln1_g, ln1_b, mem_wq, mem_wk, mem_wv, mem_wo, ln2_g, ln2_b,
           router_w, router_b, w_gate, b_gate, w_up, b_up, w_down, b_down, ln3_g, ln3_b):
    B, S, D = x.shape
    assert [d for _, d in DILATED_PATTERNS] == [1, 4, 16] and all(w // d == BAND for w, d in DILATED_PATTERNS)
    for l in range(w_in.shape[0]):
        h, qkv4, qkv16 = _in_proj(x, w_in[l].astype(BF16))
        pats = [_band_attn(h, B, S, 1, N_IN // GROUP),
                _band_attn(qkv4, B, S // 4, 4, 3), _band_attn(qkv16, B, S // 16, 16, 3)]
        ret = _retention(h, ret_norm_g[l])
        x1 = _out_proj([p[0] for p in pats], [p[1] for p in pats], ret, x,
                       w_out[l].astype(BF16), ln1_g[l], ln1_b[l])
        kmem, vmem = _mem_kv(mem, mem_wk[l].astype(BF16), mem_wv[l].astype(BF16))
        x2 = _cross_attn(x1, kmem, vmem, mem_wq[l].astype(BF16), mem_wo[l].astype(BF16),
                         ln2_g[l], ln2_b[l])
        x = _moe(x2, router_w[l], router_b[l], w_gate[l], b_gate[l], w_up[l], b_up[l],
                 w_down[l], b_down[l], ln3_g[l], ln3_b[l]).reshape(B, S, D)
    return x
```

```python
import functools

import jax
import jax.numpy as jnp
import numpy as np
from jax import lax
from jax.experimental import pallas as pl
from jax.experimental.pallas import tpu as pltpu

F32 = jnp.float32
BF16 = jnp.bfloat16

HEAD_DIM = 64
N_HEADS_DIL = 8
N_HEADS_RET = 8
W_DIL = N_HEADS_DIL * HEAD_DIM
W_RET = N_HEADS_RET * HEAD_DIM
N_IN = 3 * W_DIL + 4 * W_RET
DILATED_PATTERNS = ((128, 1), (512, 4), (2048, 16))
BAND = 128
ROPE_THETA = 500000.0
ROT_DIM = HEAD_DIM // 4
RET_THETA = 10000.0
RET_CHUNK = 128
N_HEADS_MEM = 4
N_EXPERTS = 32
TOP_K = 4
SWIGLU_LIMIT = 7.0
SWIGLU_ALPHA = 1.702
LN_EPS = 1e-5
DEPTH = 1
DEEPNORM_ALPHA = (2 * DEPTH) ** 0.25

GROUP = 512
TM = 512
TQ = 512
T_ROUTE = 1024
T_RANK = 512
RANK_CHUNKS = 4
T_DEST = 4096
T_MOVE = 512
T_COMBINE = 256
BM = 1024
VMEM_LIMIT = 56 * 1024 * 1024
NEG = -1e30


def _cparams(*sem):
    return pltpu.CompilerParams(dimension_semantics=sem, vmem_limit_bytes=VMEM_LIMIT)


def _nt_dot(a, b):
    return lax.dot_general(a, b, (((1,), (1,)), ((), ())), preferred_element_type=F32)


def _layer_norm(y, g, b):
    mu = jnp.mean(y, axis=-1, keepdims=True)
    d = y - mu
    var = jnp.mean(d * d, axis=-1, keepdims=True)
    return d * lax.rsqrt(var + LN_EPS) * g + b


def _in_proj_kernel(x_ref, w_ref, ca_ref, la_ref, ha_ref, cr_ref, lr_ref, hr_ref, o_ref, o4_ref, o16_ref,
                    acc_ref):
    xb = x_ref[...].astype(BF16)
    rep = GROUP // 128
    tm = x_ref.shape[0]
    qkv = 3 * W_DIL

    def rot(acc, c_ref, lo_ref, hi_ref, half):
        c = jnp.tile(c_ref[...], (1, rep))
        lo = jnp.tile(lo_ref[...], (1, rep))
        hi = jnp.tile(hi_ref[...], (1, rep))
        up = pltpu.roll(acc, GROUP - half, axis=1)
        dn = pltpu.roll(acc, half, axis=1)
        return acc * c + up * lo + dn * hi

    for g in range(N_IN // GROUP):
        acc = jnp.dot(xb, w_ref[:, g * GROUP:(g + 1) * GROUP], preferred_element_type=F32)
        if g == 0:
            acc = rot(acc, ca_ref, la_ref, ha_ref, ROT_DIM // 2) * (HEAD_DIM ** -0.5)
        elif g == 1:
            acc = rot(acc, ca_ref, la_ref, ha_ref, ROT_DIM // 2)
        elif g == 3:
            acc = rot(acc, cr_ref, lr_ref, hr_ref, HEAD_DIM // 2)
        elif g == 4:
            acc = rot(acc, cr_ref, lr_ref, hr_ref, HEAD_DIM // 2) * (HEAD_DIM ** -0.5)
        o_ref[:, g * GROUP:(g + 1) * GROUP] = acc.astype(BF16)
        if g < 3:
            for c in range(rep):
                acc_ref[c] = acc[:, c * 128:(c + 1) * 128]
            for d, od_ref in ((4, o4_ref), (16, o16_ref)):
                for r in range(d):
                    for c in range(rep):
                        c0 = r * qkv + g * GROUP + c * 128
                        od_ref[:, c0:c0 + 128] = acc_ref[c, pl.ds(r, tm // d, stride=d), :].astype(BF16)


def _rotary_tables(S, theta, rot_dim):
    half = rot_dim // 2
    f32 = np.float32
    inv = f32(1.0) / (f32(theta) ** (np.arange(half, dtype=f32) / f32(half)))
    ang = np.arange(S, dtype=f32)[:, None] * inv[None, :]
    cos, sin = np.cos(ang).astype(f32), np.sin(ang).astype(f32)
    pad = HEAD_DIM - rot_dim
    c = np.concatenate([cos, cos, np.ones((S, pad), f32)], axis=1)
    lo = np.concatenate([-sin, np.zeros((S, half + pad), f32)], axis=1)
    hi = np.concatenate([np.zeros((S, half), f32), sin, np.zeros((S, pad), f32)], axis=1)
    two = lambda t: np.concatenate([t, t], axis=1)
    return two(c), two(lo), two(hi)


def _in_proj(x3, w_bf):
    B, S, D = x3.shape
    tabs = _rotary_tables(S, ROPE_THETA, ROT_DIM) + _rotary_tables(S, RET_THETA, HEAD_DIM)
    tab_spec = pl.BlockSpec((TM, 128), lambda b, i: (i, 0))
    qkv = 3 * W_DIL
    cls_spec = lambda d: pl.BlockSpec((None, TM // d, d * qkv), lambda b, i: (b, i, 0))
    cls_shape = lambda d: jax.ShapeDtypeStruct((B, S // d, d * qkv), BF16)
    return pl.pallas_call(
        _in_proj_kernel,
        grid=(B, S // TM),
        in_specs=[pl.BlockSpec((None, TM, D), lambda b, i: (b, i, 0)),
                  pl.BlockSpec((D, N_IN), lambda b, i: (0, 0))] + [tab_spec] * 6,
        out_specs=[pl.BlockSpec((None, TM, N_IN), lambda b, i: (b, i, 0)), cls_spec(4), cls_spec(16)],
        out_shape=[jax.ShapeDtypeStruct((B, S, N_IN), BF16), cls_shape(4), cls_shape(16)],
        scratch_shapes=[pltpu.VMEM((GROUP // 128, TM, 128), F32)],
        compiler_params=_cparams("parallel", "parallel"),
        name="in_proj",
    )(x3, w_bf, *tabs)


def _band_attn_kernel(q_ref, kc_ref, kp_ref, vc_ref, vp_ref, o_ref, l_ref,
                      k_all, v_all, s_scr, p_scr, r_scr):
    j = pl.program_id(2)
    tq = q_ref.shape[0]
    nsub = tq // BAND
    pair = 2 * HEAD_DIM
    n_pairs = N_HEADS_DIL // 2
    k_all[0:BAND, :] = kp_ref[...]
    k_all[BAND:, :] = kc_ref[...]
    v_all[0:BAND, :] = vp_ref[...]
    v_all[BAND:, :] = vc_ref[...]
    qi = lax.broadcasted_iota(jnp.int32, (BAND, 2 * BAND), 0)
    kj = lax.broadcasted_iota(jnp.int32, (BAND, 2 * BAND), 1)
    dist = qi + BAND - kj
    band = (dist >= 0) & (dist <= BAND)
    bias = jnp.where(band, 0.0, NEG)
    bias_first = jnp.where(band & ((kj >= BAND) | (j > 0)), 0.0, NEG)
    low = lax.broadcasted_iota(jnp.int32, (BAND, pair), 1) < HEAD_DIM
    zero = jnp.zeros((BAND, pair), BF16)

    for pr in range(n_pairs):
        cs = slice(pr * pair, (pr + 1) * pair)
        for n in range(nsub):
            q = q_ref[n * BAND:(n + 1) * BAND, cs]
            kk = k_all[n * BAND:(n + 2) * BAND, cs]
            b = bias_first if n == 0 else bias
            i = pr * nsub + n
            s_scr[2 * i] = _nt_dot(jnp.where(low, q, zero), kk) + b
            s_scr[2 * i + 1] = _nt_dot(jnp.where(low, zero, q), kk) + b

    for pr in range(n_pairs):
        cs = slice(pr * pair, (pr + 1) * pair)
        for n in range(nsub):
            i = pr * nsub + n
            stats = []
            for a in range(2):
                s = s_scr[2 * i + a]
                m = jnp.max(s, axis=-1, keepdims=True)
                p = jnp.exp(s - m)
                l = jnp.sum(p, axis=-1, keepdims=True)
                p_scr[2 * i + a] = p.astype(BF16)
                stats.append((m, l))
            (m0, l0), (m1, l1) = stats
            r_scr[i] = jnp.where(low, 1.0 / l0, 1.0 / l1)
            l_ref[n * BAND:(n + 1) * BAND, cs] = jnp.where(low, m0 + jnp.log(l0), m1 + jnp.log(l1))

    for pr in range(n_pairs):
        cs = slice(pr * pair, (pr + 1) * pair)
        for n in range(nsub):
            i = pr * nsub + n
            vv = v_all[n * BAND:(n + 2) * BAND, cs]
            o0 = jnp.dot(p_scr[2 * i], vv, preferred_element_type=F32)
            o1 = jnp.dot(p_scr[2 * i + 1], vv, preferred_element_type=F32)
            o_ref[n * BAND:(n + 1) * BAND, cs] = jnp.where(low, o0, o1) * r_scr[i]


def _band_attn(src, B, L, d, gpr):
    tq = min(TQ, L)
    assert L % tq == 0
    sub = tq // BAND
    cur = lambda g: pl.BlockSpec((None, tq, GROUP), lambda b, r, j: (b, j, r * gpr + g))
    prev = lambda g: pl.BlockSpec((None, BAND, GROUP),
                                  lambda b, r, j: (b, jnp.maximum(j * sub - 1, 0), r * gpr + g))
    out_spec = pl.BlockSpec((None, tq, W_DIL), lambda b, r, j: (b, j, r))
    shp = jax.ShapeDtypeStruct((B, L, d * W_DIL), F32)
    n_blk = (N_HEADS_DIL // 2) * sub
    return pl.pallas_call(
        _band_attn_kernel,
        grid=(B, d, L // tq),
        in_specs=[cur(0), cur(1), prev(1), cur(2), prev(2)],
        out_specs=[out_spec, out_spec],
        out_shape=[shp, shp],
        scratch_shapes=[pltpu.VMEM((BAND + tq, GROUP), BF16), pltpu.VMEM((BAND + tq, GROUP), BF16),
                        pltpu.VMEM((2 * n_blk, BAND, 2 * BAND), F32),
                        pltpu.VMEM((2 * n_blk, BAND, 2 * BAND), BF16),
                        pltpu.VMEM((n_blk, BAND, 2 * HEAD_DIM), F32)],
        compiler_params=_cparams("parallel", "parallel", "parallel"),
        name=f"band_attn_d{d}",
    )(src, src, src, src, src)


RET_UNROLL = 32


def _retention_kernel(q_ref, k_ref, v_ref, g_ref, gain_ref, din_ref, kd_ref, qd_ref, cd_ref,
                      o_ref, kv_ref, st_ref):
    C = RET_CHUNK
    U = min(RET_UNROLL, q_ref.shape[0] // C)
    n_iter = q_ref.shape[0] // (C * U)
    pair = 2 * HEAD_DIM
    low = lax.broadcasted_iota(jnp.int32, (C, pair), 1) < HEAD_DIM

    def rows(i, u):
        return pl.ds(pl.multiple_of(i * (C * U), C * U) + u * C, C)

    def kv_body(i, carry):
        for u in range(U):
            kd = (k_ref[rows(i, u), :].astype(F32) * kd_ref[...]).astype(BF16)
            kv_ref[i * U + u] = lax.dot_general(kd, v_ref[rows(i, u), :], (((0,), (0,)), ((), ())),
                                                preferred_element_type=F32)
        return carry

    lax.fori_loop(0, n_iter, kv_body, 0)

    def scan_body(c, state):
        st_ref[c] = state.astype(BF16)
        return state * cd_ref[...] + kv_ref[c]

    lax.fori_loop(0, n_iter * U, scan_body, jnp.zeros((pair, pair), F32))

    def out_body(i, carry):
        for u in range(U):
            q = q_ref[rows(i, u), :]
            k = k_ref[rows(i, u), :]
            v = v_ref[rows(i, u), :]
            prev = st_ref[i * U + u]
            ys = []
            for a in range(2):
                qa = jnp.where(low, q, jnp.zeros_like(q)) if a == 0 else jnp.where(low, jnp.zeros_like(q), q)
                s = _nt_dot(qa, k) * din_ref[a]
                inner = jnp.dot(s.astype(BF16), v, preferred_element_type=F32)
                qd = (qa.astype(F32) * qd_ref[...]).astype(BF16)
                ys.append(inner + jnp.dot(qd, prev, preferred_element_type=F32))
            y = jnp.where(low, ys[0], ys[1])
            inv = 1.0 / HEAD_DIM
            half_sum = lambda t: jnp.where(low, jnp.sum(jnp.where(low, t, 0.0), axis=-1, keepdims=True),
                                           jnp.sum(jnp.where(low, 0.0, t), axis=-1, keepdims=True))
            dlt = y - half_sum(y) * inv
            var = half_sum(dlt * dlt) * inv
            yn = dlt * lax.rsqrt(var + LN_EPS) * gain_ref[...]
            gr = g_ref[rows(i, u), :].astype(F32)
            o_ref[rows(i, u), :] = (yn * (gr / (1.0 + jnp.exp(-gr)))).astype(BF16)
        return carry

    lax.fori_loop(0, n_iter, out_body, 0)


def _retention_tables():
    H, C = N_HEADS_RET, RET_CHUNK
    f32 = np.float32
    log_g = np.log1p(-(f32(2.0) ** (f32(-5.0) - np.arange(H, dtype=f32)))).astype(f32)
    idx = np.arange(C, dtype=f32)
    diff = idx[:, None] - idx[None, :]
    inner_decay = (np.exp(log_g[:, None, None] * np.maximum(diff, f32(0.0))) * (diff >= 0)).astype(f32)
    k_decay = np.exp(log_g[:, None] * (f32(C - 1) - idx)).astype(f32)
    q_decay = np.exp(log_g[:, None] * (idx + f32(1.0))).astype(f32)
    chunk_decay = np.exp(log_g * f32(C)).astype(f32)
    lanes = lambda t: np.repeat(t.reshape(H // 2, 2, C).transpose(0, 2, 1), HEAD_DIM, axis=2)
    cd = np.ascontiguousarray(np.broadcast_to(
        np.repeat(chunk_decay.reshape(H // 2, 2), HEAD_DIM, axis=1)[:, :, None],
        (H // 2, 2 * HEAD_DIM, 2 * HEAD_DIM)))
    return inner_decay, lanes(k_decay), lanes(q_decay), cd


def _retention(h, gain):
    B, S, _ = h.shape
    lanes = 2 * HEAD_DIM
    col0 = 3 * W_DIL // lanes
    per = W_RET // lanes
    col = lambda g: pl.BlockSpec((None, S, lanes), lambda b, p: (b, 0, col0 + g * per + p))
    din, kd, qd, cd = _retention_tables()
    C = RET_CHUNK
    assert S % (C * min(RET_UNROLL, S // C)) == 0
    tab = lambda a: pl.BlockSpec((None, a, lanes), lambda b, p: (p, 0, 0))
    return pl.pallas_call(
        _retention_kernel,
        grid=(B, per),
        in_specs=[col(0), col(1), col(2), col(3),
                  pl.BlockSpec((1, lanes), lambda b, p: (0, p)),
                  pl.BlockSpec((2, C, C), lambda b, p: (p, 0, 0)), tab(C), tab(C), tab(lanes)],
        out_specs=pl.BlockSpec((None, S, lanes), lambda b, p: (b, 0, p)),
        out_shape=jax.ShapeDtypeStruct((B, S, W_RET), BF16),
        scratch_shapes=[pltpu.VMEM((S // C, lanes, lanes), F32), pltpu.VMEM((S // C, lanes, lanes), BF16)],
        compiler_params=_cparams("parallel", "parallel"),
        name="retention",
    )(h, h, h, h, gain.reshape(1, W_RET).astype(F32), din, kd, qd, cd)


def _out_proj_kernel(o1, o4, o16, l1, l4, l16, ret_ref, x_ref, w_ref, g_ref, b_ref, out_ref,
                     so4, so16, sl4, sl16):
    tm = x_ref.shape[0]
    n_slab = W_DIL // 128
    for d, pairs in ((4, ((o4, so4), (l4, sl4))), (16, ((o16, so16), (l16, sl16)))):
        for src, dst in pairs:
            for r in range(d):
                for c in range(n_slab):
                    c0 = r * W_DIL + c * 128
                    dst[c, pl.ds(r, tm // d, stride=d), :] = src[:, c0:c0 + 128]
    atts = []
    for c in range(n_slab):
        cs = slice(c * 128, (c + 1) * 128)
        a1, a4, a16 = l1[:, cs], sl4[c], sl16[c]
        m = jnp.maximum(jnp.maximum(a1, a4), a16)
        e1, e4, e16 = jnp.exp(a1 - m), jnp.exp(a4 - m), jnp.exp(a16 - m)
        att = (e1 * o1[:, cs] + e4 * so4[c] + e16 * so16[c]) / (e1 + e4 + e16)
        atts.append(att.astype(BF16))
    acc = jnp.dot(jnp.concatenate(atts, axis=1), w_ref[:W_DIL, :], preferred_element_type=F32)
    acc += jnp.dot(ret_ref[...], w_ref[W_DIL:, :], preferred_element_type=F32)
    out_ref[...] = _layer_norm(DEEPNORM_ALPHA * x_ref[...] + acc, g_ref[...], b_ref[...])


def _out_proj(os_, ls_, ret, x3, w_bf, g, b):
    B, S, D = x3.shape
    cls = lambda d: pl.BlockSpec((None, TM // d, d * W_DIL), lambda b_, i: (b_, i, 0))
    half = cls(1)
    full = pl.BlockSpec((None, TM, D), lambda b_, i: (b_, i, 0))
    vec = pl.BlockSpec((1, D), lambda b_, i: (0, 0))
    return pl.pallas_call(
        _out_proj_kernel,
        grid=(B, S // TM),
        in_specs=[half, cls(4), cls(16), half, cls(4), cls(16), half, full,
                  pl.BlockSpec((D, D), lambda b_, i: (0, 0)), vec, vec],
        out_specs=full,
        out_shape=jax.ShapeDtypeStruct((B, S, D), F32),
        scratch_shapes=[pltpu.VMEM((W_DIL // 128, TM, 128), F32)] * 4,
        compiler_params=_cparams("parallel", "parallel"),
        name="out_proj_ln1",
    )(*os_, *ls_, ret, x3, w_bf, g.reshape(1, D), b.reshape(1, D))


def _mem_kv_kernel(m_ref, wk_ref, wv_ref, k_ref, v_ref):
    mb = m_ref[...].astype(BF16)
    k_ref[...] = jnp.dot(mb, wk_ref[...], preferred_element_type=F32).astype(BF16)
    v_ref[...] = jnp.dot(mb, wv_ref[...], preferred_element_type=F32).astype(BF16)


def _mem_kv(mem, wk_bf, wv_bf):
    B, M, D = mem.shape
    blk = pl.BlockSpec((None, M, D), lambda b: (b, 0, 0))
    wsp = pl.BlockSpec((D, D), lambda b: (0, 0))
    shp = jax.ShapeDtypeStruct((B, M, D), BF16)
    return pl.pallas_call(
        _mem_kv_kernel, grid=(B,), in_specs=[blk, wsp, wsp], out_specs=[blk, blk], out_shape=[shp, shp],
        compiler_params=_cparams("parallel"), name="mem_kv",
    )(mem, wk_bf, wv_bf)


def _cross_attn_kernel(x_ref, k_ref, v_ref, wq_ref, wo_ref, g_ref, b_ref, out_ref):
    x = x_ref[...]
    hd = x.shape[-1] // N_HEADS_MEM
    q = (jnp.dot(x.astype(BF16), wq_ref[...], preferred_element_type=F32) * (hd ** -0.5)).astype(BF16)
    outs = []
    for hh in range(N_HEADS_MEM):
        cs = slice(hh * hd, (hh + 1) * hd)
        s = _nt_dot(q[:, cs], k_ref[:, cs])
        m = jnp.max(s, axis=-1, keepdims=True)
        p = jnp.exp(s - m)
        l = jnp.sum(p, axis=-1, keepdims=True)
        outs.append((jnp.dot(p.astype(BF16), v_ref[:, cs], preferred_element_type=F32) / l).astype(BF16))
    o = jnp.concatenate(outs, axis=-1)
    c = jnp.dot(o, wo_ref[...], preferred_element_type=F32)
    out_ref[...] = _layer_norm(DEEPNORM_ALPHA * x + c, g_ref[...], b_ref[...])


def _cross_attn(x3, kmem, vmem, wq_bf, wo_bf, g, b):
    B, S, D = x3.shape
    M = kmem.shape[1]
    xs = pl.BlockSpec((None, TM, D), lambda b_, i: (b_, i, 0))
    ms = pl.BlockSpec((None, M, D), lambda b_, i: (b_, 0, 0))
    ws = pl.BlockSpec((D, D), lambda b_, i: (0, 0))
    vs = pl.BlockSpec((1, D), lambda b_, i: (0, 0))
    return pl.pallas_call(
        _cross_attn_kernel,
        grid=(B, S // TM),
        in_specs=[xs, ms, ms, ws, ws, vs, vs],
        out_specs=xs,
        out_shape=jax.ShapeDtypeStruct((B, S, D), F32),
        compiler_params=_cparams("parallel", "parallel"),
        name="cross_attn_ln2",
    )(x3, kmem, vmem, wq_bf, wo_bf, g.reshape(1, D), b.reshape(1, D)).reshape(B * S, D)


def _split_bf16(t):
    hi = t.astype(BF16)
    return hi, (t - hi.astype(F32)).astype(BF16)


def _router_kernel(x_ref, w_ref, b_ref, idx_ref, wt_ref):
    x_hi, x_lo = _split_bf16(x_ref[...])
    by_hi = _nt_dot(w_ref[...], x_hi)
    logits = (by_hi[:N_EXPERTS] + (by_hi[N_EXPERTS:] + _nt_dot(w_ref[:N_EXPERTS, :], x_lo))) + b_ref[:, :1]
    e_iota = lax.broadcasted_iota(jnp.int32, logits.shape, 0)
    cur = logits
    vals, idxs = [], []
    for _ in range(TOP_K):
        m = jnp.max(cur, axis=0, keepdims=True)
        idx = jnp.min(jnp.where(cur == m, e_iota, N_EXPERTS), axis=0, keepdims=True)
        cur = jnp.where(e_iota == idx, -jnp.inf, cur)
        vals.append(m)
        idxs.append(idx)
    es = [jnp.exp(v - vals[0]) for v in vals]
    tot = es[0] + es[1] + es[2] + es[3]
    idx_ref[...] = jnp.concatenate(idxs, axis=0)
    wt_ref[...] = jnp.concatenate([e / tot for e in es], axis=0)


def _router(x2, router_w, router_b):
    N, D = x2.shape
    out = pl.BlockSpec((TOP_K, T_ROUTE), lambda i: (0, i))
    return pl.pallas_call(
        _router_kernel,
        grid=(N // T_ROUTE,),
        in_specs=[pl.BlockSpec((T_ROUTE, D), lambda i: (i, 0)),
                  pl.BlockSpec((2 * N_EXPERTS, D), lambda i: (0, 0)),
                  pl.BlockSpec((N_EXPERTS, 128), lambda i: (0, 0))],
        out_specs=[out, out],
        out_shape=[jax.ShapeDtypeStruct((TOP_K, N), jnp.int32), jax.ShapeDtypeStruct((TOP_K, N), F32)],
        compiler_params=_cparams("parallel"),
        name="router_top4",
    )(x2, jnp.concatenate(_split_bf16(router_w.T.astype(F32)), axis=0),
      jnp.broadcast_to(router_b.astype(F32)[:, None], (N_EXPERTS, 128)))


def _rank_kernel(idx_ref, tri_ref, rank_ref, cnt_ref, carry_ref):
    @pl.when(pl.program_id(0) == 0)
    def _():
        carry_ref[...] = jnp.zeros_like(carry_ref)

    T = tri_ref.shape[0]
    e_iota = lax.broadcasted_iota(jnp.int32, (N_EXPERTS, T), 0)
    carry = carry_ref[:, :1]
    for j in range(idx_ref.shape[1] // T):
        ts = slice(j * T, (j + 1) * T)
        hot = [e_iota == idx_ref[k:k + 1, ts] for k in range(TOP_K)]
        c = sum(h.astype(F32) for h in hot)
        before = jnp.dot(c.astype(BF16), tri_ref[...], preferred_element_type=F32) + carry
        rank_ref[:, ts] = jnp.concatenate(
            [jnp.sum(jnp.where(h, before, 0.0), axis=0, keepdims=True) for h in hot], axis=0).astype(jnp.int32)
        carry = carry + jnp.sum(c, axis=1, keepdims=True)
    carry_ref[...] = jnp.broadcast_to(carry, carry_ref.shape)
    cnt_ref[...] = carry_ref[...].astype(jnp.int32)


def _ranks(idx):
    N = idx.shape[1]
    T = T_RANK
    step = min(RANK_CHUNKS * T, N)
    tri = jnp.asarray(np.arange(T)[:, None] < np.arange(T)[None, :], BF16)
    blk = pl.BlockSpec((TOP_K, step), lambda i: (0, i))
    return pl.pallas_call(
        _rank_kernel,
        grid=(N // step,),
        in_specs=[blk, pl.BlockSpec((T, T), lambda i: (0, 0))],
        out_specs=[blk, pl.BlockSpec((N_EXPERTS, 128), lambda i: (0, 0))],
        out_shape=[jax.ShapeDtypeStruct((TOP_K, N), jnp.int32),
                   jax.ShapeDtypeStruct((N_EXPERTS, 128), jnp.int32)],
        scratch_shapes=[pltpu.VMEM((N_EXPERTS, 128), F32)],
        compiler_params=_cparams("arbitrary"),
        name="expert_ranks",
    )(idx, tri)


def _dest_kernel(idx_ref, rank_ref, start_ref, dest_ref):
    T = idx_ref.shape[1]
    e_iota = lax.broadcasted_iota(jnp.int32, (N_EXPERTS, T), 0)
    start = jnp.tile(start_ref[...], (1, T // 128))
    rows = [jnp.sum(jnp.where(e_iota == idx_ref[k:k + 1, :], start, 0), axis=0, keepdims=True)
            for k in range(TOP_K)]
    dest_ref[...] = jnp.concatenate(rows, axis=0) + rank_ref[...]


def _dests(idx, rank, group_start):
    N = idx.shape[1]
    T = min(T_DEST, N)
    blk = pl.BlockSpec((TOP_K, T), lambda i: (0, i))
    return pl.pallas_call(
        _dest_kernel,
        grid=(N // T,),
        in_specs=[blk, blk, pl.BlockSpec((N_EXPERTS, 128), lambda i: (0, 0))],
        out_specs=blk,
        out_shape=jax.ShapeDtypeStruct((TOP_K, N), jnp.int32),
        compiler_params=_cparams("parallel"),
        name="expert_dests",
    )(idx, rank, jnp.broadcast_to(group_start[:, None], (N_EXPERTS, 128)))


MOVE_UNROLL = 8
SLAB_ROWS = 8


def _for_rows(T, fn):
    def body(g, carry):
        for u in range(MOVE_UNROLL):
            for k in range(TOP_K):
                t = g * MOVE_UNROLL + u
                fn(t, k, k * T + t)
        return carry

    lax.fori_loop(0, T // MOVE_UNROLL, body, 0)


def _slab(ref_at, row8):
    return ref_at.at[pl.ds(pl.multiple_of(row8, SLAB_ROWS), SLAB_ROWS), :]


def _dispatch_kernel(dcur_ref, dprev_ref, x_ref, *refs, n_cast):
    w_refs, xs_ref, wb_refs, (pk_ref, sem) = refs[:n_cast], refs[n_cast], refs[n_cast + 1:-2], refs[-2:]
    for w_ref, wb_ref in zip(w_refs, wb_refs):
        wb_ref[...] = w_ref[...].astype(BF16)
    i = pl.program_id(0)
    n = pl.num_programs(0)
    T = x_ref.shape[0]
    slot = i % 2
    for c in range(SLAB_ROWS):
        pk_ref[slot, pl.ds(c, T, stride=SLAB_ROWS), :] = x_ref[:, c * 128:(c + 1) * 128]

    def row_copy(sl, dref, t, j):
        return pltpu.make_async_copy(_slab(pk_ref.at[sl], t * SLAB_ROWS), _slab(xs_ref, dref[j]), sem.at[sl])

    _for_rows(T, lambda t, k, j: row_copy(slot, dcur_ref, t, j).start(priority=k % 2))

    @pl.when(i > 0)
    def _():
        _for_rows(T, lambda t, k, j: row_copy(1 - slot, dprev_ref, t, j).wait())

    @pl.when(i == n - 1)
    def _():
        _for_rows(T, lambda t, k, j: row_copy(slot, dcur_ref, t, j).wait())


CAST_BLOCK_BYTES = 2 * 1024 * 1024


def _dispatch(x2, dest8, P, weights):
    N, D = x2.shape
    assert D == SLAB_ROWS * 128
    T = T_MOVE
    n = N // T
    flat = [w.reshape(-1, w.shape[-1]) for w in weights]
    fused = all(f.shape[0] % n == 0 and (f.shape[0] // n) % 16 == 0
                and (f.shape[0] // n) * f.shape[1] * 4 <= CAST_BLOCK_BYTES for f in flat)
    if not fused:
        flat = []
    w_specs = [pl.BlockSpec((f.shape[0] // n, f.shape[1]), lambda i: (i, 0)) for f in flat]
    smem = lambda f: pl.BlockSpec((TOP_K * T,), f, memory_space=pltpu.SMEM)
    outs = pl.pallas_call(
        functools.partial(_dispatch_kernel, n_cast=len(flat)),
        grid=(n,),
        in_specs=[smem(lambda i: (i,)), smem(lambda i: (jnp.maximum(i - 1, 0),)),
                  pl.BlockSpec((T, D), lambda i: (i, 0))] + w_specs,
        out_specs=[pl.BlockSpec(memory_space=pl.ANY)] + w_specs,
        out_shape=[jax.ShapeDtypeStruct((P * SLAB_ROWS, 128), F32)]
                  + [jax.ShapeDtypeStruct(f.shape, BF16) for f in flat],
        scratch_shapes=[pltpu.VMEM((2, T * SLAB_ROWS, 128), F32), pltpu.SemaphoreType.DMA((2,))],
        compiler_params=_cparams("arbitrary"),
        name="moe_dispatch",
    )(dest8, dest8, x2, *flat)
    if fused:
        return outs[0], [o.reshape(w.shape) for o, w in zip(outs[1:], weights)]
    return outs[0], [w.astype(BF16) for w in weights]


def _expert_kernel(be_ref, bv_ref, xs_ref, wg_ref, wu_ref, wd_ref, bg_ref, bu_ref, bd_ref, ys_ref):
    j = pl.program_id(0)
    valid = bv_ref[j]

    @pl.when(valid > 0)
    def _():
        bm = xs_ref.shape[0] // SLAB_ROWS
        row = lax.broadcasted_iota(jnp.int32, (bm, 128), 0)
        xb = jnp.concatenate(
            [jnp.where(row < valid, xs_ref[pl.ds(c, bm, stride=SLAB_ROWS), :], 0.0).astype(BF16)
             for c in range(SLAB_ROWS)], axis=1)

        def proj(w_ref, b_ref):
            return jnp.dot(xb, w_ref[...], preferred_element_type=F32) + b_ref[...]

        gate = jnp.minimum(proj(wg_ref, bg_ref), SWIGLU_LIMIT)
        up = jnp.clip(proj(wu_ref, bu_ref), -SWIGLU_LIMIT, SWIGLU_LIMIT)
        hmid = gate * (1.0 / (1.0 + jnp.exp(-SWIGLU_ALPHA * gate))) * (up + 1.0)
        out = jnp.dot(hmid.astype(BF16), wd_ref[...], preferred_element_type=F32) + bd_ref[...]
        for c in range(SLAB_ROWS):
            ys_ref[pl.ds(c, bm, stride=SLAB_ROWS), :] = out[:, c * 128:(c + 1) * 128]


def _experts(xs, block_e, block_valid, wg, wu, wd, bg, bu, bd):
    D = SLAB_ROWS * 128
    F = wg.shape[-1]
    wspec = lambda a, c: pl.BlockSpec((None, a, c), lambda j, be, bv: (be[j], 0, 0))
    slabs = pl.BlockSpec((BM * SLAB_ROWS, 128), lambda j, be, bv: (j, 0))
    grid_spec = pltpu.PrefetchScalarGridSpec(
        num_scalar_prefetch=2,
        grid=(xs.shape[0] // (BM * SLAB_ROWS),),
        in_specs=[slabs, wspec(D, F), wspec(D, F), wspec(F, D), wspec(1, F), wspec(1, F), wspec(1, D)],
        out_specs=slabs,
    )
    return pl.pallas_call(
        _expert_kernel,
        grid_spec=grid_spec,
        out_shape=jax.ShapeDtypeStruct(xs.shape, F32),
        compiler_params=_cparams("arbitrary"),
        name="moe_experts",
    )(block_e, block_valid, xs, wg, wu, wd, bg, bu, bd)


def _combine_kernel(dcur_ref, dnext_ref, x_ref, wt_ref, ys_ref, g_ref, b_ref, out_ref, buf_ref, sem):
    i = pl.program_id(0)
    n = pl.num_programs(0)
    T = x_ref.shape[0]
    slot = i % 2

    def row_copy(sl, dref, t, k, j):
        return pltpu.make_async_copy(_slab(ys_ref, dref[j]), _slab(buf_ref.at[sl], (k * T + t) * SLAB_ROWS),
                                     sem.at[sl])

    @pl.when(i == 0)
    def _():
        _for_rows(T, lambda t, k, j: row_copy(slot, dcur_ref, t, k, j).start(priority=k % 2))

    @pl.when(i + 1 < n)
    def _():
        _for_rows(T, lambda t, k, j: row_copy(1 - slot, dnext_ref, t, k, j).start(priority=k % 2))

    _for_rows(T, lambda t, k, j: row_copy(slot, dcur_ref, t, k, j).wait())

    wt = wt_ref[...]
    wb = [jnp.broadcast_to(wt[:, k:k + 1], (T, 128)) for k in range(TOP_K)]
    ys = []
    for c in range(SLAB_ROWS):
        chunk = lambda k: buf_ref[slot, pl.ds(k * T * SLAB_ROWS + c, T, stride=SLAB_ROWS), :]
        yc = wb[0] * chunk(0)
        for k in range(1, TOP_K):
            yc = yc + wb[k] * chunk(k)
        ys.append(yc)
    y = jnp.concatenate(ys, axis=1)
    out_ref[...] = _layer_norm(DEEPNORM_ALPHA * x_ref[...] + y, g_ref[...], b_ref[...])


def _combine(x2, dest_flat, wt_tok, ys, g, b):
    N, D = x2.shape
    T = T_COMBINE
    n = N // T
    vec = pl.BlockSpec((1, D), lambda i: (0, 0))
    smem = lambda f: pl.BlockSpec((TOP_K * T,), f, memory_space=pltpu.SMEM)
    return pl.pallas_call(
        _combine_kernel,
        grid=(n,),
        in_specs=[smem(lambda i: (i,)), smem(lambda i: (jnp.minimum(i + 1, n - 1),)),
                  pl.BlockSpec((T, D), lambda i: (i, 0)),
                  pl.BlockSpec((T, TOP_K), lambda i: (i, 0)),
                  pl.BlockSpec(memory_space=pl.ANY), vec, vec],
        out_specs=pl.BlockSpec((T, D), lambda i: (i, 0)),
        out_shape=jax.ShapeDtypeStruct((N, D), F32),
        scratch_shapes=[pltpu.VMEM((2, TOP_K * T * SLAB_ROWS, 128), F32), pltpu.SemaphoreType.DMA((2,))],
        compiler_params=_cparams("arbitrary"),
        name="moe_combine_ln3",
    )(dest_flat, dest_flat, x2, wt_tok, ys, g.reshape(1, D), b.reshape(1, D))


def _tile_major(a, T):
    K, N = a.shape
    return a.reshape(K, N // T, T).transpose(1, 0, 2).reshape(N * K)


def _moe(x2, router_w, router_b, w_gate, b_gate, w_up, b_up, w_down, b_down, ln_g, ln_b):
    N, D = x2.shape
    idx, wt = _router(x2, router_w, router_b)
    rank, cnt = _ranks(idx)
    counts = cnt[:, 0]
    blocks = (counts + BM - 1) // BM
    blk_end = jnp.cumsum(blocks)
    blk_start = blk_end - blocks
    dest = _dests(idx, rank, (blk_start * BM).astype(jnp.int32))
    n_blocks = -(-N * TOP_K // BM) + N_EXPERTS
    bi = jnp.arange(n_blocks, dtype=jnp.int32)
    be_raw = jnp.sum(bi[:, None] >= blk_end[None, :], axis=1).astype(jnp.int32)
    last_e = jnp.max(jnp.where(counts > 0, jnp.arange(N_EXPERTS), 0)).astype(jnp.int32)
    block_e = jnp.minimum(be_raw, last_e)
    in_use = bi < blk_end[-1]
    block_valid = jnp.where(in_use, jnp.clip(counts[block_e] - (bi - blk_start[block_e]) * BM, 0, BM), 0)
    dest8 = dest * SLAB_ROWS
    xs, (wg_bf, wu_bf, wd_bf) = _dispatch(x2, _tile_major(dest8, T_MOVE), n_blocks * BM, [w_gate, w_up, w_down])
    f3 = lambda t: t.astype(F32)[:, None, :]
    ys = _experts(xs, block_e, block_valid.astype(jnp.int32), wg_bf, wu_bf, wd_bf,
                  f3(b_gate), f3(b_up), f3(b_down))
    return _combine(x2, _tile_major(dest8, T_COMBINE), wt.T, ys, ln_g, ln_b)


def kernel(x, mem, w_in, ret_norm_g, w_out, ln1_g, ln1_b, mem_wq, mem_wk, mem_wv, mem_wo, ln2_g, ln2_b,
           router_w, router_b, w_gate, b_gate, w_up, b_up, w_down, b_down, ln3_g, ln3_b):
    B, S, D = x.shape
    assert [d for _, d in DILATED_PATTERNS] == [1, 4, 16] and all(w // d == BAND for w, d in DILATED_PATTERNS)
    for l in range(w_in.shape[0]):
        h, qkv4, qkv16 = _in_proj(x, w_in[l].astype(BF16))
        pats = [_band_attn(h, B, S, 1, N_IN // GROUP),
                _band_attn(qkv4, B, S // 4, 4, 3), _band_attn(qkv16, B, S // 16, 16, 3)]
        ret = _retention(h, ret_norm_g[l])
        x1 = _out_proj([p[0] for p in pats], [p[1] for p in pats], ret, x,
                       w_out[l].astype(BF16), ln1_g[l], ln1_b[l])
        kmem, vmem = _mem_kv(mem, mem_wk[l].astype(BF16), mem_wv[l].astype(BF16))
        x2 = _cross_attn(x1, kmem, vmem, mem_wq[l].astype(BF16), mem_wo[l].astype(BF16),
                         ln2_g[l], ln2_b[l])
        x = _moe(x2, router_w[l], router_b[l], w_gate[l], b_gate[l], w_up[l], b_up[l],
                 w_down[l], b_down[l], ln3_g[l], ln3_b[l]).reshape(B, S, D)
    return x
```

```python
import functools

import jax
import jax.numpy as jnp
import numpy as np
from jax import lax
from jax.experimental import pallas as pl
from jax.experimental.pallas import tpu as pltpu

F32 = jnp.float32
BF16 = jnp.bfloat16

HEAD_DIM = 64
N_HEADS_DIL = 8
N_HEADS_RET = 8
W_DIL = N_HEADS_DIL * HEAD_DIM
W_RET = N_HEADS_RET * HEAD_DIM
N_IN = 3 * W_DIL + 4 * W_RET
DILATED_PATTERNS = ((128, 1), (512, 4), (2048, 16))
BAND = 128
ROPE_THETA = 500000.0
ROT_DIM = HEAD_DIM // 4
RET_THETA = 10000.0
RET_CHUNK = 128
N_HEADS_MEM = 4
N_EXPERTS = 32
TOP_K = 4
SWIGLU_LIMIT = 7.0
SWIGLU_ALPHA = 1.702
LN_EPS = 1e-5
DEPTH = 1
DEEPNORM_ALPHA = (2 * DEPTH) ** 0.25

GROUP = 512
TM = 512
TQ = 512
T_ROUTE = 1024
T_RANK = 512
RANK_CHUNKS = 4
T_DEST = 4096
T_MOVE = 512
T_COMBINE = 256
BM = 1024
OUT_SPLIT = 2
VMEM_LIMIT = 56 * 1024 * 1024
NEG = -1e30


def _cparams(*sem):
    return pltpu.CompilerParams(dimension_semantics=sem, vmem_limit_bytes=VMEM_LIMIT)


def _nt_dot(a, b):
    return lax.dot_general(a, b, (((1,), (1,)), ((), ())), preferred_element_type=F32)


def _layer_norm(y, g, b):
    mu = jnp.mean(y, axis=-1, keepdims=True)
    d = y - mu
    var = jnp.mean(d * d, axis=-1, keepdims=True)
    return d * lax.rsqrt(var + LN_EPS) * g + b


def _in_proj_kernel(x_ref, w_ref, ca_ref, la_ref, ha_ref, cr_ref, lr_ref, hr_ref, o_ref, o4_ref, o16_ref,
                    acc_ref):
    xb = x_ref[...].astype(BF16)
    rep = GROUP // 128
    tm = x_ref.shape[0]
    qkv = 3 * W_DIL

    def rot(acc, c_ref, lo_ref, hi_ref, half):
        c = jnp.tile(c_ref[...], (1, rep))
        lo = jnp.tile(lo_ref[...], (1, rep))
        hi = jnp.tile(hi_ref[...], (1, rep))
        up = pltpu.roll(acc, GROUP - half, axis=1)
        dn = pltpu.roll(acc, half, axis=1)
        return acc * c + up * lo + dn * hi

    for g in range(N_IN // GROUP):
        acc = jnp.dot(xb, w_ref[:, g * GROUP:(g + 1) * GROUP], preferred_element_type=F32)
        if g == 0:
            acc = rot(acc, ca_ref, la_ref, ha_ref, ROT_DIM // 2) * (HEAD_DIM ** -0.5)
        elif g == 1:
            acc = rot(acc, ca_ref, la_ref, ha_ref, ROT_DIM // 2)
        elif g == 3:
            acc = rot(acc, cr_ref, lr_ref, hr_ref, HEAD_DIM // 2)
        elif g == 4:
            acc = rot(acc, cr_ref, lr_ref, hr_ref, HEAD_DIM // 2) * (HEAD_DIM ** -0.5)
        o_ref[:, g * GROUP:(g + 1) * GROUP] = acc.astype(BF16)
        if g < 3:
            for c in range(rep):
                acc_ref[c] = acc[:, c * 128:(c + 1) * 128]
            for d, od_ref in ((4, o4_ref), (16, o16_ref)):
                for r in range(d):
                    for c in range(rep):
                        c0 = r * qkv + g * GROUP + c * 128
                        od_ref[:, c0:c0 + 128] = acc_ref[c, pl.ds(r, tm // d, stride=d), :].astype(BF16)


def _rotary_tables(S, theta, rot_dim):
    half = rot_dim // 2
    f32 = np.float32
    inv = f32(1.0) / (f32(theta) ** (np.arange(half, dtype=f32) / f32(half)))
    ang = np.arange(S, dtype=f32)[:, None] * inv[None, :]
    cos, sin = np.cos(ang).astype(f32), np.sin(ang).astype(f32)
    pad = HEAD_DIM - rot_dim
    c = np.concatenate([cos, cos, np.ones((S, pad), f32)], axis=1)
    lo = np.concatenate([-sin, np.zeros((S, half + pad), f32)], axis=1)
    hi = np.concatenate([np.zeros((S, half), f32), sin, np.zeros((S, pad), f32)], axis=1)
    two = lambda t: np.concatenate([t, t], axis=1)
    return two(c), two(lo), two(hi)


def _in_proj(x3, w_bf):
    B, S, D = x3.shape
    tabs = _rotary_tables(S, ROPE_THETA, ROT_DIM) + _rotary_tables(S, RET_THETA, HEAD_DIM)
    tab_spec = pl.BlockSpec((TM, 128), lambda b, i: (i, 0))
    qkv = 3 * W_DIL
    cls_spec = lambda d: pl.BlockSpec((None, TM // d, d * qkv), lambda b, i: (b, i, 0))
    cls_shape = lambda d: jax.ShapeDtypeStruct((B, S // d, d * qkv), BF16)
    return pl.pallas_call(
        _in_proj_kernel,
        grid=(B, S // TM),
        in_specs=[pl.BlockSpec((None, TM, D), lambda b, i: (b, i, 0)),
                  pl.BlockSpec((D, N_IN), lambda b, i: (0, 0))] + [tab_spec] * 6,
        out_specs=[pl.BlockSpec((None, TM, N_IN), lambda b, i: (b, i, 0)), cls_spec(4), cls_spec(16)],
        out_shape=[jax.ShapeDtypeStruct((B, S, N_IN), BF16), cls_shape(4), cls_shape(16)],
        scratch_shapes=[pltpu.VMEM((GROUP // 128, TM, 128), F32)],
        compiler_params=_cparams("parallel", "parallel"),
        name="in_proj",
    )(x3, w_bf, *tabs)


def _band_attn_kernel(q_ref, kc_ref, kp_ref, vc_ref, vp_ref, o_ref, l_ref,
                      k_all, v_all, s_scr, p_scr, r_scr):
    j = pl.program_id(2)
    tq = q_ref.shape[0]
    nsub = tq // BAND
    pair = 2 * HEAD_DIM
    n_pairs = N_HEADS_DIL // 2
    k_all[0:BAND, :] = kp_ref[...]
    k_all[BAND:, :] = kc_ref[...]
    v_all[0:BAND, :] = vp_ref[...]
    v_all[BAND:, :] = vc_ref[...]
    qi = lax.broadcasted_iota(jnp.int32, (BAND, 2 * BAND), 0)
    kj = lax.broadcasted_iota(jnp.int32, (BAND, 2 * BAND), 1)
    dist = qi + BAND - kj
    band = (dist >= 0) & (dist <= BAND)
    bias = jnp.where(band, 0.0, NEG)
    bias_first = jnp.where(band & ((kj >= BAND) | (j > 0)), 0.0, NEG)
    low = lax.broadcasted_iota(jnp.int32, (BAND, pair), 1) < HEAD_DIM
    zero = jnp.zeros((BAND, pair), BF16)

    for pr in range(n_pairs):
        cs = slice(pr * pair, (pr + 1) * pair)
        for n in range(nsub):
            q = q_ref[n * BAND:(n + 1) * BAND, cs]
            kk = k_all[n * BAND:(n + 2) * BAND, cs]
            b = bias_first if n == 0 else bias
            i = pr * nsub + n
            s_scr[2 * i] = _nt_dot(jnp.where(low, q, zero), kk) + b
            s_scr[2 * i + 1] = _nt_dot(jnp.where(low, zero, q), kk) + b

    for pr in range(n_pairs):
        cs = slice(pr * pair, (pr + 1) * pair)
        for n in range(nsub):
            i = pr * nsub + n
            stats = []
            for a in range(2):
                s = s_scr[2 * i + a]
                m = jnp.max(s, axis=-1, keepdims=True)
                p = jnp.exp(s - m)
                l = jnp.sum(p, axis=-1, keepdims=True)
                p_scr[2 * i + a] = p.astype(BF16)
                stats.append((m, l))
            (m0, l0), (m1, l1) = stats
            r_scr[i] = jnp.where(low, 1.0 / l0, 1.0 / l1)
            l_ref[n * BAND:(n + 1) * BAND, cs] = jnp.where(low, m0 + jnp.log(l0), m1 + jnp.log(l1))

    for pr in range(n_pairs):
        cs = slice(pr * pair, (pr + 1) * pair)
        for n in range(nsub):
            i = pr * nsub + n
            vv = v_all[n * BAND:(n + 2) * BAND, cs]
            o0 = jnp.dot(p_scr[2 * i], vv, preferred_element_type=F32)
            o1 = jnp.dot(p_scr[2 * i + 1], vv, preferred_element_type=F32)
            o_ref[n * BAND:(n + 1) * BAND, cs] = jnp.where(low, o0, o1) * r_scr[i]


def _band_attn(src, B, L, d, gpr):
    tq = min(TQ, L)
    assert L % tq == 0
    sub = tq // BAND
    cur = lambda g: pl.BlockSpec((None, tq, GROUP), lambda b, r, j: (b, j, r * gpr + g))
    prev = lambda g: pl.BlockSpec((None, BAND, GROUP),
                                  lambda b, r, j: (b, jnp.maximum(j * sub - 1, 0), r * gpr + g))
    out_spec = pl.BlockSpec((None, tq, W_DIL), lambda b, r, j: (b, j, r))
    shp = jax.ShapeDtypeStruct((B, L, d * W_DIL), F32)
    n_blk = (N_HEADS_DIL // 2) * sub
    return pl.pallas_call(
        _band_attn_kernel,
        grid=(B, d, L // tq),
        in_specs=[cur(0), cur(1), prev(1), cur(2), prev(2)],
        out_specs=[out_spec, out_spec],
        out_shape=[shp, shp],
        scratch_shapes=[pltpu.VMEM((BAND + tq, GROUP), BF16), pltpu.VMEM((BAND + tq, GROUP), BF16),
                        pltpu.VMEM((2 * n_blk, BAND, 2 * BAND), F32),
                        pltpu.VMEM((2 * n_blk, BAND, 2 * BAND), BF16),
                        pltpu.VMEM((n_blk, BAND, 2 * HEAD_DIM), F32)],
        compiler_params=_cparams("parallel", "parallel", "parallel"),
        name=f"band_attn_d{d}",
    )(src, src, src, src, src)


RET_UNROLL = 32


def _retention_kernel(q_ref, k_ref, v_ref, g_ref, gain_ref, din_ref, kd_ref, qd_ref, cd_ref,
                      o_ref, kv_ref, st_ref):
    C = RET_CHUNK
    U = min(RET_UNROLL, q_ref.shape[0] // C)
    n_iter = q_ref.shape[0] // (C * U)
    pair = 2 * HEAD_DIM
    low = lax.broadcasted_iota(jnp.int32, (C, pair), 1) < HEAD_DIM

    def rows(i, u):
        return pl.ds(pl.multiple_of(i * (C * U), C * U) + u * C, C)

    def kv_body(i, carry):
        for u in range(U):
            kd = (k_ref[rows(i, u), :].astype(F32) * kd_ref[...]).astype(BF16)
            kv_ref[i * U + u] = lax.dot_general(kd, v_ref[rows(i, u), :], (((0,), (0,)), ((), ())),
                                                preferred_element_type=F32)
        return carry

    lax.fori_loop(0, n_iter, kv_body, 0)

    def scan_body(c, state):
        st_ref[c] = state.astype(BF16)
        return state * cd_ref[...] + kv_ref[c]

    lax.fori_loop(0, n_iter * U, scan_body, jnp.zeros((pair, pair), F32))

    def out_body(i, carry):
        for u in range(U):
            q = q_ref[rows(i, u), :]
            k = k_ref[rows(i, u), :]
            v = v_ref[rows(i, u), :]
            prev = st_ref[i * U + u]
            ys = []
            for a in range(2):
                qa = jnp.where(low, q, jnp.zeros_like(q)) if a == 0 else jnp.where(low, jnp.zeros_like(q), q)
                s = _nt_dot(qa, k) * din_ref[a]
                inner = jnp.dot(s.astype(BF16), v, preferred_element_type=F32)
                qd = (qa.astype(F32) * qd_ref[...]).astype(BF16)
                ys.append(inner + jnp.dot(qd, prev, preferred_element_type=F32))
            y = jnp.where(low, ys[0], ys[1])
            inv = 1.0 / HEAD_DIM
            half_sum = lambda t: jnp.where(low, jnp.sum(jnp.where(low, t, 0.0), axis=-1, keepdims=True),
                                           jnp.sum(jnp.where(low, 0.0, t), axis=-1, keepdims=True))
            dlt = y - half_sum(y) * inv
            var = half_sum(dlt * dlt) * inv
            yn = dlt * lax.rsqrt(var + LN_EPS) * gain_ref[...]
            gr = g_ref[rows(i, u), :].astype(F32)
            o_ref[rows(i, u), :] = (yn * (gr / (1.0 + jnp.exp(-gr)))).astype(BF16)
        return carry

    lax.fori_loop(0, n_iter, out_body, 0)


def _retention_tables():
    H, C = N_HEADS_RET, RET_CHUNK
    f32 = np.float32
    log_g = np.log1p(-(f32(2.0) ** (f32(-5.0) - np.arange(H, dtype=f32)))).astype(f32)
    idx = np.arange(C, dtype=f32)
    diff = idx[:, None] - idx[None, :]
    inner_decay = (np.exp(log_g[:, None, None] * np.maximum(diff, f32(0.0))) * (diff >= 0)).astype(f32)
    k_decay = np.exp(log_g[:, None] * (f32(C - 1) - idx)).astype(f32)
    q_decay = np.exp(log_g[:, None] * (idx + f32(1.0))).astype(f32)
    chunk_decay = np.exp(log_g * f32(C)).astype(f32)
    lanes = lambda t: np.repeat(t.reshape(H // 2, 2, C).transpose(0, 2, 1), HEAD_DIM, axis=2)
    cd = np.ascontiguousarray(np.broadcast_to(
        np.repeat(chunk_decay.reshape(H // 2, 2), HEAD_DIM, axis=1)[:, :, None],
        (H // 2, 2 * HEAD_DIM, 2 * HEAD_DIM)))
    return inner_decay, lanes(k_decay), lanes(q_decay), cd


def _retention(h, gain):
    B, S, _ = h.shape
    lanes = 2 * HEAD_DIM
    col0 = 3 * W_DIL // lanes
    per = W_RET // lanes
    col = lambda g: pl.BlockSpec((None, S, lanes), lambda b, p: (b, 0, col0 + g * per + p))
    din, kd, qd, cd = _retention_tables()
    C = RET_CHUNK
    assert S % (C * min(RET_UNROLL, S // C)) == 0
    tab = lambda a: pl.BlockSpec((None, a, lanes), lambda b, p: (p, 0, 0))
    return pl.pallas_call(
        _retention_kernel,
        grid=(B, per),
        in_specs=[col(0), col(1), col(2), col(3),
                  pl.BlockSpec((1, lanes), lambda b, p: (0, p)),
                  pl.BlockSpec((2, C, C), lambda b, p: (p, 0, 0)), tab(C), tab(C), tab(lanes)],
        out_specs=pl.BlockSpec((None, S, lanes), lambda b, p: (b, 0, p)),
        out_shape=jax.ShapeDtypeStruct((B, S, W_RET), BF16),
        scratch_shapes=[pltpu.VMEM((S // C, lanes, lanes), F32), pltpu.VMEM((S // C, lanes, lanes), BF16)],
        compiler_params=_cparams("parallel", "parallel"),
        name="retention",
    )(h, h, h, h, gain.reshape(1, W_RET).astype(F32), din, kd, qd, cd)


def _out_proj_kernel(o1, o4, o16, l1, l4, l16, ret_ref, x_ref, w_ref, g_ref, b_ref, out_ref,
                     so4, so16, sl4, sl16):
    tm = x_ref.shape[0]
    n_slab = W_DIL // 128
    for d, pairs in ((4, ((o4, so4), (l4, sl4))), (16, ((o16, so16), (l16, sl16)))):
        for src, dst in pairs:
            for r in range(d):
                for c in range(n_slab):
                    c0 = r * W_DIL + c * 128
                    dst[c, pl.ds(r, tm // d, stride=d), :] = src[:, c0:c0 + 128]
    rows = tm // OUT_SPLIT
    for part in range(OUT_SPLIT):
        rs = slice(part * rows, (part + 1) * rows)
        atts = []
        for c in range(n_slab):
            cs = slice(c * 128, (c + 1) * 128)
            a1, a4, a16 = l1[rs, cs], sl4[c, rs, :], sl16[c, rs, :]
            m = jnp.maximum(jnp.maximum(a1, a4), a16)
            e1, e4, e16 = jnp.exp(a1 - m), jnp.exp(a4 - m), jnp.exp(a16 - m)
            att = (e1 * o1[rs, cs] + e4 * so4[c, rs, :] + e16 * so16[c, rs, :]) / (e1 + e4 + e16)
            atts.append(att.astype(BF16))
        acc = jnp.dot(jnp.concatenate(atts, axis=1), w_ref[:W_DIL, :], preferred_element_type=F32)
        acc += jnp.dot(ret_ref[rs, :], w_ref[W_DIL:, :], preferred_element_type=F32)
        out_ref[rs, :] = _layer_norm(DEEPNORM_ALPHA * x_ref[rs, :] + acc, g_ref[...], b_ref[...])


def _out_proj(os_, ls_, ret, x3, w_bf, g, b):
    B, S, D = x3.shape
    cls = lambda d: pl.BlockSpec((None, TM // d, d * W_DIL), lambda b_, i: (b_, i, 0))
    half = cls(1)
    full = pl.BlockSpec((None, TM, D), lambda b_, i: (b_, i, 0))
    vec = pl.BlockSpec((1, D), lambda b_, i: (0, 0))
    return pl.pallas_call(
        _out_proj_kernel,
        grid=(B, S // TM),
        in_specs=[half, cls(4), cls(16), half, cls(4), cls(16), half, full,
                  pl.BlockSpec((D, D), lambda b_, i: (0, 0)), vec, vec],
        out_specs=full,
        out_shape=jax.ShapeDtypeStruct((B, S, D), F32),
        scratch_shapes=[pltpu.VMEM((W_DIL // 128, TM, 128), F32)] * 4,
        compiler_params=_cparams("parallel", "parallel"),
        name="out_proj_ln1",
    )(*os_, *ls_, ret, x3, w_bf, g.reshape(1, D), b.reshape(1, D))


def _mem_kv_kernel(m_ref, wk_ref, wv_ref, k_ref, v_ref):
    mb = m_ref[...].astype(BF16)
    k_ref[...] = jnp.dot(mb, wk_ref[...], preferred_element_type=F32).astype(BF16)
    v_ref[...] = jnp.dot(mb, wv_ref[...], preferred_element_type=F32).astype(BF16)


def _mem_kv(mem, wk_bf, wv_bf):
    B, M, D = mem.shape
    blk = pl.BlockSpec((None, M, D), lambda b: (b, 0, 0))
    wsp = pl.BlockSpec((D, D), lambda b: (0, 0))
    shp = jax.ShapeDtypeStruct((B, M, D), BF16)
    return pl.pallas_call(
        _mem_kv_kernel, grid=(B,), in_specs=[blk, wsp, wsp], out_specs=[blk, blk], out_shape=[shp, shp],
        compiler_params=_cparams("parallel"), name="mem_kv",
    )(mem, wk_bf, wv_bf)


def _cross_attn_kernel(x_ref, k_ref, v_ref, wq_ref, wo_ref, g_ref, b_ref, out_ref):
    x = x_ref[...]
    hd = x.shape[-1] // N_HEADS_MEM
    q = (jnp.dot(x.astype(BF16), wq_ref[...], preferred_element_type=F32) * (hd ** -0.5)).astype(BF16)
    outs = []
    for hh in range(N_HEADS_MEM):
        cs = slice(hh * hd, (hh + 1) * hd)
        s = _nt_dot(q[:, cs], k_ref[:, cs])
        m = jnp.max(s, axis=-1, keepdims=True)
        p = jnp.exp(s - m)
        l = jnp.sum(p, axis=-1, keepdims=True)
        outs.append((jnp.dot(p.astype(BF16), v_ref[:, cs], preferred_element_type=F32) / l).astype(BF16))
    o = jnp.concatenate(outs, axis=-1)
    c = jnp.dot(o, wo_ref[...], preferred_element_type=F32)
    out_ref[...] = _layer_norm(DEEPNORM_ALPHA * x + c, g_ref[...], b_ref[...])


def _cross_attn(x3, kmem, vmem, wq_bf, wo_bf, g, b):
    B, S, D = x3.shape
    M = kmem.shape[1]
    xs = pl.BlockSpec((None, TM, D), lambda b_, i: (b_, i, 0))
    ms = pl.BlockSpec((None, M, D), lambda b_, i: (b_, 0, 0))
    ws = pl.BlockSpec((D, D), lambda b_, i: (0, 0))
    vs = pl.BlockSpec((1, D), lambda b_, i: (0, 0))
    return pl.pallas_call(
        _cross_attn_kernel,
        grid=(B, S // TM),
        in_specs=[xs, ms, ms, ws, ws, vs, vs],
        out_specs=xs,
        out_shape=jax.ShapeDtypeStruct((B, S, D), F32),
        compiler_params=_cparams("parallel", "parallel"),
        name="cross_attn_ln2",
    )(x3, kmem, vmem, wq_bf, wo_bf, g.reshape(1, D), b.reshape(1, D)).reshape(B * S, D)


def _split_bf16(t):
    hi = t.astype(BF16)
    return hi, (t - hi.astype(F32)).astype(BF16)


def _router_kernel(x_ref, w_ref, b_ref, idx_ref, wt_ref):
    x_hi, x_lo = _split_bf16(x_ref[...])
    by_hi = _nt_dot(w_ref[...], x_hi)
    logits = (by_hi[:N_EXPERTS] + (by_hi[N_EXPERTS:] + _nt_dot(w_ref[:N_EXPERTS, :], x_lo))) + b_ref[:, :1]
    e_iota = lax.broadcasted_iota(jnp.int32, logits.shape, 0)
    cur = logits
    vals, idxs = [], []
    for _ in range(TOP_K):
        m = jnp.max(cur, axis=0, keepdims=True)
        idx = jnp.min(jnp.where(cur == m, e_iota, N_EXPERTS), axis=0, keepdims=True)
        cur = jnp.where(e_iota == idx, -jnp.inf, cur)
        vals.append(m)
        idxs.append(idx)
    es = [jnp.exp(v - vals[0]) for v in vals]
    tot = es[0] + es[1] + es[2] + es[3]
    idx_ref[...] = jnp.concatenate(idxs, axis=0)
    wt_ref[...] = jnp.concatenate([e / tot for e in es], axis=0)


def _router(x2, router_w, router_b):
    N, D = x2.shape
    out = pl.BlockSpec((TOP_K, T_ROUTE), lambda i: (0, i))
    return pl.pallas_call(
        _router_kernel,
        grid=(N // T_ROUTE,),
        in_specs=[pl.BlockSpec((T_ROUTE, D), lambda i: (i, 0)),
                  pl.BlockSpec((2 * N_EXPERTS, D), lambda i: (0, 0)),
                  pl.BlockSpec((N_EXPERTS, 128), lambda i: (0, 0))],
        out_specs=[out, out],
        out_shape=[jax.ShapeDtypeStruct((TOP_K, N), jnp.int32), jax.ShapeDtypeStruct((TOP_K, N), F32)],
        compiler_params=_cparams("parallel"),
        name="router_top4",
    )(x2, jnp.concatenate(_split_bf16(router_w.T.astype(F32)), axis=0),
      jnp.broadcast_to(router_b.astype(F32)[:, None], (N_EXPERTS, 128)))


def _rank_kernel(idx_ref, tri_ref, rank_ref, cnt_ref, carry_ref):
    @pl.when(pl.program_id(0) == 0)
    def _():
        carry_ref[...] = jnp.zeros_like(carry_ref)

    T = tri_ref.shape[0]
    e_iota = lax.broadcasted_iota(jnp.int32, (N_EXPERTS, T), 0)
    carry = carry_ref[:, :1]
    for j in range(idx_ref.shape[1] // T):
        ts = slice(j * T, (j + 1) * T)
        hot = [e_iota == idx_ref[k:k + 1, ts] for k in range(TOP_K)]
        c = sum(h.astype(F32) for h in hot)
        before = jnp.dot(c.astype(BF16), tri_ref[...], preferred_element_type=F32) + carry
        rank_ref[:, ts] = jnp.concatenate(
            [jnp.sum(jnp.where(h, before, 0.0), axis=0, keepdims=True) for h in hot], axis=0).astype(jnp.int32)
        carry = carry + jnp.sum(c, axis=1, keepdims=True)
    carry_ref[...] = jnp.broadcast_to(carry, carry_ref.shape)
    cnt_ref[...] = carry_ref[...].astype(jnp.int32)


def _ranks(idx):
    N = idx.shape[1]
    T = T_RANK
    step = min(RANK_CHUNKS * T, N)
    tri = jnp.asarray(np.arange(T)[:, None] < np.arange(T)[None, :], BF16)
    blk = pl.BlockSpec((TOP_K, step), lambda i: (0, i))
    return pl.pallas_call(
        _rank_kernel,
        grid=(N // step,),
        in_specs=[blk, pl.BlockSpec((T, T), lambda i: (0, 0))],
        out_specs=[blk, pl.BlockSpec((N_EXPERTS, 128), lambda i: (0, 0))],
        out_shape=[jax.ShapeDtypeStruct((TOP_K, N), jnp.int32),
                   jax.ShapeDtypeStruct((N_EXPERTS, 128), jnp.int32)],
        scratch_shapes=[pltpu.VMEM((N_EXPERTS, 128), F32)],
        compiler_params=_cparams("arbitrary"),
        name="expert_ranks",
    )(idx, tri)


def _dest_kernel(idx_ref, rank_ref, start_ref, dest_ref):
    T = idx_ref.shape[1]
    e_iota = lax.broadcasted_iota(jnp.int32, (N_EXPERTS, T), 0)
    start = jnp.tile(start_ref[...], (1, T // 128))
    rows = [jnp.sum(jnp.where(e_iota == idx_ref[k:k + 1, :], start, 0), axis=0, keepdims=True)
            for k in range(TOP_K)]
    dest_ref[...] = jnp.concatenate(rows, axis=0) + rank_ref[...]


def _dests(idx, rank, group_start):
    N = idx.shape[1]
    T = min(T_DEST, N)
    blk = pl.BlockSpec((TOP_K, T), lambda i: (0, i))
    return pl.pallas_call(
        _dest_kernel,
        grid=(N // T,),
        in_specs=[blk, blk, pl.BlockSpec((N_EXPERTS, 128), lambda i: (0, 0))],
        out_specs=blk,
        out_shape=jax.ShapeDtypeStruct((TOP_K, N), jnp.int32),
        compiler_params=_cparams("parallel"),
        name="expert_dests",
    )(idx, rank, jnp.broadcast_to(group_start[:, None], (N_EXPERTS, 128)))


MOVE_UNROLL = 8
SLAB_ROWS = 8


def _for_rows(T, fn):
    def body(g, carry):
        for u in range(MOVE_UNROLL):
            for k in range(TOP_K):
                t = g * MOVE_UNROLL + u
                fn(t, k, k * T + t)
        return carry

    lax.fori_loop(0, T // MOVE_UNROLL, body, 0)


def _slab(ref_at, row8):
    return ref_at.at[pl.ds(pl.multiple_of(row8, SLAB_ROWS), SLAB_ROWS), :]


def _dispatch_kernel(dcur_ref, dprev_ref, x_ref, *refs, n_cast):
    w_refs, xs_ref, wb_refs, (pk_ref, sem) = refs[:n_cast], refs[n_cast], refs[n_cast + 1:-2], refs[-2:]
    for w_ref, wb_ref in zip(w_refs, wb_refs):
        wb_ref[...] = w_ref[...].astype(BF16)
    i = pl.program_id(0)
    n = pl.num_programs(0)
    T = x_ref.shape[0]
    slot = i % 2
    for c in range(SLAB_ROWS):
        pk_ref[slot, pl.ds(c, T, stride=SLAB_ROWS), :] = x_ref[:, c * 128:(c + 1) * 128]

    def row_copy(sl, dref, t, j):
        return pltpu.make_async_copy(_slab(pk_ref.at[sl], t * SLAB_ROWS), _slab(xs_ref, dref[j]), sem.at[sl])

    _for_rows(T, lambda t, k, j: row_copy(slot, dcur_ref, t, j).start(priority=k % 2))

    @pl.when(i > 0)
    def _():
        _for_rows(T, lambda t, k, j: row_copy(1 - slot, dprev_ref, t, j).wait())

    @pl.when(i == n - 1)
    def _():
        _for_rows(T, lambda t, k, j: row_copy(slot, dcur_ref, t, j).wait())


CAST_BLOCK_BYTES = 2 * 1024 * 1024


def _dispatch(x2, dest8, P, weights):
    N, D = x2.shape
    assert D == SLAB_ROWS * 128
    T = T_MOVE
    n = N // T
    flat = [w.reshape(-1, w.shape[-1]) for w in weights]
    fused = all(f.shape[0] % n == 0 and (f.shape[0] // n) % 16 == 0
                and (f.shape[0] // n) * f.shape[1] * 4 <= CAST_BLOCK_BYTES for f in flat)
    if not fused:
        flat = []
    w_specs = [pl.BlockSpec((f.shape[0] // n, f.shape[1]), lambda i: (i, 0)) for f in flat]
    smem = lambda f: pl.BlockSpec((TOP_K * T,), f, memory_space=pltpu.SMEM)
    outs = pl.pallas_call(
        functools.partial(_dispatch_kernel, n_cast=len(flat)),
        grid=(n,),
        in_specs=[smem(lambda i: (i,)), smem(lambda i: (jnp.maximum(i - 1, 0),)),
                  pl.BlockSpec((T, D), lambda i: (i, 0))] + w_specs,
        out_specs=[pl.BlockSpec(memory_space=pl.ANY)] + w_specs,
        out_shape=[jax.ShapeDtypeStruct((P * SLAB_ROWS, 128), F32)]
                  + [jax.ShapeDtypeStruct(f.shape, BF16) for f in flat],
        scratch_shapes=[pltpu.VMEM((2, T * SLAB_ROWS, 128), F32), pltpu.SemaphoreType.DMA((2,))],
        compiler_params=_cparams("arbitrary"),
        name="moe_dispatch",
    )(dest8, dest8, x2, *flat)
    if fused:
        return outs[0], [o.reshape(w.shape) for o, w in zip(outs[1:], weights)]
    return outs[0], [w.astype(BF16) for w in weights]


def _expert_kernel(be_ref, bv_ref, xs_ref, wg_ref, wu_ref, wd_ref, bg_ref, bu_ref, bd_ref, ys_ref):
    j = pl.program_id(0)
    valid = bv_ref[j]

    @pl.when(valid > 0)
    def _():
        bm = xs_ref.shape[0] // SLAB_ROWS
        row = lax.broadcasted_iota(jnp.int32, (bm, 128), 0)
        xb = jnp.concatenate(
            [jnp.where(row < valid, xs_ref[pl.ds(c, bm, stride=SLAB_ROWS), :], 0.0).astype(BF16)
             for c in range(SLAB_ROWS)], axis=1)

        def proj(w_ref, b_ref):
            return jnp.dot(xb, w_ref[...], preferred_element_type=F32) + b_ref[...]

        gate = jnp.minimum(proj(wg_ref, bg_ref), SWIGLU_LIMIT)
        up = jnp.clip(proj(wu_ref, bu_ref), -SWIGLU_LIMIT, SWIGLU_LIMIT)
        hmid = gate * (1.0 / (1.0 + jnp.exp(-SWIGLU_ALPHA * gate))) * (up + 1.0)
        out = jnp.dot(hmid.astype(BF16), wd_ref[...], preferred_element_type=F32) + bd_ref[...]
        for c in range(SLAB_ROWS):
            ys_ref[pl.ds(c, bm, stride=SLAB_ROWS), :] = out[:, c * 128:(c + 1) * 128]


def _experts(xs, block_e, block_valid, wg, wu, wd, bg, bu, bd):
    D = SLAB_ROWS * 128
    F = wg.shape[-1]
    wspec = lambda a, c: pl.BlockSpec((None, a, c), lambda j, be, bv: (be[j], 0, 0))
    slabs = pl.BlockSpec((BM * SLAB_ROWS, 128), lambda j, be, bv: (j, 0))
    grid_spec = pltpu.PrefetchScalarGridSpec(
        num_scalar_prefetch=2,
        grid=(xs.shape[0] // (BM * SLAB_ROWS),),
        in_specs=[slabs, wspec(D, F), wspec(D, F), wspec(F, D), wspec(1, F), wspec(1, F), wspec(1, D)],
        out_specs=slabs,
    )
    return pl.pallas_call(
        _expert_kernel,
        grid_spec=grid_spec,
        out_shape=jax.ShapeDtypeStruct(xs.shape, F32),
        compiler_params=_cparams("arbitrary"),
        name="moe_experts",
    )(block_e, block_valid, xs, wg, wu, wd, bg, bu, bd)


def _combine_kernel(dcur_ref, dnext_ref, x_ref, wt_ref, ys_ref, g_ref, b_ref, out_ref, buf_ref, sem):
    i = pl.program_id(0)
    n = pl.num_programs(0)
    T = x_ref.shape[0]
    slot = i % 2

    def row_copy(sl, dref, t, k, j):
        return pltpu.make_async_copy(_slab(ys_ref, dref[j]), _slab(buf_ref.at[sl], (k * T + t) * SLAB_ROWS),
                                     sem.at[sl])

    @pl.when(i == 0)
    def _():
        _for_rows(T, lambda t, k, j: row_copy(slot, dcur_ref, t, k, j).start(priority=k % 2))

    @pl.when(i + 1 < n)
    def _():
        _for_rows(T, lambda t, k, j: row_copy(1 - slot, dnext_ref, t, k, j).start(priority=k % 2))

    _for_rows(T, lambda t, k, j: row_copy(slot, dcur_ref, t, k, j).wait())

    wt = wt_ref[...]
    wb = [jnp.broadcast_to(wt[:, k:k + 1], (T, 128)) for k in range(TOP_K)]
    ys = []
    for c in range(SLAB_ROWS):
        chunk = lambda k: buf_ref[slot, pl.ds(k * T * SLAB_ROWS + c, T, stride=SLAB_ROWS), :]
        yc = wb[0] * chunk(0)
        for k in range(1, TOP_K):
            yc = yc + wb[k] * chunk(k)
        ys.append(yc)
    y = jnp.concatenate(ys, axis=1)
    out_ref[...] = _layer_norm(DEEPNORM_ALPHA * x_ref[...] + y, g_ref[...], b_ref[...])


def _combine(x2, dest_flat, wt_tok, ys, g, b):
    N, D = x2.shape
    T = T_COMBINE
    n = N // T
    vec = pl.BlockSpec((1, D), lambda i: (0, 0))
    smem = lambda f: pl.BlockSpec((TOP_K * T,), f, memory_space=pltpu.SMEM)
    return pl.pallas_call(
        _combine_kernel,
        grid=(n,),
        in_specs=[smem(lambda i: (i,)), smem(lambda i: (jnp.minimum(i + 1, n - 1),)),
                  pl.BlockSpec((T, D), lambda i: (i, 0)),
                  pl.BlockSpec((T, TOP_K), lambda i: (i, 0)),
                  pl.BlockSpec(memory_space=pl.ANY), vec, vec],
        out_specs=pl.BlockSpec((T, D), lambda i: (i, 0)),
        out_shape=jax.ShapeDtypeStruct((N, D), F32),
        scratch_shapes=[pltpu.VMEM((2, TOP_K * T * SLAB_ROWS, 128), F32), pltpu.SemaphoreType.DMA((2,))],
        compiler_params=_cparams("arbitrary"),
        name="moe_combine_ln3",
    )(dest_flat, dest_flat, x2, wt_tok, ys, g.reshape(1, D), b.reshape(1, D))


def _tile_major(a, T):
    K, N = a.shape
    return a.reshape(K, N // T, T).transpose(1, 0, 2).reshape(N * K)


def _moe(x2, router_w, router_b, w_gate, b_gate, w_up, b_up, w_down, b_down, ln_g, ln_b):
    N, D = x2.shape
    idx, wt = _router(x2, router_w, router_b)
    rank, cnt = _ranks(idx)
    counts = cnt[:, 0]
    blocks = (counts + BM - 1) // BM
    blk_end = jnp.cumsum(blocks)
    blk_start = blk_end - blocks
    dest = _dests(idx, rank, (blk_start * BM).astype(jnp.int32))
    n_blocks = -(-N * TOP_K // BM) + N_EXPERTS
    bi = jnp.arange(n_blocks, dtype=jnp.int32)
    be_raw = jnp.sum(bi[:, None] >= blk_end[None, :], axis=1).astype(jnp.int32)
    last_e = jnp.max(jnp.where(counts > 0, jnp.arange(N_EXPERTS), 0)).astype(jnp.int32)
    block_e = jnp.minimum(be_raw, last_e)
    in_use = bi < blk_end[-1]
    block_valid = jnp.where(in_use, jnp.clip(counts[block_e] - (bi - blk_start[block_e]) * BM, 0, BM), 0)
    dest8 = dest * SLAB_ROWS
    xs, (wg_bf, wu_bf, wd_bf) = _dispatch(x2, _tile_major(dest8, T_MOVE), n_blocks * BM, [w_gate, w_up, w_down])
    f3 = lambda t: t.astype(F32)[:, None, :]
    ys = _experts(xs, block_e, block_valid.astype(jnp.int32), wg_bf, wu_bf, wd_bf,
                  f3(b_gate), f3(b_up), f3(b_down))
    return _combine(x2, _tile_major(dest8, T_COMBINE), wt.T, ys, ln_g, ln_b)


def kernel(x, mem, w_in, ret_norm_g, w_out, ln1_g, ln1_b, mem_wq, mem_wk, mem_wv, mem_wo, ln2_g, ln2_b,
           router_w, router_b, w_gate, b_gate, w_up, b_up, w_down, b_down, ln3_g, ln3_b):
    B, S, D = x.shape
    assert [d for _, d in DILATED_PATTERNS] == [1, 4, 16] and all(w // d == BAND for w, d in DILATED_PATTERNS)
    for l in range(w_in.shape[0]):
        h, qkv4, qkv16 = _in_proj(x, w_in[l].astype(BF16))
        pats = [_band_attn(h, B, S, 1, N_IN // GROUP),
                _band_attn(qkv4, B, S // 4, 4, 3), _band_attn(qkv16, B, S // 16, 16, 3)]
        ret = _retention(h, ret_norm_g[l])
        x1 = _out_proj([p[0] for p in pats], [p[1] for p in pats], ret, x,
                       w_out[l].astype(BF16), ln1_g[l], ln1_b[l])
        kmem, vmem = _mem_kv(mem, mem_wk[l].astype(BF16), mem_wv[l].astype(BF16))
        x2 = _cross_attn(x1, kmem, vmem, mem_wq[l].astype(BF16), mem_wo[l].astype(BF16),
                         ln2_g[l], ln2_b[l])
        x = _moe(x2, router_w[l], router_b[l], w_gate[l], b_gate[l], w_up[l], b_up[l],
                 w_down[l], b_down[l], ln3_g[l], ln3_b[l]).reshape(B, S, D)
    return x
```
